```python
import jax
import jax.numpy as jnp
from jax import lax
import numpy as np

D_MODEL = 2048
BATCH = 2
SEQ = 4096
DEPTH = 2
DEC_BATCH = 32
DEC_SEQ = 1
PAST_LEN = 8192
PAGE_SIZE = 128

N_A_LAYERS = DEPTH // 2
N_B_LAYERS = DEPTH - N_A_LAYERS
CHUNK = 128
D_A = D_MODEL
A_GROUPS = 16
A_GROUP_DIM = D_A // A_GROUPS
HEAD_DIM = 128
N_HEADS = D_MODEL // HEAD_DIM
N_KV = 2
Q_PER_KV = N_HEADS // N_KV
CMP_STRIDE = 16
CMP_BLOCK = 2 * CMP_STRIDE
SLC_BLOCK = 64
N_SELECT = 16
WINDOW = 512
N_BRANCH = 3
KV_ALIGN = 64
ROT_DIM = HEAD_DIM // 4
ROPE_THETA = 500000.0
Q_BLOCK = 128
D_FF = 4 * D_MODEL
EPS = 1e-6
NEG = -1e30
FORCE_BONUS = 1e4

kernel_name = 'yoco_gmlp_nsa_step'


def rms_norm(x, g):
    xf = x.astype(jnp.float32)
    y = xf * lax.rsqrt(jnp.mean(xf * xf, axis=-1, keepdims=True) + EPS)
    return (y * g.astype(jnp.float32)).astype(x.dtype)


def rope(x, pos):
    inv = ROPE_THETA ** (-jnp.arange(0, ROT_DIM, 2, dtype=jnp.float32) / ROT_DIM)
    ang = pos.astype(jnp.float32)[:, None] * inv[None, :]
    cos = jnp.cos(ang)[None, :, None, :]
    sin = jnp.sin(ang)[None, :, None, :]
    xr = x[..., :ROT_DIM].astype(jnp.float32)
    x1, x2 = xr[..., :ROT_DIM // 2], xr[..., ROT_DIM // 2:]
    rot = jnp.concatenate([x1 * cos - x2 * sin, x2 * cos + x1 * sin], axis=-1).astype(x.dtype)
    return jnp.concatenate([rot, x[..., ROT_DIM:]], axis=-1)


def sqrelu_mlp(x, g, w_up, w_down):
    h = rms_norm(x, g) @ w_up
    return x + jnp.square(jax.nn.relu(h)) @ w_down


def chunk_gmlp(x, g, w_in, v_g, w_s, b_s, w_out):
    B, T, _ = x.shape
    z = jax.nn.gelu(rms_norm(x, g) @ w_in)
    u, v = z[..., :D_A], z[..., D_A:]
    v = rms_norm(v, v_g)
    pad = (-T) % CHUNK
    nc = (T + pad) // CHUNK
    vc = jnp.pad(v, ((0, 0), (0, pad), (0, 0))).reshape(B, nc, CHUNK, A_GROUPS, A_GROUP_DIM)
    causal = jnp.tril(jnp.ones((CHUNK, CHUNK), dtype=bool))
    ws = jnp.where(causal[None], w_s, 0.0)
    mixed = jnp.einsum('gts,bcsgd->bctgd', ws, vc) + b_s.T[None, None, :, :, None]
    mixed = mixed.reshape(B, nc * CHUNK, D_A)[:, :T]
    return x + (u * mixed) @ w_out, v


def shared_kv_rows(h, pos, kv_g, w_kv, k_norm):
    B, T, _ = h.shape
    kv = (rms_norm(h, kv_g) @ w_kv).reshape(B, T, 6, N_KV, HEAD_DIM)
    k_slc = rope(rms_norm(kv[:, :, 2], k_norm[1]), pos)
    k_win = rope(rms_norm(kv[:, :, 4], k_norm[2]), pos)
    paged = jnp.stack([kv[:, :, 0], kv[:, :, 1], k_slc, kv[:, :, 3]], axis=2)
    win = jnp.stack([k_win, kv[:, :, 5]], axis=2)
    return paged, win


def compress(rows, pe, w1, w2):
    B, Tk = rows.shape[:2]
    sub = rows.reshape(B, Tk // CMP_STRIDE, CMP_STRIDE, N_KV, HEAD_DIM)
    first = jnp.einsum('bnrgd,rde->bnge', sub, w1[:CMP_STRIDE])
    second = jnp.einsum('bnrgd,rde->bnge', sub, w1[CMP_STRIDE:])
    bias = jnp.einsum('rd,rde->e', pe, w1)
    h = jax.nn.silu(first[:, :-1] + second[:, 1:] + bias)
    return h @ w2


def nsa_keys(rows, cmp_pe, cmp_w1, cmp_w2, k_norm):
    B, Tp = rows.shape[:2]
    kc = compress(rows[:, :, 0], cmp_pe[0], cmp_w1[0], cmp_w2[0])
    vc = compress(rows[:, :, 1], cmp_pe[1], cmp_w1[1], cmp_w2[1])
    n_cmp = kc.shape[1]
    cmp_pos = jnp.arange(n_cmp) * CMP_STRIDE + CMP_BLOCK - 1
    kc = rope(rms_norm(kc, k_norm[0]), cmp_pos)
    n_slc = Tp // SLC_BLOCK
    kb = rows[:, :, 2].reshape(B, n_slc, SLC_BLOCK, N_KV, HEAD_DIM).transpose(0, 3, 1, 2, 4)
    vb = rows[:, :, 3].reshape(B, n_slc, SLC_BLOCK, N_KV, HEAD_DIM).transpose(0, 3, 1, 2, 4)
    ci = jnp.arange(n_cmp)[:, None] * CMP_STRIDE
    sj = jnp.arange(n_slc)[None, :] * SLC_BLOCK
    overlap = ((ci < sj + SLC_BLOCK) & (ci + CMP_BLOCK > sj)).astype(jnp.float32)
    return kc, vc, cmp_pos, kb, vb, overlap


def nsa_queries(h, pos, g, w_in, q_g):
    B, T, _ = h.shape
    z = rms_norm(h, g) @ w_in
    q = z[..., :N_HEADS * HEAD_DIM].reshape(B, T, N_HEADS, HEAD_DIM)
    q = rope(rms_norm(q, q_g), pos)
    gates = jax.nn.sigmoid(z[..., N_HEADS * HEAD_DIM:].astype(jnp.float32)).astype(h.dtype)
    return q, gates.reshape(B, T, N_HEADS, N_BRANCH)


def nsa_attend(q, qpos, gates, kc, vc, cmp_pos, kb, vb, overlap, kw, vw, kw_pos):
    B, Tq = q.shape[:2]
    scale = HEAD_DIM ** -0.5
    qg = q.reshape(B, Tq, N_KV, Q_PER_KV, HEAD_DIM)
    t = qpos[:, None]
    s_c = jnp.einsum('bqgrd,bngd->bqgrn', qg, kc).astype(jnp.float32) * scale
    m_c = (cmp_pos[None, :] <= t)[None, :, None, None, :]
    p_c = jnp.where(m_c, jax.nn.softmax(jnp.where(m_c, s_c, NEG), axis=-1), 0.0)
    o_c = jnp.einsum('bqgrn,bngd->bqgrd', p_c.astype(vc.dtype), vc)
    imp = jnp.einsum('bqgn,ns->bqgs', p_c.sum(axis=3), overlap)
    n_slc = kb.shape[2]
    blk = jnp.arange(n_slc)[None, :]
    cur = t // SLC_BLOCK
    causal = (blk * SLC_BLOCK <= t)[None, :, None, :]
    forced = ((blk == 0) | (blk == cur) | (blk == cur - 1))[None, :, None, :]
    score = jnp.where(causal, imp + jnp.where(forced, FORCE_BONUS, 0.0), NEG)
    k_sel = min(N_SELECT, n_slc)
    vals, idx = lax.top_k(score, k_sel)
    sel_ok = vals > 0.5 * NEG
    idx_t = idx.transpose(0, 2, 1, 3).reshape(B, N_KV, Tq * k_sel)
    bi = jnp.arange(B)[:, None, None]
    gi = jnp.arange(N_KV)[None, :, None]
    ks = kb[bi, gi, idx_t].reshape(B, N_KV, Tq, k_sel, SLC_BLOCK, HEAD_DIM)
    vs = vb[bi, gi, idx_t].reshape(B, N_KV, Tq, k_sel, SLC_BLOCK, HEAD_DIM)
    s_s = jnp.einsum('bqgrd,bgqkld->bqgrkl', qg, ks).astype(jnp.float32) * scale
    tok = idx[..., None] * SLC_BLOCK + jnp.arange(SLC_BLOCK)
    m_s = sel_ok[..., None] & (tok <= qpos[None, :, None, None, None])
    s_s = jnp.where(m_s[:, :, :, None], s_s, NEG).reshape(B, Tq, N_KV, Q_PER_KV, k_sel * SLC_BLOCK)
    p_s = jax.nn.softmax(s_s, axis=-1).reshape(B, Tq, N_KV, Q_PER_KV, k_sel, SLC_BLOCK)
    o_s = jnp.einsum('bqgrkl,bgqkld->bqgrd', p_s.astype(vs.dtype), vs)
    s_w = jnp.einsum('bqgrd,bkgd->bqgrk', qg, kw).astype(jnp.float32) * scale
    dist = t - kw_pos[None, :]
    m_w = ((dist >= 0) & (dist < WINDOW) & (kw_pos[None, :] >= 0))[None, :, None, None, :]
    p_w = jax.nn.softmax(jnp.where(m_w, s_w, NEG), axis=-1)
    o_w = jnp.einsum('bqgrk,bkgd->bqgrd', p_w.astype(vw.dtype), vw)
    g = gates.reshape(B, Tq, N_KV, Q_PER_KV, N_BRANCH)
    out = g[..., 0:1] * o_c + g[..., 1:2] * o_s + g[..., 2:3] * o_w
    return out.reshape(B, Tq, N_HEADS, HEAD_DIM)


def nsa_prompt(q, gates, keys, win):
    B, T = q.shape[:2]
    win_pad = jnp.pad(win, ((0, 0), (WINDOW, 0), (0, 0), (0, 0), (0, 0)))

    def one_block(qb):
        s = qb * Q_BLOCK
        qpos = s + jnp.arange(Q_BLOCK)
        q_b = lax.dynamic_slice_in_dim(q, s, Q_BLOCK, axis=1)
        g_b = lax.dynamic_slice_in_dim(gates, s, Q_BLOCK, axis=1)
        w_b = lax.dynamic_slice_in_dim(win_pad, s, Q_BLOCK + WINDOW, axis=1)
        kw_pos = s - WINDOW + jnp.arange(Q_BLOCK + WINDOW)
        return nsa_attend(q_b, qpos, g_b, *keys, w_b[:, :, 0], w_b[:, :, 1], kw_pos)

    out = lax.map(one_block, jnp.arange(T // Q_BLOCK))
    return out.transpose(1, 0, 2, 3, 4).reshape(B, T, N_HEADS, HEAD_DIM)


def setup_inputs(seed: int = 0) -> dict:
    key = jax.random.key(seed)
    ks = jax.random.split(key, 24)
    n_pages = PAST_LEN // PAGE_SIZE
    n_pool = (5 * DEC_BATCH * n_pages + 3) // 4
    win_keep = min(WINDOW, PAST_LEN)

    def nrm(k, shape, scale):
        return scale * jax.random.normal(k, shape, jnp.float32)

    def gain(k, shape):
        return 1.0 + 0.02 * jax.random.normal(k, shape, jnp.float32)

    page_table = jax.random.permutation(ks[4], n_pool)[:DEC_BATCH * n_pages]
    page_table = page_table.reshape(DEC_BATCH, n_pages).astype(jnp.int32)
    return {
        'x_prompt': nrm(ks[0], (BATCH, SEQ, D_MODEL), 1.0),
        'x_sample': nrm(ks[1], (DEC_BATCH, DEC_SEQ, D_MODEL), 1.0),
        'cache_kv': nrm(ks[2], (n_pool, PAGE_SIZE, 4, N_KV, HEAD_DIM), 1.0),
        'state_kv_win': nrm(ks[3], (DEC_BATCH, win_keep, 2, N_KV, HEAD_DIM), 1.0),
        'page_table': page_table,
        'a_norm': gain(ks[5], (N_A_LAYERS, D_MODEL)),
        'a_w_in': nrm(ks[6], (N_A_LAYERS, D_MODEL, 2 * D_A), D_MODEL ** -0.5),
        'a_v_norm': gain(ks[7], (N_A_LAYERS, D_A)),
        'a_w_s': nrm(ks[8], (N_A_LAYERS, A_GROUPS, CHUNK, CHUNK), CHUNK ** -0.5),
        'a_b_s': gain(ks[9], (N_A_LAYERS, A_GROUPS, CHUNK)),
        'a_w_out': nrm(ks[10], (N_A_LAYERS, D_A, D_MODEL), D_A ** -0.5),
        'mlp_norm': gain(ks[11], (DEPTH, D_MODEL)),
        'mlp_w_up': nrm(ks[12], (DEPTH, D_MODEL, D_FF), D_MODEL ** -0.5),
        'mlp_w_down': nrm(ks[13], (DEPTH, D_FF, D_MODEL), D_FF ** -0.5),
        'kv_norm': gain(ks[14], (D_MODEL,)),
        'w_kv': nrm(ks[15], (D_MODEL, 6 * N_KV * HEAD_DIM), D_MODEL ** -0.5),
        'cmp_pe': nrm(ks[16], (2, CMP_BLOCK, HEAD_DIM), 0.1),
        'cmp_w1': nrm(ks[17], (2, CMP_BLOCK, HEAD_DIM, HEAD_DIM), (CMP_BLOCK * HEAD_DIM) ** -0.5),
        'cmp_w2': nrm(ks[18], (2, HEAD_DIM, HEAD_DIM), HEAD_DIM ** -0.5),
        'k_norm': gain(ks[19], (3, HEAD_DIM)),
        'b_norm': gain(ks[20], (N_B_LAYERS, D_MODEL)),
        'b_w_in': nrm(ks[21], (N_B_LAYERS, D_MODEL, N_HEADS * HEAD_DIM + N_BRANCH * N_HEADS), D_MODEL ** -0.5),
        'b_q_norm': gain(ks[22], (N_B_LAYERS, HEAD_DIM)),
        'b_w_out': nrm(ks[23], (N_B_LAYERS, N_HEADS * HEAD_DIM, D_MODEL), (N_HEADS * HEAD_DIM) ** -0.5),
    }


def reference(x_prompt, x_sample, cache_kv, state_kv_win, page_table, a_norm, a_w_in, a_v_norm,
              a_w_s, a_b_s, a_w_out, mlp_norm, mlp_w_up, mlp_w_down, kv_norm, w_kv, cmp_pe,
              cmp_w1, cmp_w2, k_norm, b_norm, b_w_in, b_q_norm, b_w_out):
    bp, T = x_prompt.shape[:2]
    bd, td = x_sample.shape[:2]
    past_len = page_table.shape[1] * cache_kv.shape[1]
    win_keep_s = state_kv_win.shape[1]
    pos_p = jnp.arange(T)
    pos_s = past_len + jnp.arange(td)
    h_p, h_s = x_prompt, x_sample
    v_rows_s = []
    for layer in range(DEPTH):
        if layer < N_A_LAYERS:
            h_p, _ = chunk_gmlp(h_p, a_norm[layer], a_w_in[layer], a_v_norm[layer],
                                a_w_s[layer], a_b_s[layer], a_w_out[layer])
            h_s, v_s = chunk_gmlp(h_s, a_norm[layer], a_w_in[layer], a_v_norm[layer],
                                  a_w_s[layer], a_b_s[layer], a_w_out[layer])
            v_rows_s.append(v_s)
        else:
            if layer == N_A_LAYERS:
                kv_p, win_p = shared_kv_rows(h_p, pos_p, kv_norm, w_kv, k_norm)
                kv_s, win_s = shared_kv_rows(h_s, pos_s, kv_norm, w_kv, k_norm)
                pad_p = (-T) % KV_ALIGN
                rows_p = jnp.pad(kv_p, ((0, 0), (0, pad_p), (0, 0), (0, 0), (0, 0)))
                keys_p = nsa_keys(rows_p, cmp_pe, cmp_w1, cmp_w2, k_norm)
                past = cache_kv[page_table].reshape(bd, past_len, 4, N_KV, HEAD_DIM)
                pad_s = (-(past_len + td)) % KV_ALIGN
                rows_s = jnp.concatenate(
                    [past, kv_s, jnp.zeros((bd, pad_s, 4, N_KV, HEAD_DIM), past.dtype)], axis=1)
                keys_s = nsa_keys(rows_s, cmp_pe, cmp_w1, cmp_w2, k_norm)
                win_all_s = jnp.concatenate([state_kv_win, win_s], axis=1)
                kw_pos_s = past_len - win_keep_s + jnp.arange(win_keep_s + td)
            j = layer - N_A_LAYERS
            q, g = nsa_queries(h_p, pos_p, b_norm[j], b_w_in[j], b_q_norm[j])
            o = nsa_prompt(q, g, keys_p, win_p)
            h_p = h_p + o.reshape(bp, T, N_HEADS * HEAD_DIM) @ b_w_out[j]
            q, g = nsa_queries(h_s, pos_s, b_norm[j], b_w_in[j], b_q_norm[j])
            o = nsa_attend(q, pos_s, g, *keys_s, win_all_s[:, :, 0], win_all_s[:, :, 1], kw_pos_s)
            h_s = h_s + o.reshape(bd, td, N_HEADS * HEAD_DIM) @ b_w_out[j]
        h_p = sqrelu_mlp(h_p, mlp_norm[layer], mlp_w_up[layer], mlp_w_down[layer])
        h_s = sqrelu_mlp(h_s, mlp_norm[layer], mlp_w_up[layer], mlp_w_down[layer])
    win_new_p = win_p[:, T - min(WINDOW, T):]
    win_new_s = win_all_s[:, td:]
    v_a_s = jnp.stack(v_rows_s, axis=0)
    return (h_p, h_s, kv_p, win_new_p, kv_s, win_new_s, v_a_s)
```

```python
import functools

import jax
import jax.numpy as jnp
from jax import lax
from jax.experimental import pallas as pl
from jax.experimental.pallas import tpu as pltpu

F32 = jnp.float32
BF16 = jnp.bfloat16

CHUNK = 128
A_GROUPS = 16
HEAD_DIM = 128
N_KV = 2
CMP_STRIDE = 16
CMP_BLOCK = 2 * CMP_STRIDE
SLC_BLOCK = 64
N_SELECT = 16
WINDOW = 512
N_BRANCH = 3
KV_ALIGN = 64
ROT_DIM = HEAD_DIM // 4
ROPE_THETA = 500000.0
Q_BLOCK = 128
EPS = 1e-6
NEG = -1e30
FORCE_BONUS = 1e4

LANES = 128
SUBLANES = 8
VMEM_LIMIT_BYTES = 56 * 1024 * 1024

KV_TILE = 512
N_KINDS = 6

_NT = (((1,), (1,)), ((), ()))


def _params(*sem):
    return pltpu.CompilerParams(dimension_semantics=sem, vmem_limit_bytes=VMEM_LIMIT_BYTES)


def _rms(x, g):
    ms = jnp.mean(x * x, axis=-1, keepdims=True)
    return x * lax.rsqrt(ms + EPS) * g


def _rope(x, cos, sin):
    half = ROT_DIM // 2
    lane = lax.broadcasted_iota(jnp.int32, x.shape, 1)
    partner = jnp.where(lane < half, pltpu.roll(x, LANES - half, 1), pltpu.roll(x, half, 1))
    return x * cos + partner * sin


def _rope_tables(pos):
    inv = ROPE_THETA ** (-jnp.arange(0, ROT_DIM, 2, dtype=F32) / ROT_DIM)
    ang = pos.astype(F32)[:, None] * inv[None, :]
    c, s = jnp.cos(ang), jnp.sin(ang)
    n = pos.shape[0]
    pad = LANES - ROT_DIM
    cos = jnp.concatenate([c, c, jnp.ones((n, pad), F32)], axis=1)
    sin = jnp.concatenate([-s, s, jnp.zeros((n, pad), F32)], axis=1)
    return cos, sin


def _gmlp_in_kernel(x_ref, g_ref, w_ref, z_ref, xn_ref):
    @pl.when(pl.program_id(1) == 0)
    def _():
        xn_ref[...] = _rms(x_ref[...], g_ref[...]).astype(BF16)

    z_ref[...] = jax.nn.gelu(jnp.dot(xn_ref[...], w_ref[...], preferred_element_type=F32))


def _gmlp_in(x, g, w, tm, tn):
    m, d = x.shape
    n = w.shape[1]
    return pl.pallas_call(
        _gmlp_in_kernel,
        grid=(m // tm, n // tn),
        in_specs=[
            pl.BlockSpec((tm, d), lambda i, j: (i, 0)),
            pl.BlockSpec((1, d), lambda i, j: (0, 0)),
            pl.BlockSpec((d, tn), lambda i, j: (0, j)),
        ],
        out_specs=pl.BlockSpec((tm, tn), lambda i, j: (i, j)),
        out_shape=jax.ShapeDtypeStruct((m, n), F32),
        scratch_shapes=[pltpu.VMEM((tm, d), BF16)],
        compiler_params=_params("parallel", "arbitrary"),
        name="gmlp_in",
    )(x, g, w)


def _gmlp_out_kernel(x_ref, u_ref, v_ref, vg_ref, ws_ref, bt_ref, wo_ref, o_ref, vn_ref, y_ref):
    tm = x_ref.shape[0]
    vn_ref[...] = _rms(v_ref[...], vg_ref[...]).astype(BF16)
    row = lax.broadcasted_iota(jnp.int32, (CHUNK, CHUNK), 0)
    col = lax.broadcasted_iota(jnp.int32, (CHUNK, CHUNK), 1)
    causal = row >= col
    for g in range(A_GROUPS):
        wsg = jnp.where(causal, ws_ref[g], 0.0).astype(BF16)
        bias = bt_ref[:, g:g + 1]
        cs = slice(g * LANES, (g + 1) * LANES)
        for c in range(tm // CHUNK):
            rs = slice(c * CHUNK, (c + 1) * CHUNK)
            mixed = jnp.dot(wsg, vn_ref[rs, cs], preferred_element_type=F32) + bias
            y_ref[rs, cs] = (u_ref[rs, cs] * mixed).astype(BF16)
    o_ref[...] = x_ref[...] + jnp.dot(y_ref[...], wo_ref[...], preferred_element_type=F32)


def _gmlp_out(x, z, vg, ws, bt, wo, tm):
    m, d = x.shape
    return pl.pallas_call(
        _gmlp_out_kernel,
        grid=(m // tm,),
        in_specs=[
            pl.BlockSpec((tm, d), lambda i: (i, 0)),
            pl.BlockSpec((tm, d), lambda i: (i, 0)),
            pl.BlockSpec((tm, d), lambda i: (i, 1)),
            pl.BlockSpec((1, d), lambda i: (0, 0)),
            pl.BlockSpec((A_GROUPS, CHUNK, CHUNK), lambda i: (0, 0, 0)),
            pl.BlockSpec((CHUNK, A_GROUPS), lambda i: (0, 0)),
            pl.BlockSpec((d, d), lambda i: (0, 0)),
        ],
        out_specs=pl.BlockSpec((tm, d), lambda i: (i, 0)),
        out_shape=jax.ShapeDtypeStruct((m, d), F32),
        scratch_shapes=[pltpu.VMEM((tm, d), BF16), pltpu.VMEM((tm, d), BF16)],
        compiler_params=_params("parallel"),
        name="gmlp_out",
    )(x, z, z, vg, ws, bt, wo)


def _gmlp_out_single_kernel(x_ref, u_ref, v_ref, vg_ref, wd_ref, bb_ref, wo_ref, o_ref, vn_ref):
    vn = _rms(v_ref[...], vg_ref[...])
    vn_ref[...] = vn
    mixed = vn * wd_ref[...] + bb_ref[...]
    y = (u_ref[...] * mixed).astype(BF16)
    o_ref[...] = x_ref[...] + jnp.dot(y, wo_ref[...], preferred_element_type=F32)


def _gmlp_out_single(x, z, vg, wd, bb, wo):
    m, d = x.shape
    full = lambda i: (0, 0)
    return pl.pallas_call(
        _gmlp_out_single_kernel,
        grid=(1,),
        in_specs=[
            pl.BlockSpec((m, d), full),
            pl.BlockSpec((m, d), lambda i: (0, 0)),
            pl.BlockSpec((m, d), lambda i: (0, 1)),
            pl.BlockSpec((1, d), full),
            pl.BlockSpec((1, d), full),
            pl.BlockSpec((1, d), full),
            pl.BlockSpec((d, d), full),
        ],
        out_specs=[pl.BlockSpec((m, d), full), pl.BlockSpec((m, d), full)],
        out_shape=[jax.ShapeDtypeStruct((m, d), F32), jax.ShapeDtypeStruct((m, d), F32)],
        compiler_params=_params("arbitrary"),
        name="gmlp_out_single",
    )(x, z, z, vg, wd, bb, wo)


def _mlp_kernel(x_ref, g_ref, wu_ref, wd_ref, o_ref, xn_ref):
    @pl.when(pl.program_id(1) == 0)
    def _():
        x = x_ref[...]
        xn_ref[...] = _rms(x, g_ref[...]).astype(BF16)
        o_ref[...] = x

    h = jnp.dot(xn_ref[...], wu_ref[...], preferred_element_type=F32)
    a = jnp.square(jnp.maximum(h, 0.0)).astype(BF16)
    o_ref[...] += jnp.dot(a, wd_ref[...], preferred_element_type=F32)


def _mlp(x, g, wu, wd, tm, tf):
    m, d = x.shape
    f = wu.shape[1]
    return pl.pallas_call(
        _mlp_kernel,
        grid=(m // tm, f // tf),
        in_specs=[
            pl.BlockSpec((tm, d), lambda i, j: (i, 0)),
            pl.BlockSpec((1, d), lambda i, j: (0, 0)),
            pl.BlockSpec((d, tf), lambda i, j: (0, j)),
            pl.BlockSpec((tf, d), lambda i, j: (j, 0)),
        ],
        out_specs=pl.BlockSpec((tm, d), lambda i, j: (i, 0)),
        out_shape=jax.ShapeDtypeStruct((m, d), F32),
        scratch_shapes=[pltpu.VMEM((tm, d), BF16)],
        compiler_params=_params("parallel", "arbitrary"),
        name="mlp",
    )(x, g, wu, wd)


def _kv_proj_kernel(x_ref, g_ref, w_ref, kn_ref, cos_ref, sin_ref, paged_ref, win_ref, kvb_ref):
    xn = _rms(x_ref[...], g_ref[...]).astype(BF16)
    kv = jnp.dot(xn, w_ref[...], preferred_element_type=F32)
    cos, sin = cos_ref[...], sin_ref[...]
    for c in range(N_KINDS * N_KV):
        kind = c // N_KV
        h = kv[:, c * LANES:(c + 1) * LANES]
        if kind == 2:
            h = _rope(_rms(h, kn_ref[1:2, :]), cos, sin)
        elif kind == 4:
            h = _rope(_rms(h, kn_ref[2:3, :]), cos, sin)
        if kind < 4:
            paged_ref[:, c * LANES:(c + 1) * LANES] = h
        else:
            win_ref[:, (c - 4 * N_KV) * LANES:(c - 4 * N_KV + 1) * LANES] = h
        kvb_ref[c] = h.astype(BF16)


def _kv_proj(x, g, w, kn, cos, sin, tm):
    m, d = x.shape
    n = w.shape[1]
    n_heads = N_KINDS * N_KV
    return pl.pallas_call(
        _kv_proj_kernel,
        grid=(m // tm,),
        in_specs=[
            pl.BlockSpec((tm, d), lambda i: (i, 0)),
            pl.BlockSpec((1, d), lambda i: (0, 0)),
            pl.BlockSpec((d, n), lambda i: (0, 0)),
            pl.BlockSpec((3, HEAD_DIM), lambda i: (0, 0)),
            pl.BlockSpec((tm, LANES), lambda i: (i, 0)),
            pl.BlockSpec((tm, LANES), lambda i: (i, 0)),
        ],
        out_specs=[
            pl.BlockSpec((tm, 4 * N_KV * HEAD_DIM), lambda i: (i, 0)),
            pl.BlockSpec((tm, 2 * N_KV * HEAD_DIM), lambda i: (i, 0)),
            pl.BlockSpec((n_heads, tm, HEAD_DIM), lambda i: (0, i, 0)),
        ],
        out_shape=[
            jax.ShapeDtypeStruct((m, 4 * N_KV * HEAD_DIM), F32),
            jax.ShapeDtypeStruct((m, 2 * N_KV * HEAD_DIM), F32),
            jax.ShapeDtypeStruct((n_heads, m, HEAD_DIM), BF16),
        ],
        compiler_params=_params("parallel"),
        name="kv_proj",
    )(x, g, w, kn, cos, sin)


def _q_proj_kernel(x_ref, g_ref, wq_ref, wg_ref, qn_ref, cos_ref, sin_ref, q_ref, gate_ref):
    xn = _rms(x_ref[...], g_ref[...]).astype(BF16)
    z = jnp.dot(xn, wq_ref[...], preferred_element_type=F32)
    cos, sin = cos_ref[...], sin_ref[...]
    qn = qn_ref[...]
    scale = HEAD_DIM ** -0.5
    for h in range(z.shape[1] // HEAD_DIM):
        cs = slice(h * HEAD_DIM, (h + 1) * HEAD_DIM)
        qh = _rope(_rms(z[:, cs], qn), cos, sin) * scale
        q_ref[:, cs] = qh.astype(BF16)
    gate_ref[...] = jax.nn.sigmoid(jnp.dot(xn, wg_ref[...], preferred_element_type=F32))


def _q_proj(x, g, wq, wg, qn, cos, sin, tm):
    m, d = x.shape
    n = wq.shape[1]
    return pl.pallas_call(
        _q_proj_kernel,
        grid=(m // tm,),
        in_specs=[
            pl.BlockSpec((tm, d), lambda i: (i, 0)),
            pl.BlockSpec((1, d), lambda i: (0, 0)),
            pl.BlockSpec((d, n), lambda i: (0, 0)),
            pl.BlockSpec((d, LANES), lambda i: (0, 0)),
            pl.BlockSpec((1, HEAD_DIM), lambda i: (0, 0)),
            pl.BlockSpec((tm, LANES), lambda i: (i, 0)),
            pl.BlockSpec((tm, LANES), lambda i: (i, 0)),
        ],
        out_specs=[
            pl.BlockSpec((tm, n), lambda i: (i, 0)),
            pl.BlockSpec((tm, LANES), lambda i: (i, 0)),
        ],
        out_shape=[
            jax.ShapeDtypeStruct((m, n), BF16),
            jax.ShapeDtypeStruct((m, LANES), F32),
        ],
        compiler_params=_params("parallel"),
        name="q_proj",
    )(x, g, wq, wg, qn, cos, sin)


def _out_proj_kernel(h_ref, o_ref, w_ref, y_ref):
    y_ref[...] = h_ref[...] + jnp.dot(o_ref[...], w_ref[...], preferred_element_type=F32)


def _out_proj(h, o, w, tm):
    m, d = h.shape
    k = o.shape[1]
    return pl.pallas_call(
        _out_proj_kernel,
        grid=(m // tm,),
        in_specs=[
            pl.BlockSpec((tm, d), lambda i: (i, 0)),
            pl.BlockSpec((tm, k), lambda i: (i, 0)),
            pl.BlockSpec((k, d), lambda i: (0, 0)),
        ],
        out_specs=pl.BlockSpec((tm, d), lambda i: (i, 0)),
        out_shape=jax.ShapeDtypeStruct((m, d), F32),
        compiler_params=_params("parallel"),
        name="out_proj",
    )(h, o, w)


def _cmp_bias(pe_ref, w1f_ref, kind):
    return jnp.dot(pe_ref[kind], w1f_ref[kind], preferred_element_type=F32)[0:1, :]


def _cmp_finish(fs, bias, w2, n_rows):
    first = fs[:, :HEAD_DIM]
    second = pltpu.roll(fs[:, HEAD_DIM:], n_rows - 1, 0)
    h = first + second + bias
    h = h * jax.nn.sigmoid(h)
    return jnp.dot(h.astype(BF16), w2, preferred_element_type=F32)


def _compress_prompt_kernel(krows_ref, vrows_ref, w1_ref, pe_ref, w1f_ref, w2_ref, kn_ref,
                            cos_ref, sin_ref, kc_ref, vc_ref):
    n_sub = krows_ref.shape[0] // CMP_STRIDE
    rowi = lax.broadcasted_iota(jnp.int32, (n_sub, HEAD_DIM), 0)
    for kind, rows_ref, out_ref in ((0, krows_ref, kc_ref), (1, vrows_ref, vc_ref)):
        fs = jnp.zeros((n_sub, 2 * HEAD_DIM), F32)
        for r in range(CMP_STRIDE):
            xr = rows_ref[pl.ds(r, n_sub, stride=CMP_STRIDE), :].astype(BF16)
            fs = fs + jnp.dot(xr, w1_ref[kind, r], preferred_element_type=F32)
        out = _cmp_finish(fs, _cmp_bias(pe_ref, w1f_ref, kind), w2_ref[kind], n_sub)
        if kind == 0:
            out = _rope(_rms(out, kn_ref[0:1, :]), cos_ref[...], sin_ref[...])
        out_ref[...] = jnp.where(rowi < n_sub - 1, out, 0.0).astype(BF16)


def _compress_prompt(paged, w1cat, pe8, w1f, w2, kn, cos, sin, b, t):
    n_sub = t // CMP_STRIDE
    full = lambda nd: (lambda i, g: (0,) * nd)
    out_spec = pl.BlockSpec((None, None, n_sub, HEAD_DIM), lambda i, g: (i, g, 0, 0))
    out_shape = jax.ShapeDtypeStruct((b, N_KV, n_sub, HEAD_DIM), BF16)
    return pl.pallas_call(
        _compress_prompt_kernel,
        grid=(b, N_KV),
        in_specs=[
            pl.BlockSpec((t, HEAD_DIM), lambda i, g: (i, g)),
            pl.BlockSpec((t, HEAD_DIM), lambda i, g: (i, N_KV + g)),
            pl.BlockSpec(w1cat.shape, full(4)),
            pl.BlockSpec(pe8.shape, full(3)),
            pl.BlockSpec(w1f.shape, full(3)),
            pl.BlockSpec(w2.shape, full(3)),
            pl.BlockSpec(kn.shape, full(2)),
            pl.BlockSpec(cos.shape, full(2)),
            pl.BlockSpec(sin.shape, full(2)),
        ],
        out_specs=[out_spec, out_spec],
        out_shape=[out_shape, out_shape],
        compiler_params=_params("parallel", "parallel"),
        name="compress_prompt",
    )(paged, paged, w1cat, pe8, w1f, w2, kn, cos, sin)


def _split_bf16(x):
    hi = x.astype(BF16)
    lo = (x - hi.astype(F32)).astype(BF16)
    return hi, lo


def _overlap_matrix(n_cmp, n_slc, rows, cols):
    ci = jnp.arange(rows)[:, None] * CMP_STRIDE
    sj = jnp.arange(cols)[None, :] * SLC_BLOCK
    ov = (ci < sj + SLC_BLOCK) & (ci + CMP_BLOCK > sj)
    ov = ov & (jnp.arange(rows)[:, None] < n_cmp) & (jnp.arange(cols)[None, :] < n_slc)
    return ov.astype(BF16)


def _softmax_rows(s, valid):
    s = jnp.where(valid, s, NEG)
    e = jnp.exp(s - jnp.max(s, axis=-1, keepdims=True))
    return e / jnp.sum(e, axis=-1, keepdims=True)


def _attn_prompt_kernel(q_ref, gate_ref, kc_ref, vc_ref, ks_ref, vs_ref, kw_ref, vw_ref,
                        ovt_ref, exp_ref, o_ref, m_ref, l_ref, acc_ref):
    qb = pl.program_id(2)
    g = pl.program_id(1)
    nq = Q_BLOCK
    rep = q_ref.shape[1] // HEAD_DIM
    n_cmp_pad = kc_ref.shape[0]
    n_slc = ovt_ref.shape[0]
    t_len = ks_ref.shape[0]
    s0 = qb * nq

    q_all = q_ref[...]
    q2 = jnp.concatenate([q_all[:, r * HEAD_DIM:(r + 1) * HEAD_DIM] for r in range(rep)], axis=0)
    t_col = s0 + lax.broadcasted_iota(jnp.int32, (nq, 1), 0)

    kc = kc_ref[...]
    vc = vc_ref[...]
    cpos = lax.broadcasted_iota(jnp.int32, (1, n_cmp_pad), 1) * CMP_STRIDE + (CMP_BLOCK - 1)
    m_c = cpos <= t_col
    s_c = lax.dot_general(q2, kc, _NT, preferred_element_type=F32)
    psum = jnp.zeros((nq, n_cmp_pad), F32)
    p_parts = []
    for r in range(rep):
        p = jnp.where(m_c, _softmax_rows(s_c[r * nq:(r + 1) * nq], m_c), 0.0)
        psum = psum + p
        p_parts.append(p.astype(BF16))
    o_c = jnp.dot(jnp.concatenate(p_parts, axis=0), vc, preferred_element_type=F32)

    ovt = ovt_ref[...]
    p_hi, p_lo = _split_bf16(psum)
    imp_t = (lax.dot_general(ovt, p_hi, _NT, preferred_element_type=F32)
             + lax.dot_general(ovt, p_lo, _NT, preferred_element_type=F32))
    t_row = s0 + lax.broadcasted_iota(jnp.int32, (n_slc, nq), 1)
    blk = lax.broadcasted_iota(jnp.int32, (n_slc, nq), 0)
    cur = t_row // SLC_BLOCK
    causal = blk * SLC_BLOCK <= t_row
    forced = (blk == 0) | (blk == cur) | (blk == cur - 1)
    score = jnp.where(causal, imp_t + jnp.where(forced, FORCE_BONUS, 0.0), NEG)
    rank = jnp.zeros((n_slc, nq), jnp.int32)
    for i in range(n_slc):
        si = score[i:i + 1, :]
        later = (blk > i).astype(jnp.int32)
        rank = rank + jnp.where(si > score, 1, 0) + jnp.where(si == score, later, 0)
    sel_t = jnp.where(causal & (rank < N_SELECT), 1.0, 0.0)
    sel_t = jnp.concatenate([sel_t, jnp.zeros((LANES - n_slc, nq), F32)], axis=0)
    sel = sel_t.T.astype(BF16)

    m_ref[...] = jnp.full(m_ref.shape, NEG, F32)
    l_ref[...] = jnp.zeros(l_ref.shape, F32)
    acc_ref[...] = jnp.zeros(acc_ref.shape, F32)
    n_tiles = (s0 + nq + KV_TILE - 1) // KV_TILE

    def tile_body(j, carry):
        off = pl.multiple_of(j * KV_TILE, KV_TILE)
        k = ks_ref[pl.ds(off, KV_TILE), :]
        v = vs_ref[pl.ds(off, KV_TILE), :]
        picked = jnp.dot(sel, exp_ref[:, pl.ds(off, KV_TILE)], preferred_element_type=F32)
        kpos = off + lax.broadcasted_iota(jnp.int32, (nq, KV_TILE), 1)
        bias = jnp.where((picked > 0.5) & (kpos <= t_col), 0.0, NEG)
        s = lax.dot_general(q2, k, _NT, preferred_element_type=F32)
        p_parts = []
        for r in range(rep):
            rs = slice(r * nq, (r + 1) * nq)
            sr = s[rs] + bias
            m_old = m_ref[rs]
            m_new = jnp.maximum(m_old, jnp.max(sr, axis=-1, keepdims=True))
            pr = jnp.exp(sr - m_new)
            alpha = jnp.exp(m_old - m_new)
            l_ref[rs] = alpha * l_ref[rs] + jnp.sum(pr, axis=-1, keepdims=True)
            acc_ref[rs] = alpha * acc_ref[rs]
            m_ref[rs] = m_new
            p_parts.append(pr.astype(BF16))
        acc_ref[...] += jnp.dot(jnp.concatenate(p_parts, axis=0), v, preferred_element_type=F32)
        return carry

    lax.fori_loop(0, n_tiles, tile_body, 0)
    o_s = acc_ref[...] / l_ref[...]

    slab = nq + WINDOW
    w0 = pl.multiple_of(jnp.maximum(s0 - WINDOW, 0), nq)
    kw = kw_ref[pl.ds(w0, slab), :]
    vw = vw_ref[pl.ds(w0, slab), :]
    dist = t_col - (w0 + lax.broadcasted_iota(jnp.int32, (nq, slab), 1))
    m_w = (dist >= 0) & (dist < WINDOW)
    s_w = lax.dot_general(q2, kw, _NT, preferred_element_type=F32)
    p_parts = [_softmax_rows(s_w[r * nq:(r + 1) * nq], m_w).astype(BF16) for r in range(rep)]
    o_w = jnp.dot(jnp.concatenate(p_parts, axis=0), vw, preferred_element_type=F32)

    gates = gate_ref[...]
    for r in range(rep):
        rs = slice(r * nq, (r + 1) * nq)
        c0 = (g * rep + r) * N_BRANCH
        lane = lax.broadcasted_iota(jnp.int32, gates.shape, 1)
        gsel = [jnp.sum(jnp.where(lane == c0 + br, gates, 0.0), axis=-1, keepdims=True)
                for br in range(N_BRANCH)]
        out = gsel[0] * o_c[rs] + gsel[1] * o_s[rs] + gsel[2] * o_w[rs]
        o_ref[:, r * HEAD_DIM:(r + 1) * HEAD_DIM] = out.astype(BF16)


def _attn_prompt(q, gates, kc, vc, kvb, ovt, expand, b, t):
    nqb = t // Q_BLOCK
    rep = q.shape[1] // HEAD_DIM // N_KV
    n_cmp_pad = kc.shape[2]
    rows = lambda kind: pl.BlockSpec((None, t, HEAD_DIM), lambda i, g, j: (kind * N_KV + g, i, 0))
    cmp_spec = pl.BlockSpec((None, None, n_cmp_pad, HEAD_DIM), lambda i, g, j: (i, g, 0, 0))
    return pl.pallas_call(
        _attn_prompt_kernel,
        grid=(b, N_KV, nqb),
        in_specs=[
            pl.BlockSpec((Q_BLOCK, rep * HEAD_DIM), lambda i, g, j: (i * nqb + j, g)),
            pl.BlockSpec((Q_BLOCK, LANES), lambda i, g, j: (i * nqb + j, 0)),
            cmp_spec, cmp_spec,
            rows(2), rows(3), rows(4), rows(5),
            pl.BlockSpec(ovt.shape, lambda i, g, j: (0, 0)),
            pl.BlockSpec(expand.shape, lambda i, g, j: (0, 0)),
        ],
        out_specs=pl.BlockSpec((Q_BLOCK, rep * HEAD_DIM), lambda i, g, j: (i * nqb + j, g)),
        out_shape=jax.ShapeDtypeStruct(q.shape, BF16),
        scratch_shapes=[
            pltpu.VMEM((rep * Q_BLOCK, 1), F32),
            pltpu.VMEM((rep * Q_BLOCK, 1), F32),
            pltpu.VMEM((rep * Q_BLOCK, HEAD_DIM), F32),
        ],
        compiler_params=_params("parallel", "parallel", "arbitrary"),
        name="attn_prompt",
    )(q, gates, kc, vc, kvb, kvb, kvb, kvb, ovt, expand)


def _sample_cmp_kernel(pt_ref, *refs, pages_per_step, n_steps, t_pos, n_slc):
    del pt_ref
    page_refs = refs[:pages_per_step]
    (new_ref, w1_ref, pe_ref, w1f_ref, w2_ref, kn_ref, cos_ref, sin_ref, q_ref, ov_ref,
     oc_ref, idx_ref, fs_ref, stage_ref) = refs[pages_per_step:]
    step = pl.program_id(1)
    n_heads = N_KV * 2
    sub_per_page = page_refs[0].shape[0] // CMP_STRIDE
    rows_per_step = pages_per_step * sub_per_page
    n_pad = fs_ref.shape[1]
    n_past = n_steps * rows_per_step
    row0 = pl.multiple_of(step * rows_per_step, rows_per_step)

    page_rows = page_refs[0].shape[0]
    for c in range(n_heads):
        fs_ref[c, pl.ds(row0, rows_per_step), :] = jnp.zeros((rows_per_step, 2 * HEAD_DIM), F32)
        for k, p in enumerate(page_refs):
            stage_ref[c, k * page_rows:(k + 1) * page_rows, :] = p[:, c * HEAD_DIM:(c + 1) * HEAD_DIM]

    def r_body(r, carry):
        for c in range(n_heads):
            xr = stage_ref[c, pl.ds(r, rows_per_step, stride=CMP_STRIDE), :]
            fs_ref[c, pl.ds(row0, rows_per_step), :] += jnp.dot(
                xr.astype(BF16), w1_ref[c // N_KV, r], preferred_element_type=F32)
        return carry

    lax.fori_loop(0, CMP_STRIDE, r_body, 0)

    @pl.when(step == n_steps - 1)
    def _():
        tail = n_pad - n_past
        row_t = lax.broadcasted_iota(jnp.int32, (tail, HEAD_DIM), 0)
        rowi = lax.broadcasted_iota(jnp.int32, (n_pad, HEAD_DIM), 0)
        n_sub = (t_pos + 1 + KV_ALIGN - 1) // KV_ALIGN * KV_ALIGN // CMP_STRIDE
        outs = []
        for c in range(n_heads):
            kind = c // N_KV
            new_row = new_ref[:, c * HEAD_DIM:(c + 1) * HEAD_DIM]
            x_tail = jnp.where(row_t == 0, new_row, 0.0).astype(BF16)
            fs_ref[c, n_past:n_pad, :] = jnp.dot(x_tail, w1_ref[kind, 0], preferred_element_type=F32)
            out = _cmp_finish(fs_ref[c], _cmp_bias(pe_ref, w1f_ref, kind), w2_ref[kind], n_pad)
            if kind == 0:
                out = _rope(_rms(out, kn_ref[0:1, :]), cos_ref[...], sin_ref[...])
            outs.append(jnp.where(rowi < n_sub - 1, out, 0.0).astype(BF16))

        q = q_ref[...]
        n_q = q.shape[0]
        rep = n_q // N_KV
        head = lax.broadcasted_iota(jnp.int32, (n_q, 1), 0)
        cpos = lax.broadcasted_iota(jnp.int32, (1, n_pad), 1) * CMP_STRIDE + (CMP_BLOCK - 1)
        m_c = cpos <= t_pos
        lanes = ov_ref.shape[1]
        blk = lax.broadcasted_iota(jnp.int32, (1, lanes), 1)
        cur = t_pos // SLC_BLOCK
        causal = (blk * SLC_BLOCK <= t_pos) & (blk < n_slc)
        forced = (blk == 0) | (blk == cur) | (blk == cur - 1)
        eye_i = lax.broadcasted_iota(jnp.int32, (lanes, lanes), 0)
        eye_j = lax.broadcasted_iota(jnp.int32, (lanes, lanes), 1)
        o_c = jnp.zeros((n_q, HEAD_DIM), F32)
        for g in range(N_KV):
            kc, vc = outs[g], outs[N_KV + g]
            in_group = (head // rep) == g
            s = lax.dot_general(q, kc, _NT, preferred_element_type=F32)
            p = jnp.where(m_c, _softmax_rows(s, m_c), 0.0)
            o_g = jnp.dot(p.astype(BF16), vc, preferred_element_type=F32)
            o_c = jnp.where(in_group, o_g, o_c)
            psum = jnp.sum(jnp.where(in_group, p, 0.0), axis=0, keepdims=True)
            psum8 = jnp.broadcast_to(psum, (SUBLANES, n_pad))
            p_hi, p_lo = _split_bf16(psum8)
            imp = (jnp.dot(p_hi, ov_ref[...], preferred_element_type=F32)
                   + jnp.dot(p_lo, ov_ref[...], preferred_element_type=F32))[0:1, :]
            score = jnp.where(causal, imp + jnp.where(forced, FORCE_BONUS, 0.0), NEG)
            score_b = jnp.broadcast_to(score, (lanes, lanes))
            score_col = jnp.sum(jnp.where(eye_i == eye_j, score_b, 0.0), axis=1, keepdims=True)
            beats = (score_col > score_b) | ((score_col == score_b) & (eye_i < eye_j))
            rank = jnp.sum(jnp.where(beats, 1.0, 0.0), axis=0, keepdims=True)
            sel = jnp.where(causal & (rank < N_SELECT), 1.0, 0.0)
            sel_b = jnp.broadcast_to(sel, (lanes, lanes))
            sel_col = jnp.sum(jnp.where(eye_i == eye_j, sel_b, 0.0), axis=1, keepdims=True)
            slot = jnp.sum(jnp.where(eye_i < eye_j, sel_col, 0.0), axis=0, keepdims=True)
            slot_b = jnp.broadcast_to(slot, (N_SELECT, lanes))
            k_i = lax.broadcasted_iota(jnp.int32, (N_SELECT, lanes), 0).astype(F32)
            j_i = lax.broadcasted_iota(jnp.int32, (N_SELECT, lanes), 1).astype(F32)
            hit = (slot_b == k_i) & (jnp.broadcast_to(sel, (N_SELECT, lanes)) > 0.5)
            idx = jnp.sum(jnp.where(hit, j_i, 0.0), axis=1, keepdims=True)
            idx_ref[g] = jnp.broadcast_to(idx, (N_SELECT, LANES)).astype(jnp.int32)
        oc_ref[...] = o_c


def _sample_cmp(page_table, cache3, new_rows, w1cat, pe8, w1f, w2, kn, cos, sin, q3, ov,
                pages_per_step, t_pos, n_slc):
    bd, n_pages = page_table.shape
    n_steps = n_pages // pages_per_step
    page_rows = cache3.shape[1]
    half_cols = N_KV * 2 * HEAD_DIM
    n_pad = cos.shape[0]
    n_q = q3.shape[1]
    full = lambda nd: (lambda i, s, pt: (0,) * nd)

    def page_spec(k):
        return pl.BlockSpec((None, page_rows, half_cols),
                            lambda i, s, pt: (pt[i, s * pages_per_step + k], 0, 0))

    grid_spec = pltpu.PrefetchScalarGridSpec(
        num_scalar_prefetch=1,
        grid=(bd, n_steps),
        in_specs=[page_spec(k) for k in range(pages_per_step)] + [
            pl.BlockSpec((None, 1, new_rows.shape[2]), lambda i, s, pt: (i, 0, 0)),
            pl.BlockSpec(w1cat.shape, full(4)),
            pl.BlockSpec(pe8.shape, full(3)),
            pl.BlockSpec(w1f.shape, full(3)),
            pl.BlockSpec(w2.shape, full(3)),
            pl.BlockSpec(kn.shape, full(2)),
            pl.BlockSpec(cos.shape, full(2)),
            pl.BlockSpec(sin.shape, full(2)),
            pl.BlockSpec((None, n_q, HEAD_DIM), lambda i, s, pt: (i, 0, 0)),
            pl.BlockSpec(ov.shape, full(2)),
        ],
        out_specs=[
            pl.BlockSpec((None, n_q, HEAD_DIM), lambda i, s, pt: (i, 0, 0)),
            pl.BlockSpec((None, N_KV, N_SELECT, LANES), lambda i, s, pt: (i, 0, 0, 0)),
        ],
        scratch_shapes=[pltpu.VMEM((N_KV * 2, n_pad, 2 * HEAD_DIM), F32),
                        pltpu.VMEM((N_KV * 2, pages_per_step * page_rows, HEAD_DIM), F32)],
    )
    kern = functools.partial(_sample_cmp_kernel, pages_per_step=pages_per_step, n_steps=n_steps,
                             t_pos=t_pos, n_slc=n_slc)
    return pl.pallas_call(
        kern,
        grid_spec=grid_spec,
        out_shape=[
            jax.ShapeDtypeStruct((bd, n_q, HEAD_DIM), F32),
            jax.ShapeDtypeStruct((bd, N_KV, N_SELECT, LANES), jnp.int32),
        ],
        compiler_params=_params("parallel", "arbitrary"),
        name="sample_cmp",
    )(page_table, *([cache3] * pages_per_step), new_rows, w1cat, pe8, w1f, w2, kn, cos, sin, q3, ov)


def _sample_attn_kernel(pt_ref, sel_ref, *refs, t_pos, n_past_blocks):
    del pt_ref
    n_blk = N_KV * N_SELECT
    k_refs = refs[:n_blk]
    v_refs = refs[n_blk:2 * n_blk]
    new_kv_ref, new_win_ref, state_ref, q_ref, gate_ref, oc_ref, o_ref = refs[2 * n_blk:]
    b = pl.program_id(0)
    q = q_ref[...]
    qf = q.astype(F32)
    n_q = q.shape[0]
    rep = n_q // N_KV
    head = lax.broadcasted_iota(jnp.int32, (n_q, 1), 0)
    n_keys = N_SELECT * SLC_BLOCK
    lane = lax.broadcasted_iota(jnp.int32, (1, n_keys), 1)
    row_b = lax.broadcasted_iota(jnp.int32, (SLC_BLOCK, HEAD_DIM), 0)
    w_keep = state_ref.shape[0]
    o_s = jnp.zeros((n_q, HEAD_DIM), F32)
    o_w = jnp.zeros((n_q, HEAD_DIM), F32)
    for g in range(N_KV):
        in_group = (head // rep) == g
        new_k = new_kv_ref[:, (2 * N_KV + g) * HEAD_DIM:(2 * N_KV + g + 1) * HEAD_DIM]
        new_v = new_kv_ref[:, (3 * N_KV + g) * HEAD_DIM:(3 * N_KV + g + 1) * HEAD_DIM]
        tail_k = jnp.where(row_b == 0, new_k, 0.0)
        tail_v = jnp.where(row_b == 0, new_v, 0.0)
        k_parts, v_parts = [], []
        base = jnp.zeros((1, n_keys), jnp.int32)
        for k in range(N_SELECT):
            blk = sel_ref[b, g * N_SELECT + k]
            is_tail = blk >= n_past_blocks
            k_parts.append(jnp.where(is_tail, tail_k, k_refs[g * N_SELECT + k][...]).astype(BF16))
            v_parts.append(jnp.where(is_tail, tail_v, v_refs[g * N_SELECT + k][...]).astype(BF16))
            base = jnp.where(lane // SLC_BLOCK == k, blk * SLC_BLOCK, base)
        keys = jnp.concatenate(k_parts, axis=0)
        vals = jnp.concatenate(v_parts, axis=0)
        tok = base + lane % SLC_BLOCK
        s = lax.dot_general(q, keys, _NT, preferred_element_type=F32)
        p = _softmax_rows(s, tok <= t_pos)
        o_s = jnp.where(in_group, jnp.dot(p.astype(BF16), vals, preferred_element_type=F32), o_s)

        kw = state_ref[:, g * HEAD_DIM:(g + 1) * HEAD_DIM].astype(BF16)
        vw = state_ref[:, (N_KV + g) * HEAD_DIM:(N_KV + g + 1) * HEAD_DIM].astype(BF16)
        kw_new = new_win_ref[:, g * HEAD_DIM:(g + 1) * HEAD_DIM]
        vw_new = new_win_ref[:, (N_KV + g) * HEAD_DIM:(N_KV + g + 1) * HEAD_DIM]
        dist = w_keep - lax.broadcasted_iota(jnp.int32, (1, w_keep), 1)
        m_w = (dist >= 0) & (dist < WINDOW) & (t_pos - dist >= 0)
        s_w = jnp.where(m_w, lax.dot_general(q, kw, _NT, preferred_element_type=F32), NEG)
        s_new = jnp.sum(qf * kw_new, axis=-1, keepdims=True)
        m = jnp.maximum(jnp.max(s_w, axis=-1, keepdims=True), s_new)
        e_w = jnp.exp(s_w - m)
        e_new = jnp.exp(s_new - m)
        denom = jnp.sum(e_w, axis=-1, keepdims=True) + e_new
        num = jnp.dot(e_w.astype(BF16), vw, preferred_element_type=F32) + e_new * vw_new
        o_w = jnp.where(in_group, num / denom, o_w)

    gates = jnp.broadcast_to(gate_ref[...], (n_q, LANES))
    glane = lax.broadcasted_iota(jnp.int32, (n_q, LANES), 1)
    gsel = [jnp.sum(jnp.where(glane == head * N_BRANCH + br, gates, 0.0), axis=-1, keepdims=True)
            for br in range(N_BRANCH)]
    o_ref[...] = (gsel[0] * oc_ref[...] + gsel[1] * o_s + gsel[2] * o_w).astype(BF16)


def _sample_attn(page_table, sel_idx, cache_blocks, new_kv, new_win, state3, q3, gates3, o_c,
                 t_pos, n_past_blocks, blocks_per_page):
    bd = page_table.shape[0]
    n_q = q3.shape[1]

    def blk_spec(g, k, kind):
        def index(i, pt, sel):
            blk = jnp.minimum(sel[i, g * N_SELECT + k], n_past_blocks - 1)
            page = pt[i, blk // blocks_per_page]
            return (page * blocks_per_page + blk % blocks_per_page, 0, kind * N_KV + g)
        return pl.BlockSpec((None, SLC_BLOCK, HEAD_DIM), index)

    k_specs = [blk_spec(g, k, 2) for g in range(N_KV) for k in range(N_SELECT)]
    v_specs = [blk_spec(g, k, 3) for g in range(N_KV) for k in range(N_SELECT)]
    per_b = lambda shape: pl.BlockSpec((None,) + shape, lambda i, pt, sel: (i, 0, 0))
    grid_spec = pltpu.PrefetchScalarGridSpec(
        num_scalar_prefetch=2,
        grid=(bd,),
        in_specs=k_specs + v_specs + [
            per_b((1, new_kv.shape[2])),
            per_b((1, new_win.shape[2])),
            per_b(state3.shape[1:]),
            per_b((n_q, HEAD_DIM)),
            per_b((1, LANES)),
            per_b((n_q, HEAD_DIM)),
        ],
        out_specs=per_b((n_q, HEAD_DIM)),
    )
    kern = functools.partial(_sample_attn_kernel, t_pos=t_pos, n_past_blocks=n_past_blocks)
    n_blk = N_KV * N_SELECT
    return pl.pallas_call(
        kern,
        grid_spec=grid_spec,
        out_shape=jax.ShapeDtypeStruct((bd, n_q, HEAD_DIM), BF16),
        compiler_params=_params("parallel"),
        name="sample_attn",
    )(page_table, sel_idx, *([cache_blocks] * (2 * n_blk)), new_kv, new_win, state3, q3, gates3, o_c)


def kernel(x_prompt, x_sample, cache_kv, state_kv_win, page_table, a_norm, a_w_in, a_v_norm, a_w_s,
           a_b_s, a_w_out, mlp_norm, mlp_w_up, mlp_w_down, kv_norm, w_kv, cmp_pe, cmp_w1, cmp_w2,
           k_norm, b_norm, b_w_in, b_q_norm, b_w_out):
    bp, t, d = x_prompt.shape
    bd, td, _ = x_sample.shape
    n_pool, page_size = cache_kv.shape[:2]
    n_pages = page_table.shape[1]
    past_len = n_pages * page_size
    w_keep = state_kv_win.shape[1]
    depth = mlp_norm.shape[0]
    n_a = a_norm.shape[0]
    n_b = b_norm.shape[0]
    d_q = b_w_out.shape[1]
    assert td == 1 and n_b == 1 and depth == n_a + n_b
    assert d // A_GROUPS == LANES and t % KV_TILE == 0 and t >= Q_BLOCK + WINDOW
    assert w_keep == WINDOW and page_size % SLC_BLOCK == 0 and past_len % KV_ALIGN == 0

    row = lambda v: v.reshape(1, -1)
    hp = x_prompt.reshape(bp * t, d)
    hs = x_sample.reshape(bd, d)
    tm = 512

    v_rows = []
    for l in range(n_a):
        w_in = a_w_in[l].astype(BF16)
        w_out = a_w_out[l].astype(BF16)
        w_up = mlp_w_up[l].astype(BF16)
        w_down = mlp_w_down[l].astype(BF16)
        zp = _gmlp_in(hp, row(a_norm[l]), w_in, tm, 512)
        hp = _gmlp_out(hp, zp, row(a_v_norm[l]), a_w_s[l], a_b_s[l].T, w_out, 256)
        hp = _mlp(hp, row(mlp_norm[l]), w_up, w_down, tm, 512)
        zs = _gmlp_in(hs, row(a_norm[l]), w_in, bd, 512)
        wd = jnp.repeat(a_w_s[l][:, 0, 0], d // A_GROUPS).reshape(1, d)
        bb = jnp.repeat(a_b_s[l][:, 0], d // A_GROUPS).reshape(1, d)
        hs, v_s = _gmlp_out_single(hs, zs, row(a_v_norm[l]), wd, bb, w_out)
        v_rows.append(v_s.reshape(bd, td, d))
        hs = _mlp(hs, row(mlp_norm[l]), w_up, w_down, bd, 512)

    w_kv_b = w_kv.astype(BF16)
    cos_p, sin_p = _rope_tables(jnp.arange(t))
    cos_pp, sin_pp = jnp.tile(cos_p, (bp, 1)), jnp.tile(sin_p, (bp, 1))
    cos_s, sin_s = _rope_tables(jnp.full((bd,), past_len))
    paged_p, win_p, kvb_p = _kv_proj(hp, row(kv_norm), w_kv_b, k_norm, cos_pp, sin_pp, tm)
    paged_s, win_s, _ = _kv_proj(hs, row(kv_norm), w_kv_b, k_norm, cos_s, sin_s, bd)

    j = 0
    wq = b_w_in[j][:, :d_q].astype(BF16)
    wg = jnp.pad(b_w_in[j][:, d_q:], ((0, 0), (0, LANES - (b_w_in.shape[2] - d_q)))).astype(BF16)
    q_p, gates_p = _q_proj(hp, row(b_norm[j]), wq, wg, row(b_q_norm[j]), cos_pp, sin_pp, tm)
    q_s, gates_s = _q_proj(hs, row(b_norm[j]), wq, wg, row(b_q_norm[j]), cos_s, sin_s, bd)

    w1cat = jnp.concatenate([cmp_w1[:, :CMP_STRIDE], cmp_w1[:, CMP_STRIDE:]], axis=-1).astype(BF16)
    w1f = cmp_w1.reshape(2, CMP_BLOCK * HEAD_DIM, HEAD_DIM).astype(BF16)
    pe8 = jnp.broadcast_to(cmp_pe.reshape(2, 1, CMP_BLOCK * HEAD_DIM),
                           (2, SUBLANES, CMP_BLOCK * HEAD_DIM)).astype(BF16)
    w2 = cmp_w2.astype(BF16)

    n_sub_p = t // CMP_STRIDE
    n_slc_p = t // SLC_BLOCK
    cos_c, sin_c = _rope_tables(jnp.arange(n_sub_p) * CMP_STRIDE + CMP_BLOCK - 1)
    kc_p, vc_p = _compress_prompt(paged_p, w1cat, pe8, w1f, w2, k_norm, cos_c, sin_c, bp, t)
    ovt = _overlap_matrix(n_sub_p - 1, n_slc_p, n_sub_p, n_slc_p).T
    expand = (jnp.arange(LANES)[:, None] == (jnp.arange(t)[None, :] // SLC_BLOCK)).astype(BF16)
    o_p = _attn_prompt(q_p, gates_p, kc_p, vc_p, kvb_p, ovt, expand, bp, t)
    w_o = b_w_out[j].astype(BF16)
    hp = _out_proj(hp, o_p, w_o, tm)

    tp_s = -(-(past_len + td) // KV_ALIGN) * KV_ALIGN
    n_sub_s = tp_s // CMP_STRIDE
    n_slc_s = tp_s // SLC_BLOCK
    n_pad_s = -(-n_sub_s // SUBLANES) * SUBLANES
    sel_lanes = -(-n_slc_s // LANES) * LANES
    assert past_len // SLC_BLOCK + 1 >= N_SELECT
    cos_cs, sin_cs = _rope_tables(jnp.arange(n_pad_s) * CMP_STRIDE + CMP_BLOCK - 1)
    ov_s = _overlap_matrix(n_sub_s - 1, n_slc_s, n_pad_s, sel_lanes)
    cache3 = cache_kv.reshape(n_pool, page_size, 4 * N_KV * HEAD_DIM)
    q3 = q_s.reshape(bd, d_q // HEAD_DIM, HEAD_DIM)
    new_kv = paged_s.reshape(bd, 1, -1)
    new_win = win_s.reshape(bd, 1, -1)
    o_c, sel_idx = _sample_cmp(page_table, cache3, new_kv, w1cat, pe8, w1f, w2, k_norm, cos_cs, sin_cs,
                               q3, ov_s, min(16, n_pages), past_len, n_slc_s)
    blocks_per_page = page_size // SLC_BLOCK
    cache_blocks = cache_kv.reshape(n_pool * blocks_per_page, SLC_BLOCK, 4 * N_KV * HEAD_DIM)
    state3 = state_kv_win.reshape(bd, w_keep, 2 * N_KV * HEAD_DIM)
    o_s = _sample_attn(page_table, sel_idx[..., 0].reshape(bd, N_KV * N_SELECT), cache_blocks, new_kv,
                       new_win, state3, q3, gates_s.reshape(bd, 1, LANES), o_c, past_len,
                       past_len // SLC_BLOCK, blocks_per_page)
    hs = _out_proj(hs, o_s.reshape(bd, d_q), w_o, bd)

    l = n_a
    w_up = mlp_w_up[l].astype(BF16)
    w_down = mlp_w_down[l].astype(BF16)
    hp = _mlp(hp, row(mlp_norm[l]), w_up, w_down, tm, 512)
    hs = _mlp(hs, row(mlp_norm[l]), w_up, w_down, bd, 512)

    y_p = hp.reshape(bp, t, d)
    y_s = hs.reshape(bd, td, d)
    kv_p = paged_p.reshape(bp, t, 4, N_KV, HEAD_DIM)
    win_all_p = win_p.reshape(bp, t, 2, N_KV, HEAD_DIM)
    win_new_p = win_all_p[:, t - min(WINDOW, t):]
    kv_s = paged_s.reshape(bd, td, 4, N_KV, HEAD_DIM)
    win_new_s = jnp.concatenate(
        [state_kv_win[:, td:], win_s.reshape(bd, td, 2, N_KV, HEAD_DIM)], axis=1)
    v_a_s = jnp.stack(v_rows, axis=0)
    return (y_p, y_s, kv_p, win_new_p, kv_s, win_new_s, v_a_s)
```

```python
import functools

import jax
import jax.numpy as jnp
from jax import lax
from jax.experimental import pallas as pl
from jax.experimental.pallas import tpu as pltpu

F32 = jnp.float32
BF16 = jnp.bfloat16

CHUNK = 128
A_GROUPS = 16
HEAD_DIM = 128
N_KV = 2
CMP_STRIDE = 16
CMP_BLOCK = 2 * CMP_STRIDE
SLC_BLOCK = 64
N_SELECT = 16
WINDOW = 512
N_BRANCH = 3
KV_ALIGN = 64
ROT_DIM = HEAD_DIM // 4
ROPE_THETA = 500000.0
Q_BLOCK = 128
EPS = 1e-6
NEG = -1e30
FORCE_BONUS = 1e4

LANES = 128
SUBLANES = 8
VMEM_LIMIT_BYTES = 56 * 1024 * 1024

KV_TILE = 512
N_KINDS = 6

_NT = (((1,), (1,)), ((), ()))


def _params(*sem):
    return pltpu.CompilerParams(dimension_semantics=sem, vmem_limit_bytes=VMEM_LIMIT_BYTES)


def _rms(x, g):
    ms = jnp.mean(x * x, axis=-1, keepdims=True)
    return x * lax.rsqrt(ms + EPS) * g


def _rope(x, cos, sin):
    half = ROT_DIM // 2
    lane = lax.broadcasted_iota(jnp.int32, x.shape, 1)
    partner = jnp.where(lane < half, pltpu.roll(x, LANES - half, 1), pltpu.roll(x, half, 1))
    return x * cos + partner * sin


def _rope_tables(pos):
    inv = ROPE_THETA ** (-jnp.arange(0, ROT_DIM, 2, dtype=F32) / ROT_DIM)
    ang = pos.astype(F32)[:, None] * inv[None, :]
    c, s = jnp.cos(ang), jnp.sin(ang)
    n = pos.shape[0]
    pad = LANES - ROT_DIM
    cos = jnp.concatenate([c, c, jnp.ones((n, pad), F32)], axis=1)
    sin = jnp.concatenate([-s, s, jnp.zeros((n, pad), F32)], axis=1)
    return cos, sin


def _gmlp_in_kernel(x_ref, g_ref, w_ref, z_ref, xn_ref):
    @pl.when(pl.program_id(1) == 0)
    def _():
        xn_ref[...] = _rms(x_ref[...], g_ref[...]).astype(BF16)

    z_ref[...] = jax.nn.gelu(jnp.dot(xn_ref[...], w_ref[...], preferred_element_type=F32))


def _gmlp_in(x, g, w, tm, tn):
    m, d = x.shape
    n = w.shape[1]
    return pl.pallas_call(
        _gmlp_in_kernel,
        grid=(m // tm, n // tn),
        in_specs=[
            pl.BlockSpec((tm, d), lambda i, j: (i, 0)),
            pl.BlockSpec((1, d), lambda i, j: (0, 0)),
            pl.BlockSpec((d, tn), lambda i, j: (0, j)),
        ],
        out_specs=pl.BlockSpec((tm, tn), lambda i, j: (i, j)),
        out_shape=jax.ShapeDtypeStruct((m, n), F32),
        scratch_shapes=[pltpu.VMEM((tm, d), BF16)],
        compiler_params=_params("parallel", "arbitrary"),
        name="gmlp_in",
    )(x, g, w)


def _gmlp_out_kernel(x_ref, u_ref, v_ref, vg_ref, ws_ref, bt_ref, wo_ref, o_ref, vn_ref, y_ref):
    tm = x_ref.shape[0]
    vn_ref[...] = _rms(v_ref[...], vg_ref[...]).astype(BF16)
    row = lax.broadcasted_iota(jnp.int32, (CHUNK, CHUNK), 0)
    col = lax.broadcasted_iota(jnp.int32, (CHUNK, CHUNK), 1)
    causal = row >= col
    for g in range(A_GROUPS):
        wsg = jnp.where(causal, ws_ref[g], 0.0).astype(BF16)
        bias = bt_ref[:, g:g + 1]
        cs = slice(g * LANES, (g + 1) * LANES)
        for c in range(tm // CHUNK):
            rs = slice(c * CHUNK, (c + 1) * CHUNK)
            mixed = jnp.dot(wsg, vn_ref[rs, cs], preferred_element_type=F32) + bias
            y_ref[rs, cs] = (u_ref[rs, cs] * mixed).astype(BF16)
    o_ref[...] = x_ref[...] + jnp.dot(y_ref[...], wo_ref[...], preferred_element_type=F32)


def _gmlp_out(x, z, vg, ws, bt, wo, tm):
    m, d = x.shape
    return pl.pallas_call(
        _gmlp_out_kernel,
        grid=(m // tm,),
        in_specs=[
            pl.BlockSpec((tm, d), lambda i: (i, 0)),
            pl.BlockSpec((tm, d), lambda i: (i, 0)),
            pl.BlockSpec((tm, d), lambda i: (i, 1)),
            pl.BlockSpec((1, d), lambda i: (0, 0)),
            pl.BlockSpec((A_GROUPS, CHUNK, CHUNK), lambda i: (0, 0, 0)),
            pl.BlockSpec((CHUNK, A_GROUPS), lambda i: (0, 0)),
            pl.BlockSpec((d, d), lambda i: (0, 0)),
        ],
        out_specs=pl.BlockSpec((tm, d), lambda i: (i, 0)),
        out_shape=jax.ShapeDtypeStruct((m, d), F32),
        scratch_shapes=[pltpu.VMEM((tm, d), BF16), pltpu.VMEM((tm, d), BF16)],
        compiler_params=_params("parallel"),
        name="gmlp_out",
    )(x, z, z, vg, ws, bt, wo)


def _gmlp_out_single_kernel(x_ref, u_ref, v_ref, vg_ref, wd_ref, bb_ref, wo_ref, o_ref, vn_ref):
    vn = _rms(v_ref[...], vg_ref[...])
    vn_ref[...] = vn
    mixed = vn * wd_ref[...] + bb_ref[...]
    y = (u_ref[...] * mixed).astype(BF16)
    o_ref[...] = x_ref[...] + jnp.dot(y, wo_ref[...], preferred_element_type=F32)


def _gmlp_out_single(x, z, vg, wd, bb, wo):
    m, d = x.shape
    full = lambda i: (0, 0)
    return pl.pallas_call(
        _gmlp_out_single_kernel,
        grid=(1,),
        in_specs=[
            pl.BlockSpec((m, d), full),
            pl.BlockSpec((m, d), lambda i: (0, 0)),
            pl.BlockSpec((m, d), lambda i: (0, 1)),
            pl.BlockSpec((1, d), full),
            pl.BlockSpec((1, d), full),
            pl.BlockSpec((1, d), full),
            pl.BlockSpec((d, d), full),
        ],
        out_specs=[pl.BlockSpec((m, d), full), pl.BlockSpec((m, d), full)],
        out_shape=[jax.ShapeDtypeStruct((m, d), F32), jax.ShapeDtypeStruct((m, d), F32)],
        compiler_params=_params("arbitrary"),
        name="gmlp_out_single",
    )(x, z, z, vg, wd, bb, wo)


def _mlp_kernel(x_ref, g_ref, wu_ref, wd_ref, o_ref, xn_ref):
    @pl.when(pl.program_id(1) == 0)
    def _():
        x = x_ref[...]
        xn_ref[...] = _rms(x, g_ref[...]).astype(BF16)
        o_ref[...] = x

    h = jnp.dot(xn_ref[...], wu_ref[...], preferred_element_type=F32)
    a = jnp.square(jnp.maximum(h, 0.0)).astype(BF16)
    o_ref[...] += jnp.dot(a, wd_ref[...], preferred_element_type=F32)


def _mlp(x, g, wu, wd, tm, tf):
    m, d = x.shape
    f = wu.shape[1]
    return pl.pallas_call(
        _mlp_kernel,
        grid=(m // tm, f // tf),
        in_specs=[
            pl.BlockSpec((tm, d), lambda i, j: (i, 0)),
            pl.BlockSpec((1, d), lambda i, j: (0, 0)),
            pl.BlockSpec((d, tf), lambda i, j: (0, j)),
            pl.BlockSpec((tf, d), lambda i, j: (j, 0)),
        ],
        out_specs=pl.BlockSpec((tm, d), lambda i, j: (i, 0)),
        out_shape=jax.ShapeDtypeStruct((m, d), F32),
        scratch_shapes=[pltpu.VMEM((tm, d), BF16)],
        compiler_params=_params("parallel", "arbitrary"),
        name="mlp",
    )(x, g, wu, wd)


def _kv_proj_kernel(x_ref, g_ref, w_ref, kn_ref, cos_ref, sin_ref, paged_ref, win_ref, cmp_ref, kvb_ref):
    tm = x_ref.shape[0]
    n_paged = 4 * N_KV
    n_win = 2 * N_KV
    xn = _rms(x_ref[...], g_ref[...]).astype(BF16)
    kv = jnp.dot(xn, w_ref[...], preferred_element_type=F32)
    cos, sin = cos_ref[...], sin_ref[...]
    for c in range(N_KINDS * N_KV):
        kind = c // N_KV
        h = kv[:, c * LANES:(c + 1) * LANES]
        if kind == 2:
            h = _rope(_rms(h, kn_ref[1:2, :]), cos, sin)
        elif kind == 4:
            h = _rope(_rms(h, kn_ref[2:3, :]), cos, sin)
        if kind < 4:
            paged_ref[pl.ds(c, tm, stride=n_paged), :] = h
        else:
            win_ref[pl.ds(c - n_paged, tm, stride=n_win), :] = h
        if kind < 2:
            cmp_ref[c] = h
        else:
            kvb_ref[c - 2 * N_KV] = h.astype(BF16)


def _kv_proj(x, g, w, kn, cos, sin, tm):
    m, d = x.shape
    n = w.shape[1]
    n_paged = 4 * N_KV
    n_win = 2 * N_KV
    return pl.pallas_call(
        _kv_proj_kernel,
        grid=(m // tm,),
        in_specs=[
            pl.BlockSpec((tm, d), lambda i: (i, 0)),
            pl.BlockSpec((1, d), lambda i: (0, 0)),
            pl.BlockSpec((d, n), lambda i: (0, 0)),
            pl.BlockSpec((3, HEAD_DIM), lambda i: (0, 0)),
            pl.BlockSpec((tm, LANES), lambda i: (i, 0)),
            pl.BlockSpec((tm, LANES), lambda i: (i, 0)),
        ],
        out_specs=[
            pl.BlockSpec((tm * n_paged, HEAD_DIM), lambda i: (i, 0)),
            pl.BlockSpec((tm * n_win, HEAD_DIM), lambda i: (i, 0)),
            pl.BlockSpec((2 * N_KV, tm, HEAD_DIM), lambda i: (0, i, 0)),
            pl.BlockSpec((4 * N_KV, tm, HEAD_DIM), lambda i: (0, i, 0)),
        ],
        out_shape=[
            jax.ShapeDtypeStruct((m * n_paged, HEAD_DIM), F32),
            jax.ShapeDtypeStruct((m * n_win, HEAD_DIM), F32),
            jax.ShapeDtypeStruct((2 * N_KV, m, HEAD_DIM), F32),
            jax.ShapeDtypeStruct((4 * N_KV, m, HEAD_DIM), BF16),
        ],
        compiler_params=_params("parallel"),
        name="kv_proj",
    )(x, g, w, kn, cos, sin)


def _q_proj_kernel(x_ref, g_ref, wq_ref, wg_ref, qn_ref, cos_ref, sin_ref, q_ref, gate_ref):
    xn = _rms(x_ref[...], g_ref[...]).astype(BF16)
    z = jnp.dot(xn, wq_ref[...], preferred_element_type=F32)
    cos, sin = cos_ref[...], sin_ref[...]
    qn = qn_ref[...]
    scale = HEAD_DIM ** -0.5
    for h in range(z.shape[1] // HEAD_DIM):
        cs = slice(h * HEAD_DIM, (h + 1) * HEAD_DIM)
        qh = _rope(_rms(z[:, cs], qn), cos, sin) * scale
        q_ref[:, cs] = qh.astype(BF16)
    gate_ref[...] = jax.nn.sigmoid(jnp.dot(xn, wg_ref[...], preferred_element_type=F32))


def _q_proj(x, g, wq, wg, qn, cos, sin, tm):
    m, d = x.shape
    n = wq.shape[1]
    return pl.pallas_call(
        _q_proj_kernel,
        grid=(m // tm,),
        in_specs=[
            pl.BlockSpec((tm, d), lambda i: (i, 0)),
            pl.BlockSpec((1, d), lambda i: (0, 0)),
            pl.BlockSpec((d, n), lambda i: (0, 0)),
            pl.BlockSpec((d, LANES), lambda i: (0, 0)),
            pl.BlockSpec((1, HEAD_DIM), lambda i: (0, 0)),
            pl.BlockSpec((tm, LANES), lambda i: (i, 0)),
            pl.BlockSpec((tm, LANES), lambda i: (i, 0)),
        ],
        out_specs=[
            pl.BlockSpec((tm, n), lambda i: (i, 0)),
            pl.BlockSpec((tm, LANES), lambda i: (i, 0)),
        ],
        out_shape=[
            jax.ShapeDtypeStruct((m, n), BF16),
            jax.ShapeDtypeStruct((m, LANES), F32),
        ],
        compiler_params=_params("parallel"),
        name="q_proj",
    )(x, g, wq, wg, qn, cos, sin)


def _out_proj_kernel(h_ref, o_ref, w_ref, y_ref):
    y_ref[...] = h_ref[...] + jnp.dot(o_ref[...], w_ref[...], preferred_element_type=F32)


def _out_proj(h, o, w, tm):
    m, d = h.shape
    k = o.shape[1]
    return pl.pallas_call(
        _out_proj_kernel,
        grid=(m // tm,),
        in_specs=[
            pl.BlockSpec((tm, d), lambda i: (i, 0)),
            pl.BlockSpec((tm, k), lambda i: (i, 0)),
            pl.BlockSpec((k, d), lambda i: (0, 0)),
        ],
        out_specs=pl.BlockSpec((tm, d), lambda i: (i, 0)),
        out_shape=jax.ShapeDtypeStruct((m, d), F32),
        compiler_params=_params("parallel"),
        name="out_proj",
    )(h, o, w)


def _cmp_bias(pe_ref, w1f_ref, kind):
    return jnp.dot(pe_ref[kind], w1f_ref[kind], preferred_element_type=F32)[0:1, :]


def _cmp_finish(fs, bias, w2, n_rows):
    first = fs[:, :HEAD_DIM]
    second = pltpu.roll(fs[:, HEAD_DIM:], n_rows - 1, 0)
    h = first + second + bias
    h = h * jax.nn.sigmoid(h)
    return jnp.dot(h.astype(BF16), w2, preferred_element_type=F32)


def _compress_prompt_kernel(krows_ref, vrows_ref, w1_ref, pe_ref, w1f_ref, w2_ref, kn_ref,
                            cos_ref, sin_ref, kc_ref, vc_ref):
    n_sub = krows_ref.shape[0] // CMP_STRIDE
    rowi = lax.broadcasted_iota(jnp.int32, (n_sub, HEAD_DIM), 0)
    for kind, rows_ref, out_ref in ((0, krows_ref, kc_ref), (1, vrows_ref, vc_ref)):
        fs = jnp.zeros((n_sub, 2 * HEAD_DIM), F32)
        for r in range(CMP_STRIDE):
            xr = rows_ref[pl.ds(r, n_sub, stride=CMP_STRIDE), :].astype(BF16)
            fs = fs + jnp.dot(xr, w1_ref[kind, r], preferred_element_type=F32)
        out = _cmp_finish(fs, _cmp_bias(pe_ref, w1f_ref, kind), w2_ref[kind], n_sub)
        if kind == 0:
            out = _rope(_rms(out, kn_ref[0:1, :]), cos_ref[...], sin_ref[...])
        out_ref[...] = jnp.where(rowi < n_sub - 1, out, 0.0).astype(BF16)


def _compress_prompt(paged, w1cat, pe8, w1f, w2, kn, cos, sin, b, t):
    n_sub = t // CMP_STRIDE
    full = lambda nd: (lambda i, g: (0,) * nd)
    out_spec = pl.BlockSpec((None, None, n_sub, HEAD_DIM), lambda i, g: (i, g, 0, 0))
    out_shape = jax.ShapeDtypeStruct((b, N_KV, n_sub, HEAD_DIM), BF16)
    return pl.pallas_call(
        _compress_prompt_kernel,
        grid=(b, N_KV),
        in_specs=[
            pl.BlockSpec((None, t, HEAD_DIM), lambda i, g: (g, i, 0)),
            pl.BlockSpec((None, t, HEAD_DIM), lambda i, g: (N_KV + g, i, 0)),
            pl.BlockSpec(w1cat.shape, full(4)),
            pl.BlockSpec(pe8.shape, full(3)),
            pl.BlockSpec(w1f.shape, full(3)),
            pl.BlockSpec(w2.shape, full(3)),
            pl.BlockSpec(kn.shape, full(2)),
            pl.BlockSpec(cos.shape, full(2)),
            pl.BlockSpec(sin.shape, full(2)),
        ],
        out_specs=[out_spec, out_spec],
        out_shape=[out_shape, out_shape],
        compiler_params=_params("parallel", "parallel"),
        name="compress_prompt",
    )(paged, paged, w1cat, pe8, w1f, w2, kn, cos, sin)


def _split_bf16(x):
    hi = x.astype(BF16)
    lo = (x - hi.astype(F32)).astype(BF16)
    return hi, lo


def _overlap_matrix(n_cmp, n_slc, rows, cols):
    ci = jnp.arange(rows)[:, None] * CMP_STRIDE
    sj = jnp.arange(cols)[None, :] * SLC_BLOCK
    ov = (ci < sj + SLC_BLOCK) & (ci + CMP_BLOCK > sj)
    ov = ov & (jnp.arange(rows)[:, None] < n_cmp) & (jnp.arange(cols)[None, :] < n_slc)
    return ov.astype(BF16)


def _softmax_rows(s, valid):
    s = jnp.where(valid, s, NEG)
    e = jnp.exp(s - jnp.max(s, axis=-1, keepdims=True))
    return e / jnp.sum(e, axis=-1, keepdims=True)


def _attn_prompt_kernel(q_ref, gate_ref, kc_ref, vc_ref, ks_ref, vs_ref, kw_ref, vw_ref,
                        ovt_ref, exp_ref, o_ref, m_ref, l_ref, acc_ref):
    qb = pl.program_id(2)
    g = pl.program_id(1)
    nq = Q_BLOCK
    rep = q_ref.shape[1] // HEAD_DIM
    n_cmp_pad = kc_ref.shape[0]
    n_slc = ovt_ref.shape[0]
    t_len = ks_ref.shape[0]
    s0 = qb * nq

    q_all = q_ref[...]
    q2 = jnp.concatenate([q_all[:, r * HEAD_DIM:(r + 1) * HEAD_DIM] for r in range(rep)], axis=0)
    t_col = s0 + lax.broadcasted_iota(jnp.int32, (nq, 1), 0)

    kc = kc_ref[...]
    vc = vc_ref[...]
    cpos = lax.broadcasted_iota(jnp.int32, (1, n_cmp_pad), 1) * CMP_STRIDE + (CMP_BLOCK - 1)
    m_c = cpos <= t_col
    s_c = lax.dot_general(q2, kc, _NT, preferred_element_type=F32)
    psum = jnp.zeros((nq, n_cmp_pad), F32)
    p_parts = []
    for r in range(rep):
        p = jnp.where(m_c, _softmax_rows(s_c[r * nq:(r + 1) * nq], m_c), 0.0)
        psum = psum + p
        p_parts.append(p.astype(BF16))
    o_c = jnp.dot(jnp.concatenate(p_parts, axis=0), vc, preferred_element_type=F32)

    ovt = ovt_ref[...]
    p_hi, p_lo = _split_bf16(psum)
    imp_t = (lax.dot_general(ovt, p_hi, _NT, preferred_element_type=F32)
             + lax.dot_general(ovt, p_lo, _NT, preferred_element_type=F32))
    t_row = s0 + lax.broadcasted_iota(jnp.int32, (n_slc, nq), 1)
    blk = lax.broadcasted_iota(jnp.int32, (n_slc, nq), 0)
    cur = t_row // SLC_BLOCK
    causal = blk * SLC_BLOCK <= t_row
    forced = (blk == 0) | (blk == cur) | (blk == cur - 1)
    score = jnp.where(causal, imp_t + jnp.where(forced, FORCE_BONUS, 0.0), NEG)
    rank = jnp.zeros((n_slc, nq), jnp.int32)
    for i in range(n_slc):
        si = score[i:i + 1, :]
        later = (blk > i).astype(jnp.int32)
        rank = rank + jnp.where(si > score, 1, 0) + jnp.where(si == score, later, 0)
    sel_t = jnp.where(causal & (rank < N_SELECT), 1.0, 0.0)
    sel_t = jnp.concatenate([sel_t, jnp.zeros((LANES - n_slc, nq), F32)], axis=0)
    sel = sel_t.T.astype(BF16)

    m_ref[...] = jnp.full(m_ref.shape, NEG, F32)
    l_ref[...] = jnp.zeros(l_ref.shape, F32)
    acc_ref[...] = jnp.zeros(acc_ref.shape, F32)
    n_tiles = (s0 + nq + KV_TILE - 1) // KV_TILE

    def tile_body(j, carry):
        off = pl.multiple_of(j * KV_TILE, KV_TILE)
        k = ks_ref[pl.ds(off, KV_TILE), :]
        v = vs_ref[pl.ds(off, KV_TILE), :]
        picked = jnp.dot(sel, exp_ref[:, pl.ds(off, KV_TILE)], preferred_element_type=F32)
        kpos = off + lax.broadcasted_iota(jnp.int32, (nq, KV_TILE), 1)
        bias = jnp.where((picked > 0.5) & (kpos <= t_col), 0.0, NEG)
        s = lax.dot_general(q2, k, _NT, preferred_element_type=F32)
        p_parts = []
        for r in range(rep):
            rs = slice(r * nq, (r + 1) * nq)
            sr = s[rs] + bias
            m_old = m_ref[rs]
            m_new = jnp.maximum(m_old, jnp.max(sr, axis=-1, keepdims=True))
            pr = jnp.exp(sr - m_new)
            alpha = jnp.exp(m_old - m_new)
            l_ref[rs] = alpha * l_ref[rs] + jnp.sum(pr, axis=-1, keepdims=True)
            acc_ref[rs] = alpha * acc_ref[rs]
            m_ref[rs] = m_new
            p_parts.append(pr.astype(BF16))
        acc_ref[...] += jnp.dot(jnp.concatenate(p_parts, axis=0), v, preferred_element_type=F32)
        return carry

    lax.fori_loop(0, n_tiles, tile_body, 0)
    o_s = acc_ref[...] / l_ref[...]

    slab = nq + WINDOW
    w0 = pl.multiple_of(jnp.maximum(s0 - WINDOW, 0), nq)
    kw = kw_ref[pl.ds(w0, slab), :]
    vw = vw_ref[pl.ds(w0, slab), :]
    dist = t_col - (w0 + lax.broadcasted_iota(jnp.int32, (nq, slab), 1))
    m_w = (dist >= 0) & (dist < WINDOW)
    s_w = lax.dot_general(q2, kw, _NT, preferred_element_type=F32)
    p_parts = [_softmax_rows(s_w[r * nq:(r + 1) * nq], m_w).astype(BF16) for r in range(rep)]
    o_w = jnp.dot(jnp.concatenate(p_parts, axis=0), vw, preferred_element_type=F32)

    gates = gate_ref[...]
    for r in range(rep):
        rs = slice(r * nq, (r + 1) * nq)
        c0 = (g * rep + r) * N_BRANCH
        lane = lax.broadcasted_iota(jnp.int32, gates.shape, 1)
        gsel = [jnp.sum(jnp.where(lane == c0 + br, gates, 0.0), axis=-1, keepdims=True)
                for br in range(N_BRANCH)]
        out = gsel[0] * o_c[rs] + gsel[1] * o_s[rs] + gsel[2] * o_w[rs]
        o_ref[:, r * HEAD_DIM:(r + 1) * HEAD_DIM] = out.astype(BF16)


def _attn_prompt(q, gates, kc, vc, kvb, ovt, expand, b, t):
    nqb = t // Q_BLOCK
    rep = q.shape[1] // HEAD_DIM // N_KV
    n_cmp_pad = kc.shape[2]
    rows = lambda kind: pl.BlockSpec((None, t, HEAD_DIM), lambda i, g, j: ((kind - 2) * N_KV + g, i, 0))
    cmp_spec = pl.BlockSpec((None, None, n_cmp_pad, HEAD_DIM), lambda i, g, j: (i, g, 0, 0))
    return pl.pallas_call(
        _attn_prompt_kernel,
        grid=(b, N_KV, nqb),
        in_specs=[
            pl.BlockSpec((Q_BLOCK, rep * HEAD_DIM), lambda i, g, j: (i * nqb + j, g)),
            pl.BlockSpec((Q_BLOCK, LANES), lambda i, g, j: (i * nqb + j, 0)),
            cmp_spec, cmp_spec,
            rows(2), rows(3), rows(4), rows(5),
            pl.BlockSpec(ovt.shape, lambda i, g, j: (0, 0)),
            pl.BlockSpec(expand.shape, lambda i, g, j: (0, 0)),
        ],
        out_specs=pl.BlockSpec((Q_BLOCK, rep * HEAD_DIM), lambda i, g, j: (i * nqb + j, g)),
        out_shape=jax.ShapeDtypeStruct(q.shape, BF16),
        scratch_shapes=[
            pltpu.VMEM((rep * Q_BLOCK, 1), F32),
            pltpu.VMEM((rep * Q_BLOCK, 1), F32),
            pltpu.VMEM((rep * Q_BLOCK, HEAD_DIM), F32),
        ],
        compiler_params=_params("parallel", "parallel", "arbitrary"),
        name="attn_prompt",
    )(q, gates, kc, vc, kvb, kvb, kvb, kvb, ovt, expand)


def _sample_cmp_kernel(pt_ref, *refs, pages_per_step, n_steps, t_pos, n_slc):
    del pt_ref
    page_refs = refs[:pages_per_step]
    (new_ref, w1_ref, pe_ref, w1f_ref, w2_ref, kn_ref, cos_ref, sin_ref, q_ref, ov_ref,
     oc_ref, idx_ref, fs_ref, stage_ref) = refs[pages_per_step:]
    step = pl.program_id(1)
    n_heads = N_KV * 2
    heads_per_row = 4 * N_KV
    page_rows = page_refs[0].shape[0] // heads_per_row
    sub_per_page = page_rows // CMP_STRIDE
    rows_per_step = pages_per_step * sub_per_page
    n_pad = fs_ref.shape[1]
    n_past = n_steps * rows_per_step
    row0 = pl.multiple_of(step * rows_per_step, rows_per_step)

    for c in range(n_heads):
        fs_ref[c, pl.ds(row0, rows_per_step), :] = jnp.zeros((rows_per_step, 2 * HEAD_DIM), F32)
        for k, p in enumerate(page_refs):
            stage_ref[c, k * page_rows:(k + 1) * page_rows, :] = p[pl.ds(c, page_rows, stride=heads_per_row), :]

    def r_body(r, carry):
        for c in range(n_heads):
            xr = stage_ref[c, pl.ds(r, rows_per_step, stride=CMP_STRIDE), :]
            fs_ref[c, pl.ds(row0, rows_per_step), :] += jnp.dot(
                xr.astype(BF16), w1_ref[c // N_KV, r], preferred_element_type=F32)
        return carry

    lax.fori_loop(0, CMP_STRIDE, r_body, 0)

    @pl.when(step == n_steps - 1)
    def _():
        tail = n_pad - n_past
        row_t = lax.broadcasted_iota(jnp.int32, (tail, HEAD_DIM), 0)
        rowi = lax.broadcasted_iota(jnp.int32, (n_pad, HEAD_DIM), 0)
        n_sub = (t_pos + 1 + KV_ALIGN - 1) // KV_ALIGN * KV_ALIGN // CMP_STRIDE
        outs = []
        for c in range(n_heads):
            kind = c // N_KV
            new_row = new_ref[c:c + 1, :]
            x_tail = jnp.where(row_t == 0, new_row, 0.0).astype(BF16)
            fs_ref[c, n_past:n_pad, :] = jnp.dot(x_tail, w1_ref[kind, 0], preferred_element_type=F32)
            out = _cmp_finish(fs_ref[c], _cmp_bias(pe_ref, w1f_ref, kind), w2_ref[kind], n_pad)
            if kind == 0:
                out = _rope(_rms(out, kn_ref[0:1, :]), cos_ref[...], sin_ref[...])
            outs.append(jnp.where(rowi < n_sub - 1, out, 0.0).astype(BF16))

        q = q_ref[...]
        n_q = q.shape[0]
        rep = n_q // N_KV
        head = lax.broadcasted_iota(jnp.int32, (n_q, 1), 0)
        cpos = lax.broadcasted_iota(jnp.int32, (1, n_pad), 1) * CMP_STRIDE + (CMP_BLOCK - 1)
        m_c = cpos <= t_pos
        lanes = ov_ref.shape[1]
        blk = lax.broadcasted_iota(jnp.int32, (1, lanes), 1)
        cur = t_pos // SLC_BLOCK
        causal = (blk * SLC_BLOCK <= t_pos) & (blk < n_slc)
        forced = (blk == 0) | (blk == cur) | (blk == cur - 1)
        eye_i = lax.broadcasted_iota(jnp.int32, (lanes, lanes), 0)
        eye_j = lax.broadcasted_iota(jnp.int32, (lanes, lanes), 1)
        o_c = jnp.zeros((n_q, HEAD_DIM), F32)
        for g in range(N_KV):
            kc, vc = outs[g], outs[N_KV + g]
            in_group = (head // rep) == g
            s = lax.dot_general(q, kc, _NT, preferred_element_type=F32)
            p = jnp.where(m_c, _softmax_rows(s, m_c), 0.0)
            o_g = jnp.dot(p.astype(BF16), vc, preferred_element_type=F32)
            o_c = jnp.where(in_group, o_g, o_c)
            psum = jnp.sum(jnp.where(in_group, p, 0.0), axis=0, keepdims=True)
            psum8 = jnp.broadcast_to(psum, (SUBLANES, n_pad))
            p_hi, p_lo = _split_bf16(psum8)
            imp = (jnp.dot(p_hi, ov_ref[...], preferred_element_type=F32)
                   + jnp.dot(p_lo, ov_ref[...], preferred_element_type=F32))[0:1, :]
            score = jnp.where(causal, imp + jnp.where(forced, FORCE_BONUS, 0.0), NEG)
            score_b = jnp.broadcast_to(score, (lanes, lanes))
            score_col = jnp.sum(jnp.where(eye_i == eye_j, score_b, 0.0), axis=1, keepdims=True)
            beats = (score_col > score_b) | ((score_col == score_b) & (eye_i < eye_j))
            rank = jnp.sum(jnp.where(beats, 1.0, 0.0), axis=0, keepdims=True)
            sel = jnp.where(causal & (rank < N_SELECT), 1.0, 0.0)
            sel_b = jnp.broadcast_to(sel, (lanes, lanes))
            sel_col = jnp.sum(jnp.where(eye_i == eye_j, sel_b, 0.0), axis=1, keepdims=True)
            slot = jnp.sum(jnp.where(eye_i < eye_j, sel_col, 0.0), axis=0, keepdims=True)
            slot_b = jnp.broadcast_to(slot, (N_SELECT, lanes))
            k_i = lax.broadcasted_iota(jnp.int32, (N_SELECT, lanes), 0).astype(F32)
            j_i = lax.broadcasted_iota(jnp.int32, (N_SELECT, lanes), 1).astype(F32)
            hit = (slot_b == k_i) & (jnp.broadcast_to(sel, (N_SELECT, lanes)) > 0.5)
            idx = jnp.sum(jnp.where(hit, j_i, 0.0), axis=1, keepdims=True)
            idx_ref[g] = jnp.broadcast_to(idx, (N_SELECT, LANES)).astype(jnp.int32)
        oc_ref[...] = o_c


def _sample_cmp(page_table, cache_pages, new_rows, w1cat, pe8, w1f, w2, kn, cos, sin, q3, ov,
                pages_per_step, t_pos, n_slc):
    bd, n_pages = page_table.shape
    n_steps = n_pages // pages_per_step
    flat_rows = cache_pages.shape[1]
    page_rows = flat_rows // (4 * N_KV)
    n_pad = cos.shape[0]
    n_q = q3.shape[1]
    full = lambda nd: (lambda i, s, pt: (0,) * nd)

    n_pool = cache_pages.shape[0]

    def page_spec(k):
        return pl.BlockSpec(
            (None, flat_rows, HEAD_DIM),
            lambda i, s, pt: (jnp.clip(pt[i, s * pages_per_step + k], 0, n_pool - 1), 0, 0))

    grid_spec = pltpu.PrefetchScalarGridSpec(
        num_scalar_prefetch=1,
        grid=(bd, n_steps),
        in_specs=[page_spec(k) for k in range(pages_per_step)] + [
            pl.BlockSpec((None,) + new_rows.shape[1:], lambda i, s, pt: (i, 0, 0)),
            pl.BlockSpec(w1cat.shape, full(4)),
            pl.BlockSpec(pe8.shape, full(3)),
            pl.BlockSpec(w1f.shape, full(3)),
            pl.BlockSpec(w2.shape, full(3)),
            pl.BlockSpec(kn.shape, full(2)),
            pl.BlockSpec(cos.shape, full(2)),
            pl.BlockSpec(sin.shape, full(2)),
            pl.BlockSpec((None, n_q, HEAD_DIM), lambda i, s, pt: (i, 0, 0)),
            pl.BlockSpec(ov.shape, full(2)),
        ],
        out_specs=[
            pl.BlockSpec((None, n_q, HEAD_DIM), lambda i, s, pt: (i, 0, 0)),
            pl.BlockSpec((None, N_KV, N_SELECT, LANES), lambda i, s, pt: (i, 0, 0, 0)),
        ],
        scratch_shapes=[pltpu.VMEM((N_KV * 2, n_pad, 2 * HEAD_DIM), F32),
                        pltpu.VMEM((N_KV * 2, pages_per_step * page_rows, HEAD_DIM), F32)],
    )
    kern = functools.partial(_sample_cmp_kernel, pages_per_step=pages_per_step, n_steps=n_steps,
                             t_pos=t_pos, n_slc=n_slc)
    return pl.pallas_call(
        kern,
        grid_spec=grid_spec,
        out_shape=[
            jax.ShapeDtypeStruct((bd, n_q, HEAD_DIM), F32),
            jax.ShapeDtypeStruct((bd, N_KV, N_SELECT, LANES), jnp.int32),
        ],
        compiler_params=_params("parallel", "arbitrary"),
        name="sample_cmp",
    )(page_table, *([cache_pages] * pages_per_step), new_rows, w1cat, pe8, w1f, w2, kn, cos, sin, q3, ov)


def _sample_attn_kernel(pt_ref, sel_ref, *refs, t_pos, n_past_blocks):
    del pt_ref
    n_blk = N_KV * N_SELECT
    blk_refs = refs[:n_blk]
    new_kv_ref, new_win_ref, state_ref, q_ref, gate_ref, oc_ref, o_ref = refs[n_blk:]
    n_paged = 4 * N_KV
    n_win = 2 * N_KV
    b = pl.program_id(0)
    q = q_ref[...]
    qf = q.astype(F32)
    n_q = q.shape[0]
    rep = n_q // N_KV
    head = lax.broadcasted_iota(jnp.int32, (n_q, 1), 0)
    n_keys = N_SELECT * SLC_BLOCK
    lane = lax.broadcasted_iota(jnp.int32, (1, n_keys), 1)
    row_b = lax.broadcasted_iota(jnp.int32, (SLC_BLOCK, HEAD_DIM), 0)
    w_keep = state_ref.shape[0] // n_win
    o_s = jnp.zeros((n_q, HEAD_DIM), F32)
    o_w = jnp.zeros((n_q, HEAD_DIM), F32)
    for g in range(N_KV):
        in_group = (head // rep) == g
        k_head = 2 * N_KV + g
        v_head = 3 * N_KV + g
        tail_k = jnp.where(row_b == 0, new_kv_ref[k_head:k_head + 1, :], 0.0)
        tail_v = jnp.where(row_b == 0, new_kv_ref[v_head:v_head + 1, :], 0.0)
        k_parts, v_parts = [], []
        base = jnp.zeros((1, n_keys), jnp.int32)
        for k in range(N_SELECT):
            blk = sel_ref[b, g * N_SELECT + k]
            is_tail = blk >= n_past_blocks
            blk_ref = blk_refs[g * N_SELECT + k]
            k_rows = blk_ref[pl.ds(k_head, SLC_BLOCK, stride=n_paged), :]
            v_rows = blk_ref[pl.ds(v_head, SLC_BLOCK, stride=n_paged), :]
            k_parts.append(jnp.where(is_tail, tail_k, k_rows).astype(BF16))
            v_parts.append(jnp.where(is_tail, tail_v, v_rows).astype(BF16))
            base = jnp.where(lane // SLC_BLOCK == k, blk * SLC_BLOCK, base)
        keys = jnp.concatenate(k_parts, axis=0)
        vals = jnp.concatenate(v_parts, axis=0)
        tok = base + lane % SLC_BLOCK
        s = lax.dot_general(q, keys, _NT, preferred_element_type=F32)
        p = _softmax_rows(s, tok <= t_pos)
        o_s = jnp.where(in_group, jnp.dot(p.astype(BF16), vals, preferred_element_type=F32), o_s)

        kw = state_ref[pl.ds(g, w_keep, stride=n_win), :].astype(BF16)
        vw = state_ref[pl.ds(N_KV + g, w_keep, stride=n_win), :].astype(BF16)
        kw_new = new_win_ref[g:g + 1, :]
        vw_new = new_win_ref[N_KV + g:N_KV + g + 1, :]
        dist = w_keep - lax.broadcasted_iota(jnp.int32, (1, w_keep), 1)
        m_w = (dist >= 0) & (dist < WINDOW) & (t_pos - dist >= 0)
        s_w = jnp.where(m_w, lax.dot_general(q, kw, _NT, preferred_element_type=F32), NEG)
        s_new = jnp.sum(qf * kw_new, axis=-1, keepdims=True)
        m = jnp.maximum(jnp.max(s_w, axis=-1, keepdims=True), s_new)
        e_w = jnp.exp(s_w - m)
        e_new = jnp.exp(s_new - m)
        denom = jnp.sum(e_w, axis=-1, keepdims=True) + e_new
        num = jnp.dot(e_w.astype(BF16), vw, preferred_element_type=F32) + e_new * vw_new
        o_w = jnp.where(in_group, num / denom, o_w)

    gates = jnp.broadcast_to(gate_ref[...], (n_q, LANES))
    glane = lax.broadcasted_iota(jnp.int32, (n_q, LANES), 1)
    gsel = [jnp.sum(jnp.where(glane == head * N_BRANCH + br, gates, 0.0), axis=-1, keepdims=True)
            for br in range(N_BRANCH)]
    o_ref[...] = (gsel[0] * oc_ref[...] + gsel[1] * o_s + gsel[2] * o_w).astype(BF16)


def _sample_attn(page_table, sel_idx, cache_blocks, new_kv, new_win, state3, q3, gates3, o_c,
                 t_pos, n_past_blocks, blocks_per_page):
    bd = page_table.shape[0]
    n_q = q3.shape[1]

    n_pool = cache_blocks.shape[0] // blocks_per_page

    def blk_spec(g, k):
        def index(i, pt, sel):
            blk = jnp.clip(sel[i, g * N_SELECT + k], 0, n_past_blocks - 1)
            page = jnp.clip(pt[i, blk // blocks_per_page], 0, n_pool - 1)
            return (page * blocks_per_page + blk % blocks_per_page, 0, 0)
        return pl.BlockSpec((None,) + cache_blocks.shape[1:], index)

    blk_specs = [blk_spec(g, k) for g in range(N_KV) for k in range(N_SELECT)]
    per_b = lambda shape: pl.BlockSpec((None,) + shape, lambda i, pt, sel: (i, 0, 0))
    grid_spec = pltpu.PrefetchScalarGridSpec(
        num_scalar_prefetch=2,
        grid=(bd,),
        in_specs=blk_specs + [
            per_b(new_kv.shape[1:]),
            per_b(new_win.shape[1:]),
            per_b(state3.shape[1:]),
            per_b((n_q, HEAD_DIM)),
            per_b((1, LANES)),
            per_b((n_q, HEAD_DIM)),
        ],
        out_specs=per_b((n_q, HEAD_DIM)),
    )
    kern = functools.partial(_sample_attn_kernel, t_pos=t_pos, n_past_blocks=n_past_blocks)
    n_blk = N_KV * N_SELECT
    return pl.pallas_call(
        kern,
        grid_spec=grid_spec,
        out_shape=jax.ShapeDtypeStruct((bd, n_q, HEAD_DIM), BF16),
        compiler_params=_params("parallel"),
        name="sample_attn",
    )(page_table, sel_idx, *([cache_blocks] * n_blk), new_kv, new_win, state3, q3, gates3, o_c)


def kernel(x_prompt, x_sample, cache_kv, state_kv_win, page_table, a_norm, a_w_in, a_v_norm, a_w_s,
           a_b_s, a_w_out, mlp_norm, mlp_w_up, mlp_w_down, kv_norm, w_kv, cmp_pe, cmp_w1, cmp_w2,
           k_norm, b_norm, b_w_in, b_q_norm, b_w_out):
    bp, t, d = x_prompt.shape
    bd, td, _ = x_sample.shape
    n_pool, page_size = cache_kv.shape[:2]
    n_pages = page_table.shape[1]
    past_len = n_pages * page_size
    w_keep = state_kv_win.shape[1]
    depth = mlp_norm.shape[0]
    n_a = a_norm.shape[0]
    n_b = b_norm.shape[0]
    d_q = b_w_out.shape[1]
    assert td == 1 and n_b == 1 and depth == n_a + n_b
    assert d // A_GROUPS == LANES and t % KV_TILE == 0 and t >= Q_BLOCK + WINDOW
    assert w_keep == WINDOW and page_size % SLC_BLOCK == 0 and past_len % KV_ALIGN == 0

    row = lambda v: v.reshape(1, -1)
    hp = x_prompt.reshape(bp * t, d)
    hs = x_sample.reshape(bd, d)
    tm = 512

    v_rows = []
    for l in range(n_a):
        w_in = a_w_in[l].astype(BF16)
        w_out = a_w_out[l].astype(BF16)
        w_up = mlp_w_up[l].astype(BF16)
        w_down = mlp_w_down[l].astype(BF16)
        zp = _gmlp_in(hp, row(a_norm[l]), w_in, tm, 512)
        hp = _gmlp_out(hp, zp, row(a_v_norm[l]), a_w_s[l], a_b_s[l].T, w_out, 256)
        hp = _mlp(hp, row(mlp_norm[l]), w_up, w_down, tm, 512)
        zs = _gmlp_in(hs, row(a_norm[l]), w_in, bd, 512)
        wd = jnp.repeat(a_w_s[l][:, 0, 0], d // A_GROUPS).reshape(1, d)
        bb = jnp.repeat(a_b_s[l][:, 0], d // A_GROUPS).reshape(1, d)
        hs, v_s = _gmlp_out_single(hs, zs, row(a_v_norm[l]), wd, bb, w_out)
        v_rows.append(v_s.reshape(bd, td, d))
        hs = _mlp(hs, row(mlp_norm[l]), w_up, w_down, bd, 512)

    w_kv_b = w_kv.astype(BF16)
    cos_p, sin_p = _rope_tables(jnp.arange(t))
    cos_pp, sin_pp = jnp.tile(cos_p, (bp, 1)), jnp.tile(sin_p, (bp, 1))
    cos_s, sin_s = _rope_tables(jnp.full((bd,), past_len))
    paged_p, win_p, cmp_rows_p, kvb_p = _kv_proj(hp, row(kv_norm), w_kv_b, k_norm, cos_pp, sin_pp, tm)
    paged_s, win_s, _, _ = _kv_proj(hs, row(kv_norm), w_kv_b, k_norm, cos_s, sin_s, bd)

    j = 0
    wq = b_w_in[j][:, :d_q].astype(BF16)
    wg = jnp.pad(b_w_in[j][:, d_q:], ((0, 0), (0, LANES - (b_w_in.shape[2] - d_q)))).astype(BF16)
    q_p, gates_p = _q_proj(hp, row(b_norm[j]), wq, wg, row(b_q_norm[j]), cos_pp, sin_pp, tm)
    q_s, gates_s = _q_proj(hs, row(b_norm[j]), wq, wg, row(b_q_norm[j]), cos_s, sin_s, bd)

    w1cat = jnp.concatenate([cmp_w1[:, :CMP_STRIDE], cmp_w1[:, CMP_STRIDE:]], axis=-1).astype(BF16)
    w1f = cmp_w1.reshape(2, CMP_BLOCK * HEAD_DIM, HEAD_DIM).astype(BF16)
    pe8 = jnp.broadcast_to(cmp_pe.reshape(2, 1, CMP_BLOCK * HEAD_DIM),
                           (2, SUBLANES, CMP_BLOCK * HEAD_DIM)).astype(BF16)
    w2 = cmp_w2.astype(BF16)

    n_sub_p = t // CMP_STRIDE
    n_slc_p = t // SLC_BLOCK
    cos_c, sin_c = _rope_tables(jnp.arange(n_sub_p) * CMP_STRIDE + CMP_BLOCK - 1)
    kc_p, vc_p = _compress_prompt(cmp_rows_p, w1cat, pe8, w1f, w2, k_norm, cos_c, sin_c, bp, t)
    ovt = _overlap_matrix(n_sub_p - 1, n_slc_p, n_sub_p, n_slc_p).T
    expand = (jnp.arange(LANES)[:, None] == (jnp.arange(t)[None, :] // SLC_BLOCK)).astype(BF16)
    o_p = _attn_prompt(q_p, gates_p, kc_p, vc_p, kvb_p, ovt, expand, bp, t)
    w_o = b_w_out[j].astype(BF16)
    hp = _out_proj(hp, o_p, w_o, tm)

    tp_s = -(-(past_len + td) // KV_ALIGN) * KV_ALIGN
    n_sub_s = tp_s // CMP_STRIDE
    n_slc_s = tp_s // SLC_BLOCK
    n_pad_s = -(-n_sub_s // SUBLANES) * SUBLANES
    sel_lanes = -(-n_slc_s // LANES) * LANES
    assert past_len // SLC_BLOCK + 1 >= N_SELECT
    cos_cs, sin_cs = _rope_tables(jnp.arange(n_pad_s) * CMP_STRIDE + CMP_BLOCK - 1)
    ov_s = _overlap_matrix(n_sub_s - 1, n_slc_s, n_pad_s, sel_lanes)
    n_paged = 4 * N_KV
    n_win = 2 * N_KV
    cache_pages = cache_kv.reshape(n_pool, page_size * n_paged, HEAD_DIM)
    q3 = q_s.reshape(bd, d_q // HEAD_DIM, HEAD_DIM)
    new_kv = paged_s.reshape(bd, n_paged, HEAD_DIM)
    new_win = jnp.pad(win_s.reshape(bd, n_win, HEAD_DIM), ((0, 0), (0, SUBLANES - n_win), (0, 0)))
    o_c, sel_idx = _sample_cmp(page_table, cache_pages, new_kv, w1cat, pe8, w1f, w2, k_norm, cos_cs,
                               sin_cs, q3, ov_s, min(16, n_pages), past_len, n_slc_s)
    blocks_per_page = page_size // SLC_BLOCK
    cache_blocks = cache_kv.reshape(n_pool * blocks_per_page, SLC_BLOCK * n_paged, HEAD_DIM)
    state3 = state_kv_win.reshape(bd, w_keep * n_win, HEAD_DIM)
    o_s = _sample_attn(page_table, sel_idx[..., 0].reshape(bd, N_KV * N_SELECT), cache_blocks, new_kv,
                       new_win, state3, q3, gates_s.reshape(bd, 1, LANES), o_c, past_len,
                       past_len // SLC_BLOCK, blocks_per_page)
    hs = _out_proj(hs, o_s.reshape(bd, d_q), w_o, bd)

    l = n_a
    w_up = mlp_w_up[l].astype(BF16)
    w_down = mlp_w_down[l].astype(BF16)
    hp = _mlp(hp, row(mlp_norm[l]), w_up, w_down, tm, 512)
    hs = _mlp(hs, row(mlp_norm[l]), w_up, w_down, bd, 512)

    y_p = hp.reshape(bp, t, d)
    y_s = hs.reshape(bd, td, d)
    kv_p = paged_p.reshape(bp, t, 4, N_KV, HEAD_DIM)
    win_all_p = win_p.reshape(bp, t, 2, N_KV, HEAD_DIM)
    win_new_p = win_all_p[:, t - min(WINDOW, t):]
    kv_s = paged_s.reshape(bd, td, 4, N_KV, HEAD_DIM)
    win_new_s = jnp.concatenate(
        [state_kv_win[:, td:], win_s.reshape(bd, td, 2, N_KV, HEAD_DIM)], axis=1)
    v_a_s = jnp.stack(v_rows, axis=0)
    return (y_p, y_s, kv_p, win_new_p, kv_s, win_new_s, v_a_s)
```

```python
import functools

import jax
import jax.numpy as jnp
from jax import lax
from jax.experimental import pallas as pl
from jax.experimental.pallas import tpu as pltpu

F32 = jnp.float32
BF16 = jnp.bfloat16

CHUNK = 128
A_GROUPS = 16
HEAD_DIM = 128
N_KV = 2
CMP_STRIDE = 16
CMP_BLOCK = 2 * CMP_STRIDE
SLC_BLOCK = 64
N_SELECT = 16
WINDOW = 512
N_BRANCH = 3
KV_ALIGN = 64
ROT_DIM = HEAD_DIM // 4
ROPE_THETA = 500000.0
Q_BLOCK = 128
EPS = 1e-6
NEG = -1e30
FORCE_BONUS = 1e4

LANES = 128
SUBLANES = 8
VMEM_LIMIT_BYTES = 56 * 1024 * 1024

KV_TILE = 512
N_KINDS = 6

_NT = (((1,), (1,)), ((), ()))


def _params(*sem):
    return pltpu.CompilerParams(dimension_semantics=sem, vmem_limit_bytes=VMEM_LIMIT_BYTES)


def _rms(x, g):
    ms = jnp.mean(x * x, axis=-1, keepdims=True)
    return x * lax.rsqrt(ms + EPS) * g


def _rope(x, cos, sin):
    half = ROT_DIM // 2
    lane = lax.broadcasted_iota(jnp.int32, x.shape, 1)
    partner = jnp.where(lane < half, pltpu.roll(x, LANES - half, 1), pltpu.roll(x, half, 1))
    return x * cos + partner * sin


def _rope_tables(pos):
    inv = ROPE_THETA ** (-jnp.arange(0, ROT_DIM, 2, dtype=F32) / ROT_DIM)
    ang = pos.astype(F32)[:, None] * inv[None, :]
    c, s = jnp.cos(ang), jnp.sin(ang)
    n = pos.shape[0]
    pad = LANES - ROT_DIM
    cos = jnp.concatenate([c, c, jnp.ones((n, pad), F32)], axis=1)
    sin = jnp.concatenate([-s, s, jnp.zeros((n, pad), F32)], axis=1)
    return cos, sin


def _gmlp_in_kernel(x_ref, g_ref, w_ref, z_ref, xn_ref):
    @pl.when(pl.program_id(1) == 0)
    def _():
        xn_ref[...] = _rms(x_ref[...], g_ref[...]).astype(BF16)

    z_ref[...] = jax.nn.gelu(jnp.dot(xn_ref[...], w_ref[...], preferred_element_type=F32))


def _gmlp_in(x, g, w, tm, tn):
    m, d = x.shape
    n = w.shape[1]
    return pl.pallas_call(
        _gmlp_in_kernel,
        grid=(m // tm, n // tn),
        in_specs=[
            pl.BlockSpec((tm, d), lambda i, j: (i, 0)),
            pl.BlockSpec((1, d), lambda i, j: (0, 0)),
            pl.BlockSpec((d, tn), lambda i, j: (0, j)),
        ],
        out_specs=pl.BlockSpec((tm, tn), lambda i, j: (i, j)),
        out_shape=jax.ShapeDtypeStruct((m, n), F32),
        scratch_shapes=[pltpu.VMEM((tm, d), BF16)],
        compiler_params=_params("parallel", "arbitrary"),
        name="gmlp_in",
    )(x, g, w)


def _gmlp_out_kernel(x_ref, u_ref, v_ref, vg_ref, ws_ref, bt_ref, wo_ref, o_ref, vn_ref, y_ref):
    tm = x_ref.shape[0]
    vn_ref[...] = _rms(v_ref[...], vg_ref[...]).astype(BF16)
    row = lax.broadcasted_iota(jnp.int32, (CHUNK, CHUNK), 0)
    col = lax.broadcasted_iota(jnp.int32, (CHUNK, CHUNK), 1)
    causal = row >= col
    for g in range(A_GROUPS):
        wsg = jnp.where(causal, ws_ref[g], 0.0).astype(BF16)
        bias = bt_ref[:, g:g + 1]
        cs = slice(g * LANES, (g + 1) * LANES)
        for c in range(tm // CHUNK):
            rs = slice(c * CHUNK, (c + 1) * CHUNK)
            mixed = jnp.dot(wsg, vn_ref[rs, cs], preferred_element_type=F32) + bias
            y_ref[rs, cs] = (u_ref[rs, cs] * mixed).astype(BF16)
    o_ref[...] = x_ref[...] + jnp.dot(y_ref[...], wo_ref[...], preferred_element_type=F32)


def _gmlp_out(x, z, vg, ws, bt, wo, tm):
    m, d = x.shape
    return pl.pallas_call(
        _gmlp_out_kernel,
        grid=(m // tm,),
        in_specs=[
            pl.BlockSpec((tm, d), lambda i: (i, 0)),
            pl.BlockSpec((tm, d), lambda i: (i, 0)),
            pl.BlockSpec((tm, d), lambda i: (i, 1)),
            pl.BlockSpec((1, d), lambda i: (0, 0)),
            pl.BlockSpec((A_GROUPS, CHUNK, CHUNK), lambda i: (0, 0, 0)),
            pl.BlockSpec((CHUNK, A_GROUPS), lambda i: (0, 0)),
            pl.BlockSpec((d, d), lambda i: (0, 0)),
        ],
        out_specs=pl.BlockSpec((tm, d), lambda i: (i, 0)),
        out_shape=jax.ShapeDtypeStruct((m, d), F32),
        scratch_shapes=[pltpu.VMEM((tm, d), BF16), pltpu.VMEM((tm, d), BF16)],
        compiler_params=_params("parallel"),
        name="gmlp_out",
    )(x, z, z, vg, ws, bt, wo)


def _gmlp_out_single_kernel(x_ref, u_ref, v_ref, vg_ref, wd_ref, bb_ref, wo_ref, o_ref, vn_ref):
    vn = _rms(v_ref[...], vg_ref[...])
    vn_ref[...] = vn
    mixed = vn * wd_ref[...] + bb_ref[...]
    y = (u_ref[...] * mixed).astype(BF16)
    o_ref[...] = x_ref[...] + jnp.dot(y, wo_ref[...], preferred_element_type=F32)


def _gmlp_out_single(x, z, vg, wd, bb, wo):
    m, d = x.shape
    full = lambda i: (0, 0)
    return pl.pallas_call(
        _gmlp_out_single_kernel,
        grid=(1,),
        in_specs=[
            pl.BlockSpec((m, d), full),
            pl.BlockSpec((m, d), lambda i: (0, 0)),
            pl.BlockSpec((m, d), lambda i: (0, 1)),
            pl.BlockSpec((1, d), full),
            pl.BlockSpec((1, d), full),
            pl.BlockSpec((1, d), full),
            pl.BlockSpec((d, d), full),
        ],
        out_specs=[pl.BlockSpec((m, d), full), pl.BlockSpec((m, d), full)],
        out_shape=[jax.ShapeDtypeStruct((m, d), F32), jax.ShapeDtypeStruct((m, d), F32)],
        compiler_params=_params("arbitrary"),
        name="gmlp_out_single",
    )(x, z, z, vg, wd, bb, wo)


def _mlp_kernel(x_ref, g_ref, wu_ref, wd_ref, o_ref, xn_ref):
    @pl.when(pl.program_id(1) == 0)
    def _():
        x = x_ref[...]
        xn_ref[...] = _rms(x, g_ref[...]).astype(BF16)
        o_ref[...] = x

    h = jnp.dot(xn_ref[...], wu_ref[...], preferred_element_type=F32)
    a = jnp.square(jnp.maximum(h, 0.0)).astype(BF16)
    o_ref[...] += jnp.dot(a, wd_ref[...], preferred_element_type=F32)


def _mlp(x, g, wu, wd, tm, tf):
    m, d = x.shape
    f = wu.shape[1]
    return pl.pallas_call(
        _mlp_kernel,
        grid=(m // tm, f // tf),
        in_specs=[
            pl.BlockSpec((tm, d), lambda i, j: (i, 0)),
            pl.BlockSpec((1, d), lambda i, j: (0, 0)),
            pl.BlockSpec((d, tf), lambda i, j: (0, j)),
            pl.BlockSpec((tf, d), lambda i, j: (j, 0)),
        ],
        out_specs=pl.BlockSpec((tm, d), lambda i, j: (i, 0)),
        out_shape=jax.ShapeDtypeStruct((m, d), F32),
        scratch_shapes=[pltpu.VMEM((tm, d), BF16)],
        compiler_params=_params("parallel", "arbitrary"),
        name="mlp",
    )(x, g, wu, wd)


def _kv_proj_kernel(x_ref, g_ref, w_ref, kn_ref, cos_ref, sin_ref, paged_ref, win_ref, cmp_ref, kvb_ref):
    tm = x_ref.shape[0]
    n_paged = 4 * N_KV
    n_win = 2 * N_KV
    xn = _rms(x_ref[...], g_ref[...]).astype(BF16)
    kv = jnp.dot(xn, w_ref[...], preferred_element_type=F32)
    cos, sin = cos_ref[...], sin_ref[...]
    for c in range(N_KINDS * N_KV):
        kind = c // N_KV
        h = kv[:, c * LANES:(c + 1) * LANES]
        if kind == 2:
            h = _rope(_rms(h, kn_ref[1:2, :]), cos, sin)
        elif kind == 4:
            h = _rope(_rms(h, kn_ref[2:3, :]), cos, sin)
        if kind < 4:
            paged_ref[pl.ds(c, tm, stride=n_paged), :] = h
        else:
            win_ref[pl.ds(c - n_paged, tm, stride=n_win), :] = h
        if kind < 2:
            cmp_ref[c] = h
        else:
            kvb_ref[c - 2 * N_KV] = h.astype(BF16)


def _kv_proj(x, g, w, kn, cos, sin, tm):
    m, d = x.shape
    n = w.shape[1]
    n_paged = 4 * N_KV
    n_win = 2 * N_KV
    return pl.pallas_call(
        _kv_proj_kernel,
        grid=(m // tm,),
        in_specs=[
            pl.BlockSpec((tm, d), lambda i: (i, 0)),
            pl.BlockSpec((1, d), lambda i: (0, 0)),
            pl.BlockSpec((d, n), lambda i: (0, 0)),
            pl.BlockSpec((3, HEAD_DIM), lambda i: (0, 0)),
            pl.BlockSpec((tm, LANES), lambda i: (i, 0)),
            pl.BlockSpec((tm, LANES), lambda i: (i, 0)),
        ],
        out_specs=[
            pl.BlockSpec((tm * n_paged, HEAD_DIM), lambda i: (i, 0)),
            pl.BlockSpec((tm * n_win, HEAD_DIM), lambda i: (i, 0)),
            pl.BlockSpec((2 * N_KV, tm, HEAD_DIM), lambda i: (0, i, 0)),
            pl.BlockSpec((4 * N_KV, tm, HEAD_DIM), lambda i: (0, i, 0)),
        ],
        out_shape=[
            jax.ShapeDtypeStruct((m * n_paged, HEAD_DIM), F32),
            jax.ShapeDtypeStruct((m * n_win, HEAD_DIM), F32),
            jax.ShapeDtypeStruct((2 * N_KV, m, HEAD_DIM), F32),
            jax.ShapeDtypeStruct((4 * N_KV, m, HEAD_DIM), BF16),
        ],
        compiler_params=_params("parallel"),
        name="kv_proj",
    )(x, g, w, kn, cos, sin)


def _q_proj_kernel(x_ref, g_ref, wq_ref, wg_ref, qn_ref, cos_ref, sin_ref, q_ref, gate_ref):
    xn = _rms(x_ref[...], g_ref[...]).astype(BF16)
    z = jnp.dot(xn, wq_ref[...], preferred_element_type=F32)
    cos, sin = cos_ref[...], sin_ref[...]
    qn = qn_ref[...]
    scale = HEAD_DIM ** -0.5
    for h in range(z.shape[1] // HEAD_DIM):
        cs = slice(h * HEAD_DIM, (h + 1) * HEAD_DIM)
        qh = _rope(_rms(z[:, cs], qn), cos, sin) * scale
        q_ref[:, cs] = qh.astype(BF16)
    gate_ref[...] = jax.nn.sigmoid(jnp.dot(xn, wg_ref[...], preferred_element_type=F32))


def _q_proj(x, g, wq, wg, qn, cos, sin, tm):
    m, d = x.shape
    n = wq.shape[1]
    return pl.pallas_call(
        _q_proj_kernel,
        grid=(m // tm,),
        in_specs=[
            pl.BlockSpec((tm, d), lambda i: (i, 0)),
            pl.BlockSpec((1, d), lambda i: (0, 0)),
            pl.BlockSpec((d, n), lambda i: (0, 0)),
            pl.BlockSpec((d, LANES), lambda i: (0, 0)),
            pl.BlockSpec((1, HEAD_DIM), lambda i: (0, 0)),
            pl.BlockSpec((tm, LANES), lambda i: (i, 0)),
            pl.BlockSpec((tm, LANES), lambda i: (i, 0)),
        ],
        out_specs=[
            pl.BlockSpec((tm, n), lambda i: (i, 0)),
            pl.BlockSpec((tm, LANES), lambda i: (i, 0)),
        ],
        out_shape=[
            jax.ShapeDtypeStruct((m, n), BF16),
            jax.ShapeDtypeStruct((m, LANES), F32),
        ],
        compiler_params=_params("parallel"),
        name="q_proj",
    )(x, g, wq, wg, qn, cos, sin)


def _out_proj_kernel(h_ref, o_ref, w_ref, y_ref):
    y_ref[...] = h_ref[...] + jnp.dot(o_ref[...], w_ref[...], preferred_element_type=F32)


def _out_proj(h, o, w, tm):
    m, d = h.shape
    k = o.shape[1]
    return pl.pallas_call(
        _out_proj_kernel,
        grid=(m // tm,),
        in_specs=[
            pl.BlockSpec((tm, d), lambda i: (i, 0)),
            pl.BlockSpec((tm, k), lambda i: (i, 0)),
            pl.BlockSpec((k, d), lambda i: (0, 0)),
        ],
        out_specs=pl.BlockSpec((tm, d), lambda i: (i, 0)),
        out_shape=jax.ShapeDtypeStruct((m, d), F32),
        compiler_params=_params("parallel"),
        name="out_proj",
    )(h, o, w)


def _cmp_bias(pe_ref, w1f_ref, kind):
    return jnp.dot(pe_ref[kind], w1f_ref[kind], preferred_element_type=F32)[0:1, :]


def _cmp_finish(fs, bias, w2, n_rows):
    first = fs[:, :HEAD_DIM]
    second = pltpu.roll(fs[:, HEAD_DIM:], n_rows - 1, 0)
    h = first + second + bias
    h = h * jax.nn.sigmoid(h)
    return jnp.dot(h.astype(BF16), w2, preferred_element_type=F32)


def _compress_prompt_kernel(krows_ref, vrows_ref, w1_ref, pe_ref, w1f_ref, w2_ref, kn_ref,
                            cos_ref, sin_ref, kc_ref, vc_ref):
    n_sub = krows_ref.shape[0] // CMP_STRIDE
    rowi = lax.broadcasted_iota(jnp.int32, (n_sub, HEAD_DIM), 0)
    for kind, rows_ref, out_ref in ((0, krows_ref, kc_ref), (1, vrows_ref, vc_ref)):
        x = jnp.concatenate(
            [rows_ref[pl.ds(r, n_sub, stride=CMP_STRIDE), :].astype(BF16) for r in range(CMP_STRIDE)],
            axis=1)
        fs = jnp.dot(x, w1_ref[kind], preferred_element_type=F32)
        out = _cmp_finish(fs, _cmp_bias(pe_ref, w1f_ref, kind), w2_ref[kind], n_sub)
        if kind == 0:
            out = _rope(_rms(out, kn_ref[0:1, :]), cos_ref[...], sin_ref[...])
        out_ref[...] = jnp.where(rowi < n_sub - 1, out, 0.0).astype(BF16)


def _compress_prompt(paged, w1cat, pe8, w1f, w2, kn, cos, sin, b, t):
    n_sub = t // CMP_STRIDE
    full = lambda nd: (lambda i, g: (0,) * nd)
    out_spec = pl.BlockSpec((None, None, n_sub, HEAD_DIM), lambda i, g: (i, g, 0, 0))
    out_shape = jax.ShapeDtypeStruct((b, N_KV, n_sub, HEAD_DIM), BF16)
    return pl.pallas_call(
        _compress_prompt_kernel,
        grid=(b, N_KV),
        in_specs=[
            pl.BlockSpec((None, t, HEAD_DIM), lambda i, g: (g, i, 0)),
            pl.BlockSpec((None, t, HEAD_DIM), lambda i, g: (N_KV + g, i, 0)),
            pl.BlockSpec(w1cat.shape, full(3)),
            pl.BlockSpec(pe8.shape, full(3)),
            pl.BlockSpec(w1f.shape, full(3)),
            pl.BlockSpec(w2.shape, full(3)),
            pl.BlockSpec(kn.shape, full(2)),
            pl.BlockSpec(cos.shape, full(2)),
            pl.BlockSpec(sin.shape, full(2)),
        ],
        out_specs=[out_spec, out_spec],
        out_shape=[out_shape, out_shape],
        compiler_params=_params("parallel", "parallel"),
        name="compress_prompt",
    )(paged, paged, w1cat, pe8, w1f, w2, kn, cos, sin)


def _split_bf16(x):
    hi = x.astype(BF16)
    lo = (x - hi.astype(F32)).astype(BF16)
    return hi, lo


def _overlap_matrix(n_cmp, n_slc, rows, cols):
    ci = jnp.arange(rows)[:, None] * CMP_STRIDE
    sj = jnp.arange(cols)[None, :] * SLC_BLOCK
    ov = (ci < sj + SLC_BLOCK) & (ci + CMP_BLOCK > sj)
    ov = ov & (jnp.arange(rows)[:, None] < n_cmp) & (jnp.arange(cols)[None, :] < n_slc)
    return ov.astype(BF16)


def _softmax_rows(s, valid):
    s = jnp.where(valid, s, NEG)
    e = jnp.exp(s - jnp.max(s, axis=-1, keepdims=True))
    return e / jnp.sum(e, axis=-1, keepdims=True)


def _attn_prompt_kernel(q_ref, gate_ref, kc_ref, vc_ref, ks_ref, vs_ref, kw_ref, vw_ref,
                        ovt_ref, exp_ref, o_ref, m_ref, acc_ref, bias_ref):
    qb = pl.program_id(2)
    g = pl.program_id(1)
    nq = Q_BLOCK
    rep = q_ref.shape[1] // HEAD_DIM
    n_cmp_pad = kc_ref.shape[0]
    n_slc = ovt_ref.shape[0]
    t_len = ks_ref.shape[0]
    s0 = qb * nq

    q_all = q_ref[...]
    q2 = jnp.concatenate([q_all[:, r * HEAD_DIM:(r + 1) * HEAD_DIM] for r in range(rep)], axis=0)
    t_col = s0 + lax.broadcasted_iota(jnp.int32, (nq, 1), 0)

    kc = kc_ref[...]
    vc = vc_ref[...]
    cpos = lax.broadcasted_iota(jnp.int32, (1, n_cmp_pad), 1) * CMP_STRIDE + (CMP_BLOCK - 1)
    m_c = cpos <= t_col
    s_c = lax.dot_general(q2, kc, _NT, preferred_element_type=F32)
    psum = jnp.zeros((nq, n_cmp_pad), F32)
    p_parts = []
    for r in range(rep):
        p = jnp.where(m_c, _softmax_rows(s_c[r * nq:(r + 1) * nq], m_c), 0.0)
        psum = psum + p
        p_parts.append(p.astype(BF16))
    o_c = jnp.dot(jnp.concatenate(p_parts, axis=0), vc, preferred_element_type=F32)

    ovt = ovt_ref[...]
    p_hi, p_lo = _split_bf16(psum)
    imp_t = (lax.dot_general(ovt, p_hi, _NT, preferred_element_type=F32)
             + lax.dot_general(ovt, p_lo, _NT, preferred_element_type=F32))
    t_row = s0 + lax.broadcasted_iota(jnp.int32, (n_slc, nq), 1)
    blk = lax.broadcasted_iota(jnp.int32, (n_slc, nq), 0)
    cur = t_row // SLC_BLOCK
    causal = blk * SLC_BLOCK <= t_row
    forced = (blk == 0) | (blk == cur) | (blk == cur - 1)
    score = jnp.where(causal, imp_t + jnp.where(forced, FORCE_BONUS, 0.0), NEG)
    rank = jnp.zeros((n_slc, nq), jnp.int32)
    for i in range(n_slc):
        si = score[i:i + 1, :]
        later = (blk > i).astype(jnp.int32)
        rank = rank + jnp.where(si > score, 1, 0) + jnp.where(si == score, later, 0)
    sel_t = jnp.where(causal & (rank < N_SELECT), 1.0, 0.0)
    sel_t = jnp.concatenate([sel_t, jnp.zeros((LANES - n_slc, nq), F32)], axis=0)
    sel = sel_t.T.astype(BF16)

    n_tiles = (s0 + nq + KV_TILE - 1) // KV_TILE
    n_chunks = KV_TILE // LANES
    m_ref[...] = jnp.full(m_ref.shape, NEG, F32)

    def max_body(j, carry):
        off = pl.multiple_of(j * KV_TILE, KV_TILE)
        k = ks_ref[pl.ds(off, KV_TILE), :]
        picked = jnp.dot(sel, exp_ref[:, pl.ds(off, KV_TILE)], preferred_element_type=F32)
        kpos = off + lax.broadcasted_iota(jnp.int32, (nq, KV_TILE), 1)
        bias = jnp.where((picked > 0.5) & (kpos <= t_col), 0.0, NEG)
        bias_ref[:, pl.ds(off, KV_TILE)] = bias
        s = lax.dot_general(q2, k, _NT, preferred_element_type=F32)
        for r in range(rep):
            rs = slice(r * nq, (r + 1) * nq)
            m_run = m_ref[rs, :]
            for c in range(n_chunks):
                cs = slice(c * LANES, (c + 1) * LANES)
                m_run = jnp.maximum(m_run, s[rs, cs] + bias[:, cs])
            m_ref[rs, :] = m_run
        return carry

    lax.fori_loop(0, n_tiles, max_body, 0)
    m_ref[...] = jnp.broadcast_to(jnp.max(m_ref[...], axis=-1, keepdims=True), m_ref.shape)
    acc_ref[...] = jnp.zeros(acc_ref.shape, F32)
    ones = jnp.ones((KV_TILE, HEAD_DIM), BF16)

    def pv_body(j, carry):
        off = pl.multiple_of(j * KV_TILE, KV_TILE)
        k = ks_ref[pl.ds(off, KV_TILE), :]
        v1 = jnp.concatenate([vs_ref[pl.ds(off, KV_TILE), :], ones], axis=1)
        s = lax.dot_general(q2, k, _NT, preferred_element_type=F32)
        p_parts = []
        for r in range(rep):
            rs = slice(r * nq, (r + 1) * nq)
            m_row = m_ref[rs, :]
            chunks = []
            for c in range(n_chunks):
                cs = slice(c * LANES, (c + 1) * LANES)
                b_c = bias_ref[:, pl.ds(pl.multiple_of(off + c * LANES, LANES), LANES)]
                chunks.append(jnp.exp(s[rs, cs] + b_c - m_row).astype(BF16))
            p_parts.append(jnp.concatenate(chunks, axis=1))
        acc_ref[...] += jnp.dot(jnp.concatenate(p_parts, axis=0), v1, preferred_element_type=F32)
        return carry

    lax.fori_loop(0, n_tiles, pv_body, 0)
    o_s = acc_ref[:, :HEAD_DIM] / acc_ref[:, HEAD_DIM:]

    slab = nq + WINDOW
    w0 = pl.multiple_of(jnp.maximum(s0 - WINDOW, 0), nq)
    kw = kw_ref[pl.ds(w0, slab), :]
    vw = vw_ref[pl.ds(w0, slab), :]
    dist = t_col - (w0 + lax.broadcasted_iota(jnp.int32, (nq, slab), 1))
    m_w = (dist >= 0) & (dist < WINDOW)
    s_w = lax.dot_general(q2, kw, _NT, preferred_element_type=F32)
    p_parts = [_softmax_rows(s_w[r * nq:(r + 1) * nq], m_w).astype(BF16) for r in range(rep)]
    o_w = jnp.dot(jnp.concatenate(p_parts, axis=0), vw, preferred_element_type=F32)

    gates = gate_ref[...]
    for r in range(rep):
        rs = slice(r * nq, (r + 1) * nq)
        c0 = (g * rep + r) * N_BRANCH
        lane = lax.broadcasted_iota(jnp.int32, gates.shape, 1)
        gsel = [jnp.sum(jnp.where(lane == c0 + br, gates, 0.0), axis=-1, keepdims=True)
                for br in range(N_BRANCH)]
        out = gsel[0] * o_c[rs] + gsel[1] * o_s[rs] + gsel[2] * o_w[rs]
        o_ref[:, r * HEAD_DIM:(r + 1) * HEAD_DIM] = out.astype(BF16)


def _attn_prompt(q, gates, kc, vc, kvb, ovt, expand, b, t):
    nqb = t // Q_BLOCK
    rep = q.shape[1] // HEAD_DIM // N_KV
    n_cmp_pad = kc.shape[2]
    rows = lambda kind: pl.BlockSpec((None, t, HEAD_DIM), lambda i, g, j: ((kind - 2) * N_KV + g, i, 0))
    cmp_spec = pl.BlockSpec((None, None, n_cmp_pad, HEAD_DIM), lambda i, g, j: (i, g, 0, 0))
    return pl.pallas_call(
        _attn_prompt_kernel,
        grid=(b, N_KV, nqb),
        in_specs=[
            pl.BlockSpec((Q_BLOCK, rep * HEAD_DIM), lambda i, g, j: (i * nqb + j, g)),
            pl.BlockSpec((Q_BLOCK, LANES), lambda i, g, j: (i * nqb + j, 0)),
            cmp_spec, cmp_spec,
            rows(2), rows(3), rows(4), rows(5),
            pl.BlockSpec(ovt.shape, lambda i, g, j: (0, 0)),
            pl.BlockSpec(expand.shape, lambda i, g, j: (0, 0)),
        ],
        out_specs=pl.BlockSpec((Q_BLOCK, rep * HEAD_DIM), lambda i, g, j: (i * nqb + j, g)),
        out_shape=jax.ShapeDtypeStruct(q.shape, BF16),
        scratch_shapes=[
            pltpu.VMEM((rep * Q_BLOCK, LANES), F32),
            pltpu.VMEM((rep * Q_BLOCK, 2 * HEAD_DIM), F32),
            pltpu.VMEM((Q_BLOCK, t), F32),
        ],
        compiler_params=_params("parallel", "parallel", "arbitrary"),
        name="attn_prompt",
    )(q, gates, kc, vc, kvb, kvb, kvb, kvb, ovt, expand)


def _sample_cmp_kernel(pt_ref, *refs, pages_per_step, n_steps, t_pos, n_slc):
    del pt_ref
    page_refs = refs[:pages_per_step]
    (new_ref, w1_ref, pe_ref, w1f_ref, w2_ref, kn_ref, cos_ref, sin_ref, q_ref, ov_ref,
     oc_ref, idx_ref, fs_ref, stage_ref) = refs[pages_per_step:]
    step = pl.program_id(1)
    n_heads = N_KV * 2
    heads_per_row = 4 * N_KV
    page_rows = page_refs[0].shape[0] // heads_per_row
    sub_per_page = page_rows // CMP_STRIDE
    rows_per_step = pages_per_step * sub_per_page
    n_pad = fs_ref.shape[1]
    n_past = n_steps * rows_per_step
    row0 = pl.multiple_of(step * rows_per_step, rows_per_step)

    for c in range(n_heads):
        for k, p in enumerate(page_refs):
            stage_ref[c, k * page_rows:(k + 1) * page_rows, :] = p[pl.ds(c, page_rows, stride=heads_per_row), :]
        x = jnp.concatenate(
            [stage_ref[c, pl.ds(r, rows_per_step, stride=CMP_STRIDE), :].astype(BF16)
             for r in range(CMP_STRIDE)], axis=1)
        fs_ref[c, pl.ds(row0, rows_per_step), :] = jnp.dot(
            x, w1_ref[c // N_KV], preferred_element_type=F32)

    @pl.when(step == n_steps - 1)
    def _():
        tail = n_pad - n_past
        row_t = lax.broadcasted_iota(jnp.int32, (tail, HEAD_DIM), 0)
        rowi = lax.broadcasted_iota(jnp.int32, (n_pad, HEAD_DIM), 0)
        n_sub = (t_pos + 1 + KV_ALIGN - 1) // KV_ALIGN * KV_ALIGN // CMP_STRIDE
        outs = []
        for c in range(n_heads):
            kind = c // N_KV
            new_row = new_ref[c:c + 1, :]
            x_tail = jnp.where(row_t == 0, new_row, 0.0).astype(BF16)
            fs_ref[c, n_past:n_pad, :] = jnp.dot(x_tail, w1_ref[kind, 0:HEAD_DIM, :],
                                                 preferred_element_type=F32)
            out = _cmp_finish(fs_ref[c], _cmp_bias(pe_ref, w1f_ref, kind), w2_ref[kind], n_pad)
            if kind == 0:
                out = _rope(_rms(out, kn_ref[0:1, :]), cos_ref[...], sin_ref[...])
            outs.append(jnp.where(rowi < n_sub - 1, out, 0.0).astype(BF16))

        q = q_ref[...]
        n_q = q.shape[0]
        rep = n_q // N_KV
        head = lax.broadcasted_iota(jnp.int32, (n_q, 1), 0)
        cpos = lax.broadcasted_iota(jnp.int32, (1, n_pad), 1) * CMP_STRIDE + (CMP_BLOCK - 1)
        m_c = cpos <= t_pos
        lanes = ov_ref.shape[1]
        blk = lax.broadcasted_iota(jnp.int32, (1, lanes), 1)
        cur = t_pos // SLC_BLOCK
        causal = (blk * SLC_BLOCK <= t_pos) & (blk < n_slc)
        forced = (blk == 0) | (blk == cur) | (blk == cur - 1)
        eye_i = lax.broadcasted_iota(jnp.int32, (lanes, lanes), 0)
        eye_j = lax.broadcasted_iota(jnp.int32, (lanes, lanes), 1)
        o_c = jnp.zeros((n_q, HEAD_DIM), F32)
        for g in range(N_KV):
            kc, vc = outs[g], outs[N_KV + g]
            in_group = (head // rep) == g
            s = lax.dot_general(q, kc, _NT, preferred_element_type=F32)
            p = jnp.where(m_c, _softmax_rows(s, m_c), 0.0)
            o_g = jnp.dot(p.astype(BF16), vc, preferred_element_type=F32)
            o_c = jnp.where(in_group, o_g, o_c)
            psum = jnp.sum(jnp.where(in_group, p, 0.0), axis=0, keepdims=True)
            psum8 = jnp.broadcast_to(psum, (SUBLANES, n_pad))
            p_hi, p_lo = _split_bf16(psum8)
            imp = (jnp.dot(p_hi, ov_ref[...], preferred_element_type=F32)
                   + jnp.dot(p_lo, ov_ref[...], preferred_element_type=F32))[0:1, :]
            score = jnp.where(causal, imp + jnp.where(forced, FORCE_BONUS, 0.0), NEG)
            score_b = jnp.broadcast_to(score, (lanes, lanes))
            score_col = jnp.sum(jnp.where(eye_i == eye_j, score_b, 0.0), axis=1, keepdims=True)
            beats = (score_col > score_b) | ((score_col == score_b) & (eye_i < eye_j))
            rank = jnp.sum(jnp.where(beats, 1.0, 0.0), axis=0, keepdims=True)
            sel = jnp.where(causal & (rank < N_SELECT), 1.0, 0.0)
            sel_b = jnp.broadcast_to(sel, (lanes, lanes))
            sel_col = jnp.sum(jnp.where(eye_i == eye_j, sel_b, 0.0), axis=1, keepdims=True)
            slot = jnp.sum(jnp.where(eye_i < eye_j, sel_col, 0.0), axis=0, keepdims=True)
            slot_b = jnp.broadcast_to(slot, (N_SELECT, lanes))
            k_i = lax.broadcasted_iota(jnp.int32, (N_SELECT, lanes), 0).astype(F32)
            j_i = lax.broadcasted_iota(jnp.int32, (N_SELECT, lanes), 1).astype(F32)
            hit = (slot_b == k_i) & (jnp.broadcast_to(sel, (N_SELECT, lanes)) > 0.5)
            idx = jnp.sum(jnp.where(hit, j_i, 0.0), axis=1, keepdims=True)
            idx_ref[g] = jnp.broadcast_to(idx, (N_SELECT, LANES)).astype(jnp.int32)
        oc_ref[...] = o_c


def _sample_cmp(page_table, cache_pages, new_rows, w1cat, pe8, w1f, w2, kn, cos, sin, q3, ov,
                pages_per_step, t_pos, n_slc):
    bd, n_pages = page_table.shape
    n_steps = n_pages // pages_per_step
    flat_rows = cache_pages.shape[1]
    page_rows = flat_rows // (4 * N_KV)
    n_pad = cos.shape[0]
    n_q = q3.shape[1]
    full = lambda nd: (lambda i, s, pt: (0,) * nd)

    n_pool = cache_pages.shape[0]

    def page_spec(k):
        return pl.BlockSpec(
            (None, flat_rows, HEAD_DIM),
            lambda i, s, pt: (jnp.clip(pt[i, s * pages_per_step + k], 0, n_pool - 1), 0, 0))

    grid_spec = pltpu.PrefetchScalarGridSpec(
        num_scalar_prefetch=1,
        grid=(bd, n_steps),
        in_specs=[page_spec(k) for k in range(pages_per_step)] + [
            pl.BlockSpec((None,) + new_rows.shape[1:], lambda i, s, pt: (i, 0, 0)),
            pl.BlockSpec(w1cat.shape, full(3)),
            pl.BlockSpec(pe8.shape, full(3)),
            pl.BlockSpec(w1f.shape, full(3)),
            pl.BlockSpec(w2.shape, full(3)),
            pl.BlockSpec(kn.shape, full(2)),
            pl.BlockSpec(cos.shape, full(2)),
            pl.BlockSpec(sin.shape, full(2)),
            pl.BlockSpec((None, n_q, HEAD_DIM), lambda i, s, pt: (i, 0, 0)),
            pl.BlockSpec(ov.shape, full(2)),
        ],
        out_specs=[
            pl.BlockSpec((None, n_q, HEAD_DIM), lambda i, s, pt: (i, 0, 0)),
            pl.BlockSpec((None, N_KV, N_SELECT, LANES), lambda i, s, pt: (i, 0, 0, 0)),
        ],
        scratch_shapes=[pltpu.VMEM((N_KV * 2, n_pad, 2 * HEAD_DIM), F32),
                        pltpu.VMEM((N_KV * 2, pages_per_step * page_rows, HEAD_DIM), F32)],
    )
    kern = functools.partial(_sample_cmp_kernel, pages_per_step=pages_per_step, n_steps=n_steps,
                             t_pos=t_pos, n_slc=n_slc)
    return pl.pallas_call(
        kern,
        grid_spec=grid_spec,
        out_shape=[
            jax.ShapeDtypeStruct((bd, n_q, HEAD_DIM), F32),
            jax.ShapeDtypeStruct((bd, N_KV, N_SELECT, LANES), jnp.int32),
        ],
        compiler_params=_params("parallel", "arbitrary"),
        name="sample_cmp",
    )(page_table, *([cache_pages] * pages_per_step), new_rows, w1cat, pe8, w1f, w2, kn, cos, sin, q3, ov)


def _sample_attn_kernel(pt_ref, sel_ref, *refs, t_pos, n_past_blocks):
    del pt_ref
    n_blk = N_KV * N_SELECT
    blk_refs = refs[:n_blk]
    new_kv_ref, new_win_ref, state_ref, q_ref, gate_ref, oc_ref, o_ref = refs[n_blk:]
    n_paged = 4 * N_KV
    n_win = 2 * N_KV
    b = pl.program_id(0)
    q = q_ref[...]
    qf = q.astype(F32)
    n_q = q.shape[0]
    rep = n_q // N_KV
    head = lax.broadcasted_iota(jnp.int32, (n_q, 1), 0)
    n_keys = N_SELECT * SLC_BLOCK
    lane = lax.broadcasted_iota(jnp.int32, (1, n_keys), 1)
    row_b = lax.broadcasted_iota(jnp.int32, (SLC_BLOCK, HEAD_DIM), 0)
    w_keep = state_ref.shape[0] // n_win
    o_s = jnp.zeros((n_q, HEAD_DIM), F32)
    o_w = jnp.zeros((n_q, HEAD_DIM), F32)
    for g in range(N_KV):
        in_group = (head // rep) == g
        k_head = 2 * N_KV + g
        v_head = 3 * N_KV + g
        tail_k = jnp.where(row_b == 0, new_kv_ref[k_head:k_head + 1, :], 0.0)
        tail_v = jnp.where(row_b == 0, new_kv_ref[v_head:v_head + 1, :], 0.0)
        k_parts, v_parts = [], []
        base = jnp.zeros((1, n_keys), jnp.int32)
        for k in range(N_SELECT):
            blk = sel_ref[b, g * N_SELECT + k]
            is_tail = blk >= n_past_blocks
            blk_ref = blk_refs[g * N_SELECT + k]
            k_rows = blk_ref[pl.ds(k_head, SLC_BLOCK, stride=n_paged), :]
            v_rows = blk_ref[pl.ds(v_head, SLC_BLOCK, stride=n_paged), :]
            k_parts.append(jnp.where(is_tail, tail_k, k_rows).astype(BF16))
            v_parts.append(jnp.where(is_tail, tail_v, v_rows).astype(BF16))
            base = jnp.where(lane // SLC_BLOCK == k, blk * SLC_BLOCK, base)
        keys = jnp.concatenate(k_parts, axis=0)
        vals = jnp.concatenate(v_parts, axis=0)
        tok = base + lane % SLC_BLOCK
        s = lax.dot_general(q, keys, _NT, preferred_element_type=F32)
        p = _softmax_rows(s, tok <= t_pos)
        o_s = jnp.where(in_group, jnp.dot(p.astype(BF16), vals, preferred_element_type=F32), o_s)

        kw = state_ref[pl.ds(g, w_keep, stride=n_win), :].astype(BF16)
        vw = state_ref[pl.ds(N_KV + g, w_keep, stride=n_win), :].astype(BF16)
        kw_new = new_win_ref[g:g + 1, :]
        vw_new = new_win_ref[N_KV + g:N_KV + g + 1, :]
        dist = w_keep - lax.broadcasted_iota(jnp.int32, (1, w_keep), 1)
        m_w = (dist >= 0) & (dist < WINDOW) & (t_pos - dist >= 0)
        s_w = jnp.where(m_w, lax.dot_general(q, kw, _NT, preferred_element_type=F32), NEG)
        s_new = jnp.sum(qf * kw_new, axis=-1, keepdims=True)
        m = jnp.maximum(jnp.max(s_w, axis=-1, keepdims=True), s_new)
        e_w = jnp.exp(s_w - m)
        e_new = jnp.exp(s_new - m)
        denom = jnp.sum(e_w, axis=-1, keepdims=True) + e_new
        num = jnp.dot(e_w.astype(BF16), vw, preferred_element_type=F32) + e_new * vw_new
        o_w = jnp.where(in_group, num / denom, o_w)

    gates = jnp.broadcast_to(gate_ref[...], (n_q, LANES))
    glane = lax.broadcasted_iota(jnp.int32, (n_q, LANES), 1)
    gsel = [jnp.sum(jnp.where(glane == head * N_BRANCH + br, gates, 0.0), axis=-1, keepdims=True)
            for br in range(N_BRANCH)]
    o_ref[...] = (gsel[0] * oc_ref[...] + gsel[1] * o_s + gsel[2] * o_w).astype(BF16)


def _sample_attn(page_table, sel_idx, cache_blocks, new_kv, new_win, state3, q3, gates3, o_c,
                 t_pos, n_past_blocks, blocks_per_page):
    bd = page_table.shape[0]
    n_q = q3.shape[1]

    n_pool = cache_blocks.shape[0] // blocks_per_page

    def blk_spec(g, k):
        def index(i, pt, sel):
            blk = jnp.clip(sel[i, g * N_SELECT + k], 0, n_past_blocks - 1)
            page = jnp.clip(pt[i, blk // blocks_per_page], 0, n_pool - 1)
            return (page * blocks_per_page + blk % blocks_per_page, 0, 0)
        return pl.BlockSpec((None,) + cache_blocks.shape[1:], index)

    blk_specs = [blk_spec(g, k) for g in range(N_KV) for k in range(N_SELECT)]
    per_b = lambda shape: pl.BlockSpec((None,) + shape, lambda i, pt, sel: (i, 0, 0))
    grid_spec = pltpu.PrefetchScalarGridSpec(
        num_scalar_prefetch=2,
        grid=(bd,),
        in_specs=blk_specs + [
            per_b(new_kv.shape[1:]),
            per_b(new_win.shape[1:]),
            per_b(state3.shape[1:]),
            per_b((n_q, HEAD_DIM)),
            per_b((1, LANES)),
            per_b((n_q, HEAD_DIM)),
        ],
        out_specs=per_b((n_q, HEAD_DIM)),
    )
    kern = functools.partial(_sample_attn_kernel, t_pos=t_pos, n_past_blocks=n_past_blocks)
    n_blk = N_KV * N_SELECT
    return pl.pallas_call(
        kern,
        grid_spec=grid_spec,
        out_shape=jax.ShapeDtypeStruct((bd, n_q, HEAD_DIM), BF16),
        compiler_params=_params("parallel"),
        name="sample_attn",
    )(page_table, sel_idx, *([cache_blocks] * n_blk), new_kv, new_win, state3, q3, gates3, o_c)


def kernel(x_prompt, x_sample, cache_kv, state_kv_win, page_table, a_norm, a_w_in, a_v_norm, a_w_s,
           a_b_s, a_w_out, mlp_norm, mlp_w_up, mlp_w_down, kv_norm, w_kv, cmp_pe, cmp_w1, cmp_w2,
           k_norm, b_norm, b_w_in, b_q_norm, b_w_out):
    bp, t, d = x_prompt.shape
    bd, td, _ = x_sample.shape
    n_pool, page_size = cache_kv.shape[:2]
    n_pages = page_table.shape[1]
    past_len = n_pages * page_size
    w_keep = state_kv_win.shape[1]
    depth = mlp_norm.shape[0]
    n_a = a_norm.shape[0]
    n_b = b_norm.shape[0]
    d_q = b_w_out.shape[1]
    assert td == 1 and n_b == 1 and depth == n_a + n_b
    assert d // A_GROUPS == LANES and t % KV_TILE == 0 and t >= Q_BLOCK + WINDOW
    assert w_keep == WINDOW and page_size % SLC_BLOCK == 0 and past_len % KV_ALIGN == 0

    row = lambda v: v.reshape(1, -1)
    hp = x_prompt.reshape(bp * t, d)
    hs = x_sample.reshape(bd, d)
    tm = 512

    v_rows = []
    for l in range(n_a):
        w_in = a_w_in[l].astype(BF16)
        w_out = a_w_out[l].astype(BF16)
        w_up = mlp_w_up[l].astype(BF16)
        w_down = mlp_w_down[l].astype(BF16)
        zp = _gmlp_in(hp, row(a_norm[l]), w_in, 2 * tm, 512)
        hp = _gmlp_out(hp, zp, row(a_v_norm[l]), a_w_s[l], a_b_s[l].T, w_out, 256)
        hp = _mlp(hp, row(mlp_norm[l]), w_up, w_down, 2 * tm, 512)
        zs = _gmlp_in(hs, row(a_norm[l]), w_in, bd, 512)
        wd = jnp.repeat(a_w_s[l][:, 0, 0], d // A_GROUPS).reshape(1, d)
        bb = jnp.repeat(a_b_s[l][:, 0], d // A_GROUPS).reshape(1, d)
        hs, v_s = _gmlp_out_single(hs, zs, row(a_v_norm[l]), wd, bb, w_out)
        v_rows.append(v_s.reshape(bd, td, d))
        hs = _mlp(hs, row(mlp_norm[l]), w_up, w_down, bd, 512)

    w_kv_b = w_kv.astype(BF16)
    cos_p, sin_p = _rope_tables(jnp.arange(t))
    cos_pp, sin_pp = jnp.tile(cos_p, (bp, 1)), jnp.tile(sin_p, (bp, 1))
    cos_s, sin_s = _rope_tables(jnp.full((bd,), past_len))
    paged_p, win_p, cmp_rows_p, kvb_p = _kv_proj(hp, row(kv_norm), w_kv_b, k_norm, cos_pp, sin_pp, tm)
    paged_s, win_s, _, _ = _kv_proj(hs, row(kv_norm), w_kv_b, k_norm, cos_s, sin_s, bd)

    j = 0
    wq = b_w_in[j][:, :d_q].astype(BF16)
    wg = jnp.pad(b_w_in[j][:, d_q:], ((0, 0), (0, LANES - (b_w_in.shape[2] - d_q)))).astype(BF16)
    q_p, gates_p = _q_proj(hp, row(b_norm[j]), wq, wg, row(b_q_norm[j]), cos_pp, sin_pp, tm)
    q_s, gates_s = _q_proj(hs, row(b_norm[j]), wq, wg, row(b_q_norm[j]), cos_s, sin_s, bd)

    w1cat = jnp.concatenate([cmp_w1[:, :CMP_STRIDE], cmp_w1[:, CMP_STRIDE:]], axis=-1).astype(BF16)
    w1cat = w1cat.reshape(2, CMP_STRIDE * HEAD_DIM, 2 * HEAD_DIM)
    w1f = cmp_w1.reshape(2, CMP_BLOCK * HEAD_DIM, HEAD_DIM).astype(BF16)
    pe8 = jnp.broadcast_to(cmp_pe.reshape(2, 1, CMP_BLOCK * HEAD_DIM),
                           (2, SUBLANES, CMP_BLOCK * HEAD_DIM)).astype(BF16)
    w2 = cmp_w2.astype(BF16)

    n_sub_p = t // CMP_STRIDE
    n_slc_p = t // SLC_BLOCK
    cos_c, sin_c = _rope_tables(jnp.arange(n_sub_p) * CMP_STRIDE + CMP_BLOCK - 1)
    kc_p, vc_p = _compress_prompt(cmp_rows_p, w1cat, pe8, w1f, w2, k_norm, cos_c, sin_c, bp, t)
    ovt = _overlap_matrix(n_sub_p - 1, n_slc_p, n_sub_p, n_slc_p).T
    expand = (jnp.arange(LANES)[:, None] == (jnp.arange(t)[None, :] // SLC_BLOCK)).astype(BF16)
    o_p = _attn_prompt(q_p, gates_p, kc_p, vc_p, kvb_p, ovt, expand, bp, t)
    w_o = b_w_out[j].astype(BF16)
    hp = _out_proj(hp, o_p, w_o, tm)

    tp_s = -(-(past_len + td) // KV_ALIGN) * KV_ALIGN
    n_sub_s = tp_s // CMP_STRIDE
    n_slc_s = tp_s // SLC_BLOCK
    n_pad_s = -(-n_sub_s // SUBLANES) * SUBLANES
    sel_lanes = -(-n_slc_s // LANES) * LANES
    assert past_len // SLC_BLOCK + 1 >= N_SELECT
    cos_cs, sin_cs = _rope_tables(jnp.arange(n_pad_s) * CMP_STRIDE + CMP_BLOCK - 1)
    ov_s = _overlap_matrix(n_sub_s - 1, n_slc_s, n_pad_s, sel_lanes)
    n_paged = 4 * N_KV
    n_win = 2 * N_KV
    cache_pages = cache_kv.reshape(n_pool, page_size * n_paged, HEAD_DIM)
    q3 = q_s.reshape(bd, d_q // HEAD_DIM, HEAD_DIM)
    new_kv = paged_s.reshape(bd, n_paged, HEAD_DIM)
    new_win = jnp.pad(win_s.reshape(bd, n_win, HEAD_DIM), ((0, 0), (0, SUBLANES - n_win), (0, 0)))
    o_c, sel_idx = _sample_cmp(page_table, cache_pages, new_kv, w1cat, pe8, w1f, w2, k_norm, cos_cs,
                               sin_cs, q3, ov_s, min(16, n_pages), past_len, n_slc_s)
    blocks_per_page = page_size // SLC_BLOCK
    cache_blocks = cache_kv.reshape(n_pool * blocks_per_page, SLC_BLOCK * n_paged, HEAD_DIM)
    state3 = state_kv_win.reshape(bd, w_keep * n_win, HEAD_DIM)
    o_s = _sample_attn(page_table, sel_idx[..., 0].reshape(bd, N_KV * N_SELECT), cache_blocks, new_kv,
                       new_win, state3, q3, gates_s.reshape(bd, 1, LANES), o_c, past_len,
                       past_len // SLC_BLOCK, blocks_per_page)
    hs = _out_proj(hs, o_s.reshape(bd, d_q), w_o, bd)

    l = n_a
    w_up = mlp_w_up[l].astype(BF16)
    w_down = mlp_w_down[l].astype(BF16)
    hp = _mlp(hp, row(mlp_norm[l]), w_up, w_down, 2 * tm, 512)
    hs = _mlp(hs, row(mlp_norm[l]), w_up, w_down, bd, 512)

    y_p = hp.reshape(bp, t, d)
    y_s = hs.reshape(bd, td, d)
    kv_p = paged_p.reshape(bp, t, 4, N_KV, HEAD_DIM)
    win_all_p = win_p.reshape(bp, t, 2, N_KV, HEAD_DIM)
    win_new_p = win_all_p[:, t - min(WINDOW, t):]
    kv_s = paged_s.reshape(bd, td, 4, N_KV, HEAD_DIM)
    win_new_s = jnp.concatenate(
        [state_kv_win[:, td:], win_s.reshape(bd, td, 2, N_KV, HEAD_DIM)], axis=1)
    v_a_s = jnp.stack(v_rows, axis=0)
    return (y_p, y_s, kv_p, win_new_p, kv_s, win_new_s, v_a_s)
```

```python
import functools

import jax
import jax.numpy as jnp
from jax import lax
from jax.experimental import pallas as pl
from jax.experimental.pallas import tpu as pltpu

F32 = jnp.float32
BF16 = jnp.bfloat16

CHUNK = 128
A_GROUPS = 16
HEAD_DIM = 128
N_KV = 2
CMP_STRIDE = 16
CMP_BLOCK = 2 * CMP_STRIDE
SLC_BLOCK = 64
N_SELECT = 16
WINDOW = 512
N_BRANCH = 3
KV_ALIGN = 64
ROT_DIM = HEAD_DIM // 4
ROPE_THETA = 500000.0
Q_BLOCK = 128
EPS = 1e-6
NEG = -1e30
FORCE_BONUS = 1e4
LOG2_E = 1.4426950408889634

LANES = 128
SUBLANES = 8
VMEM_LIMIT_BYTES = 56 * 1024 * 1024

KV_TILE = 512
N_KINDS = 6

_NT = (((1,), (1,)), ((), ()))


def _params(*sem):
    return pltpu.CompilerParams(dimension_semantics=sem, vmem_limit_bytes=VMEM_LIMIT_BYTES)


def _rms(x, g):
    ms = jnp.mean(x * x, axis=-1, keepdims=True)
    return x * lax.rsqrt(ms + EPS) * g


def _rope(x, cos, sin):
    half = ROT_DIM // 2
    lane = lax.broadcasted_iota(jnp.int32, x.shape, 1)
    partner = jnp.where(lane < half, pltpu.roll(x, LANES - half, 1), pltpu.roll(x, half, 1))
    return x * cos + partner * sin


def _rope_tables(pos):
    inv = ROPE_THETA ** (-jnp.arange(0, ROT_DIM, 2, dtype=F32) / ROT_DIM)
    ang = pos.astype(F32)[:, None] * inv[None, :]
    c, s = jnp.cos(ang), jnp.sin(ang)
    n = pos.shape[0]
    pad = LANES - ROT_DIM
    cos = jnp.concatenate([c, c, jnp.ones((n, pad), F32)], axis=1)
    sin = jnp.concatenate([-s, s, jnp.zeros((n, pad), F32)], axis=1)
    return cos, sin


def _gmlp_in_kernel(x_ref, g_ref, w_ref, z_ref, xn_ref):
    @pl.when(pl.program_id(1) == 0)
    def _():
        xn_ref[...] = _rms(x_ref[...], g_ref[...]).astype(BF16)

    z_ref[...] = jax.nn.gelu(jnp.dot(xn_ref[...], w_ref[...], preferred_element_type=F32))


def _gmlp_in(x, g, w, tm, tn):
    m, d = x.shape
    n = w.shape[1]
    return pl.pallas_call(
        _gmlp_in_kernel,
        grid=(m // tm, n // tn),
        in_specs=[
            pl.BlockSpec((tm, d), lambda i, j: (i, 0)),
            pl.BlockSpec((1, d), lambda i, j: (0, 0)),
            pl.BlockSpec((d, tn), lambda i, j: (0, j)),
        ],
        out_specs=pl.BlockSpec((tm, tn), lambda i, j: (i, j)),
        out_shape=jax.ShapeDtypeStruct((m, n), F32),
        scratch_shapes=[pltpu.VMEM((tm, d), BF16)],
        compiler_params=_params("parallel", "arbitrary"),
        name="gmlp_in",
    )(x, g, w)


def _gmlp_out_kernel(x_ref, u_ref, v_ref, vg_ref, ws_ref, bt_ref, wo_ref, o_ref, vn_ref, y_ref):
    tm = x_ref.shape[0]
    vn_ref[...] = _rms(v_ref[...], vg_ref[...]).astype(BF16)
    row = lax.broadcasted_iota(jnp.int32, (CHUNK, CHUNK), 0)
    col = lax.broadcasted_iota(jnp.int32, (CHUNK, CHUNK), 1)
    causal = row >= col
    for g in range(A_GROUPS):
        wsg = jnp.where(causal, ws_ref[g], 0.0).astype(BF16)
        bias = bt_ref[:, g:g + 1]
        cs = slice(g * LANES, (g + 1) * LANES)
        for c in range(tm // CHUNK):
            rs = slice(c * CHUNK, (c + 1) * CHUNK)
            mixed = jnp.dot(wsg, vn_ref[rs, cs], preferred_element_type=F32) + bias
            y_ref[rs, cs] = (u_ref[rs, cs] * mixed).astype(BF16)
    o_ref[...] = x_ref[...] + jnp.dot(y_ref[...], wo_ref[...], preferred_element_type=F32)


def _gmlp_out(x, z, vg, ws, bt, wo, tm):
    m, d = x.shape
    return pl.pallas_call(
        _gmlp_out_kernel,
        grid=(m // tm,),
        in_specs=[
            pl.BlockSpec((tm, d), lambda i: (i, 0)),
            pl.BlockSpec((tm, d), lambda i: (i, 0)),
            pl.BlockSpec((tm, d), lambda i: (i, 1)),
            pl.BlockSpec((1, d), lambda i: (0, 0)),
            pl.BlockSpec((A_GROUPS, CHUNK, CHUNK), lambda i: (0, 0, 0)),
            pl.BlockSpec((CHUNK, A_GROUPS), lambda i: (0, 0)),
            pl.BlockSpec((d, d), lambda i: (0, 0)),
        ],
        out_specs=pl.BlockSpec((tm, d), lambda i: (i, 0)),
        out_shape=jax.ShapeDtypeStruct((m, d), F32),
        scratch_shapes=[pltpu.VMEM((tm, d), BF16), pltpu.VMEM((tm, d), BF16)],
        compiler_params=_params("parallel"),
        name="gmlp_out",
    )(x, z, z, vg, ws, bt, wo)


def _gmlp_out_single_kernel(x_ref, u_ref, v_ref, vg_ref, wd_ref, bb_ref, wo_ref, o_ref, vn_ref):
    vn = _rms(v_ref[...], vg_ref[...])
    vn_ref[...] = vn
    mixed = vn * wd_ref[...] + bb_ref[...]
    y = (u_ref[...] * mixed).astype(BF16)
    o_ref[...] = x_ref[...] + jnp.dot(y, wo_ref[...], preferred_element_type=F32)


def _gmlp_out_single(x, z, vg, wd, bb, wo):
    m, d = x.shape
    full = lambda i: (0, 0)
    return pl.pallas_call(
        _gmlp_out_single_kernel,
        grid=(1,),
        in_specs=[
            pl.BlockSpec((m, d), full),
            pl.BlockSpec((m, d), lambda i: (0, 0)),
            pl.BlockSpec((m, d), lambda i: (0, 1)),
            pl.BlockSpec((1, d), full),
            pl.BlockSpec((1, d), full),
            pl.BlockSpec((1, d), full),
            pl.BlockSpec((d, d), full),
        ],
        out_specs=[pl.BlockSpec((m, d), full), pl.BlockSpec((m, d), full)],
        out_shape=[jax.ShapeDtypeStruct((m, d), F32), jax.ShapeDtypeStruct((m, d), F32)],
        compiler_params=_params("arbitrary"),
        name="gmlp_out_single",
    )(x, z, z, vg, wd, bb, wo)


def _mlp_kernel(x_ref, g_ref, wu_ref, wd_ref, o_ref, xn_ref):
    @pl.when(pl.program_id(1) == 0)
    def _():
        x = x_ref[...]
        xn_ref[...] = _rms(x, g_ref[...]).astype(BF16)
        o_ref[...] = x

    h = jnp.dot(xn_ref[...], wu_ref[...], preferred_element_type=F32)
    a = jnp.square(jnp.maximum(h, 0.0)).astype(BF16)
    o_ref[...] += jnp.dot(a, wd_ref[...], preferred_element_type=F32)


def _mlp(x, g, wu, wd, layer, tm, tf):
    m, d = x.shape
    f = wu.shape[2]
    return pl.pallas_call(
        _mlp_kernel,
        grid=(m // tm, f // tf),
        in_specs=[
            pl.BlockSpec((tm, d), lambda i, j: (i, 0)),
            pl.BlockSpec((1, d), lambda i, j: (0, 0)),
            pl.BlockSpec((None, d, tf), lambda i, j: (layer, 0, j)),
            pl.BlockSpec((None, tf, d), lambda i, j: (layer, j, 0)),
        ],
        out_specs=pl.BlockSpec((tm, d), lambda i, j: (i, 0)),
        out_shape=jax.ShapeDtypeStruct((m, d), F32),
        scratch_shapes=[pltpu.VMEM((tm, d), BF16)],
        compiler_params=_params("parallel", "arbitrary"),
        name="mlp",
    )(x, g, wu, wd)


def _kv_proj_kernel(x_ref, g_ref, w_ref, kn_ref, cos_ref, sin_ref, paged_ref, win_ref, cmp_ref, kvb_ref):
    tm = x_ref.shape[0]
    n_paged = 4 * N_KV
    n_win = 2 * N_KV
    xn = _rms(x_ref[...], g_ref[...]).astype(BF16)
    kv = jnp.dot(xn, w_ref[...], preferred_element_type=F32)
    cos, sin = cos_ref[...], sin_ref[...]
    for c in range(N_KINDS * N_KV):
        kind = c // N_KV
        h = kv[:, c * LANES:(c + 1) * LANES]
        if kind == 2:
            h = _rope(_rms(h, kn_ref[1:2, :]), cos, sin)
        elif kind == 4:
            h = _rope(_rms(h, kn_ref[2:3, :]), cos, sin)
        if kind < 4:
            paged_ref[pl.ds(c, tm, stride=n_paged), :] = h
        else:
            win_ref[pl.ds(c - n_paged, tm, stride=n_win), :] = h
        if kind < 2:
            cmp_ref[c] = h
        else:
            kvb_ref[c - 2 * N_KV] = h.astype(BF16)


def _kv_proj(x, g, w, kn, cos, sin, tm):
    m, d = x.shape
    n = w.shape[1]
    n_paged = 4 * N_KV
    n_win = 2 * N_KV
    return pl.pallas_call(
        _kv_proj_kernel,
        grid=(m // tm,),
        in_specs=[
            pl.BlockSpec((tm, d), lambda i: (i, 0)),
            pl.BlockSpec((1, d), lambda i: (0, 0)),
            pl.BlockSpec((d, n), lambda i: (0, 0)),
            pl.BlockSpec((3, HEAD_DIM), lambda i: (0, 0)),
            pl.BlockSpec((tm, LANES), lambda i: (i, 0)),
            pl.BlockSpec((tm, LANES), lambda i: (i, 0)),
        ],
        out_specs=[
            pl.BlockSpec((tm * n_paged, HEAD_DIM), lambda i: (i, 0)),
            pl.BlockSpec((tm * n_win, HEAD_DIM), lambda i: (i, 0)),
            pl.BlockSpec((2 * N_KV, tm, HEAD_DIM), lambda i: (0, i, 0)),
            pl.BlockSpec((4 * N_KV, tm, HEAD_DIM), lambda i: (0, i, 0)),
        ],
        out_shape=[
            jax.ShapeDtypeStruct((m * n_paged, HEAD_DIM), F32),
            jax.ShapeDtypeStruct((m * n_win, HEAD_DIM), F32),
            jax.ShapeDtypeStruct((2 * N_KV, m, HEAD_DIM), F32),
            jax.ShapeDtypeStruct((4 * N_KV, m, HEAD_DIM), BF16),
        ],
        compiler_params=_params("parallel"),
        name="kv_proj",
    )(x, g, w, kn, cos, sin)


def _q_proj_kernel(x_ref, g_ref, wq_ref, wg_ref, qn_ref, cos_ref, sin_ref, q_ref, gate_ref):
    xn = _rms(x_ref[...], g_ref[...]).astype(BF16)
    z = jnp.dot(xn, wq_ref[...], preferred_element_type=F32)
    cos, sin = cos_ref[...], sin_ref[...]
    qn = qn_ref[...]
    scale = HEAD_DIM ** -0.5 * LOG2_E
    for h in range(z.shape[1] // HEAD_DIM):
        cs = slice(h * HEAD_DIM, (h + 1) * HEAD_DIM)
        qh = _rope(_rms(z[:, cs], qn), cos, sin) * scale
        q_ref[:, cs] = qh.astype(BF16)
    gate_ref[...] = jax.nn.sigmoid(jnp.dot(xn, wg_ref[...], preferred_element_type=F32))


def _q_proj(x, g, wq, wg, qn, cos, sin, tm, n):
    m, d = x.shape
    return pl.pallas_call(
        _q_proj_kernel,
        grid=(m // tm,),
        in_specs=[
            pl.BlockSpec((tm, d), lambda i: (i, 0)),
            pl.BlockSpec((1, d), lambda i: (0, 0)),
            pl.BlockSpec((d, n), lambda i: (0, 0)),
            pl.BlockSpec((d, LANES), lambda i: (0, 0)),
            pl.BlockSpec((1, HEAD_DIM), lambda i: (0, 0)),
            pl.BlockSpec((tm, LANES), lambda i: (i, 0)),
            pl.BlockSpec((tm, LANES), lambda i: (i, 0)),
        ],
        out_specs=[
            pl.BlockSpec((tm, n), lambda i: (i, 0)),
            pl.BlockSpec((tm, LANES), lambda i: (i, 0)),
        ],
        out_shape=[
            jax.ShapeDtypeStruct((m, n), BF16),
            jax.ShapeDtypeStruct((m, LANES), F32),
        ],
        compiler_params=_params("parallel"),
        name="q_proj",
    )(x, g, wq, wg, qn, cos, sin)


def _out_proj_kernel(h_ref, o_ref, w_ref, y_ref):
    y_ref[...] = h_ref[...] + jnp.dot(o_ref[...], w_ref[...], preferred_element_type=F32)


def _out_proj(h, o, w, tm):
    m, d = h.shape
    k = o.shape[1]
    return pl.pallas_call(
        _out_proj_kernel,
        grid=(m // tm,),
        in_specs=[
            pl.BlockSpec((tm, d), lambda i: (i, 0)),
            pl.BlockSpec((tm, k), lambda i: (i, 0)),
            pl.BlockSpec((k, d), lambda i: (0, 0)),
        ],
        out_specs=pl.BlockSpec((tm, d), lambda i: (i, 0)),
        out_shape=jax.ShapeDtypeStruct((m, d), F32),
        compiler_params=_params("parallel"),
        name="out_proj",
    )(h, o, w)


def _cmp_bias(pe_ref, w1f_ref, kind):
    return jnp.dot(pe_ref[kind], w1f_ref[kind], preferred_element_type=F32)[0:1, :]


def _cmp_finish(fs, bias, w2, n_rows):
    first = fs[:, :HEAD_DIM]
    second = pltpu.roll(fs[:, HEAD_DIM:], n_rows - 1, 0)
    h = first + second + bias
    h = h * jax.nn.sigmoid(h)
    return jnp.dot(h.astype(BF16), w2, preferred_element_type=F32)


def _compress_prompt_kernel(krows_ref, vrows_ref, w1_ref, pe_ref, w1f_ref, w2_ref, kn_ref,
                            cos_ref, sin_ref, kc_ref, vc_ref):
    n_sub = krows_ref.shape[0] // CMP_STRIDE
    rowi = lax.broadcasted_iota(jnp.int32, (n_sub, HEAD_DIM), 0)
    for kind, rows_ref, out_ref in ((0, krows_ref, kc_ref), (1, vrows_ref, vc_ref)):
        x = jnp.concatenate(
            [rows_ref[pl.ds(r, n_sub, stride=CMP_STRIDE), :].astype(BF16) for r in range(CMP_STRIDE)],
            axis=1)
        fs = jnp.dot(x, w1_ref[kind], preferred_element_type=F32)
        out = _cmp_finish(fs, _cmp_bias(pe_ref, w1f_ref, kind), w2_ref[kind], n_sub)
        if kind == 0:
            out = _rope(_rms(out, kn_ref[0:1, :]), cos_ref[...], sin_ref[...])
        out_ref[...] = jnp.where(rowi < n_sub - 1, out, 0.0).astype(BF16)


def _compress_prompt(paged, w1cat, pe8, w1f, w2, kn, cos, sin, b, t):
    n_sub = t // CMP_STRIDE
    full = lambda nd: (lambda i, g: (0,) * nd)
    out_spec = pl.BlockSpec((None, None, n_sub, HEAD_DIM), lambda i, g: (i, g, 0, 0))
    out_shape = jax.ShapeDtypeStruct((b, N_KV, n_sub, HEAD_DIM), BF16)
    return pl.pallas_call(
        _compress_prompt_kernel,
        grid=(b, N_KV),
        in_specs=[
            pl.BlockSpec((None, t, HEAD_DIM), lambda i, g: (g, i, 0)),
            pl.BlockSpec((None, t, HEAD_DIM), lambda i, g: (N_KV + g, i, 0)),
            pl.BlockSpec(w1cat.shape, full(3)),
            pl.BlockSpec(pe8.shape, full(3)),
            pl.BlockSpec(w1f.shape, full(3)),
            pl.BlockSpec(w2.shape, full(3)),
            pl.BlockSpec(kn.shape, full(2)),
            pl.BlockSpec(cos.shape, full(2)),
            pl.BlockSpec(sin.shape, full(2)),
        ],
        out_specs=[out_spec, out_spec],
        out_shape=[out_shape, out_shape],
        compiler_params=_params("parallel", "parallel"),
        name="compress_prompt",
    )(paged, paged, w1cat, pe8, w1f, w2, kn, cos, sin)


def _split_bf16(x):
    hi = x.astype(BF16)
    lo = (x - hi.astype(F32)).astype(BF16)
    return hi, lo


def _overlap_matrix(n_cmp, n_slc, rows, cols):
    ci = jnp.arange(rows)[:, None] * CMP_STRIDE
    sj = jnp.arange(cols)[None, :] * SLC_BLOCK
    ov = (ci < sj + SLC_BLOCK) & (ci + CMP_BLOCK > sj)
    ov = ov & (jnp.arange(rows)[:, None] < n_cmp) & (jnp.arange(cols)[None, :] < n_slc)
    return ov.astype(BF16)


def _softmax_rows(s, valid):
    s = jnp.where(valid, s, NEG)
    e = jnp.exp2(s - jnp.max(s, axis=-1, keepdims=True))
    return e / jnp.sum(e, axis=-1, keepdims=True)


def _attn_prompt_kernel(q_ref, gate_ref, kc_ref, vc_ref, ks_ref, vs_ref, kw_ref, vw_ref,
                        ovt_ref, nexp_ref, o_ref, m_ref, acc_ref, s_ref, kext_ref):
    qb = pl.program_id(2)
    g = pl.program_id(1)
    nq = Q_BLOCK
    rep = q_ref.shape[1] // HEAD_DIM
    n_cmp_pad = kc_ref.shape[0]
    n_slc = ovt_ref.shape[0]
    t_len = ks_ref.shape[0]
    s0 = qb * nq

    q_all = q_ref[...]
    q2 = jnp.concatenate([q_all[:, r * HEAD_DIM:(r + 1) * HEAD_DIM] for r in range(rep)], axis=0)
    t_col = s0 + lax.broadcasted_iota(jnp.int32, (nq, 1), 0)

    kc = kc_ref[...]
    vc = vc_ref[...]
    cpos = lax.broadcasted_iota(jnp.int32, (1, n_cmp_pad), 1) * CMP_STRIDE + (CMP_BLOCK - 1)
    m_c = cpos <= t_col
    s_c = lax.dot_general(q2, kc, _NT, preferred_element_type=F32)
    psum = jnp.zeros((nq, n_cmp_pad), F32)
    p_parts = []
    for r in range(rep):
        p = jnp.where(m_c, _softmax_rows(s_c[r * nq:(r + 1) * nq], m_c), 0.0)
        psum = psum + p
        p_parts.append(p.astype(BF16))
    o_c = jnp.dot(jnp.concatenate(p_parts, axis=0), vc, preferred_element_type=F32)

    ovt = ovt_ref[...]
    p_hi, p_lo = _split_bf16(psum)
    imp_t = (lax.dot_general(ovt, p_hi, _NT, preferred_element_type=F32)
             + lax.dot_general(ovt, p_lo, _NT, preferred_element_type=F32))
    t_row = s0 + lax.broadcasted_iota(jnp.int32, (n_slc, nq), 1)
    blk = lax.broadcasted_iota(jnp.int32, (n_slc, nq), 0)
    cur = t_row // SLC_BLOCK
    causal = blk * SLC_BLOCK <= t_row
    forced = (blk == 0) | (blk == cur) | (blk == cur - 1)
    score = jnp.where(causal, imp_t + jnp.where(forced, FORCE_BONUS, 0.0), NEG)
    rank = jnp.zeros((n_slc, nq), jnp.int32)
    for i in range(n_slc):
        si = score[i:i + 1, :]
        later = (blk > i).astype(jnp.int32)
        rank = rank + jnp.where(si > score, 1, 0) + jnp.where(si == score, later, 0)
    sel_t = jnp.where(causal & (rank < N_SELECT), 1.0, 0.0)
    sel_t = jnp.concatenate([sel_t, jnp.zeros((LANES - n_slc, nq), F32)], axis=0)
    unsel = 1.0 - sel_t.T

    @pl.when(qb == 0)
    def _():
        kext_ref[:, :HEAD_DIM] = ks_ref[...]
        kext_ref[:, HEAD_DIM:] = nexp_ref[...]

    blk_lane = lax.broadcasted_iota(jnp.int32, (nq, LANES), 1)
    unsel_main = jnp.where(blk_lane >= s0 // SLC_BLOCK, 1.0, unsel).astype(BF16)
    q_ext = jnp.concatenate([q2, jnp.concatenate([unsel_main] * rep, axis=0)], axis=1)
    n_main = (s0 + KV_TILE - 1) // KV_TILE
    n_chunks = KV_TILE // LANES
    m_ref[...] = jnp.full(m_ref.shape, NEG, F32)

    def max_body(j, carry):
        off = pl.multiple_of(j * KV_TILE, KV_TILE)
        s = lax.dot_general(q_ext, kext_ref[pl.ds(off, KV_TILE), :], _NT, preferred_element_type=F32)
        s_ref[:, pl.ds(off, KV_TILE)] = s
        m_run = m_ref[...]
        for c in range(n_chunks):
            m_run = jnp.maximum(m_run, s[:, c * LANES:(c + 1) * LANES])
        m_ref[...] = m_run
        return carry

    lax.fori_loop(0, n_main, max_body, 0)

    d0 = pl.multiple_of(s0, nq)
    q_i = lax.broadcasted_iota(jnp.int32, (nq, nq), 0)
    k_i = lax.broadcasted_iota(jnp.int32, (nq, nq), 1)
    bias_d = (lax.dot_general(unsel.astype(BF16), nexp_ref[pl.ds(d0, nq), :], _NT,
                              preferred_element_type=F32)
              + jnp.where(k_i <= q_i, 0.0, NEG))
    s_d = lax.dot_general(q2, ks_ref[pl.ds(d0, nq), :], _NT, preferred_element_type=F32)
    s_d = (s_d.reshape(rep, nq, nq) + bias_d[None]).reshape(rep * nq, nq)
    m_all = jnp.maximum(m_ref[...], s_d)
    m_b = jnp.broadcast_to(jnp.max(m_all, axis=-1, keepdims=True), m_all.shape)
    m_ref[...] = m_b
    v_d = jnp.concatenate([vs_ref[pl.ds(d0, nq), :], jnp.ones((nq, HEAD_DIM), BF16)], axis=1)
    acc_ref[...] = jnp.dot(jnp.exp2(s_d - m_b).astype(BF16), v_d, preferred_element_type=F32)
    ones = jnp.ones((KV_TILE, HEAD_DIM), BF16)

    def pv_body(j, carry):
        off = pl.multiple_of(j * KV_TILE, KV_TILE)
        v1 = jnp.concatenate([vs_ref[pl.ds(off, KV_TILE), :], ones], axis=1)
        m_rows = m_ref[...]
        p = jnp.concatenate(
            [jnp.exp2(s_ref[:, pl.ds(pl.multiple_of(off + c * LANES, LANES), LANES)] - m_rows).astype(BF16)
             for c in range(n_chunks)], axis=1)
        acc_ref[...] += jnp.dot(p, v1, preferred_element_type=F32)
        return carry

    lax.fori_loop(0, n_main, pv_body, 0)
    o_s = acc_ref[:, :HEAD_DIM] / acc_ref[:, HEAD_DIM:]

    slab = nq + WINDOW
    w0 = pl.multiple_of(jnp.maximum(s0 - WINDOW, 0), nq)
    kw = kw_ref[pl.ds(w0, slab), :]
    vw = jnp.concatenate([vw_ref[pl.ds(w0, slab), :], jnp.ones((slab, HEAD_DIM), BF16)], axis=1)
    dist = t_col - (w0 + lax.broadcasted_iota(jnp.int32, (nq, slab), 1))
    bias_w = jnp.where((dist >= 0) & (dist < WINDOW), 0.0, NEG)
    s_w = lax.dot_general(q2, kw, _NT, preferred_element_type=F32)
    p_parts = []
    for r in range(rep):
        s_r = s_w[r * nq:(r + 1) * nq] + bias_w
        p_parts.append(jnp.exp2(s_r - jnp.max(s_r, axis=-1, keepdims=True)).astype(BF16))
    o_w = jnp.dot(jnp.concatenate(p_parts, axis=0), vw, preferred_element_type=F32)
    o_w = o_w[:, :HEAD_DIM] / o_w[:, HEAD_DIM:]

    gates = gate_ref[...]
    for r in range(rep):
        rs = slice(r * nq, (r + 1) * nq)
        c0 = (g * rep + r) * N_BRANCH
        lane = lax.broadcasted_iota(jnp.int32, gates.shape, 1)
        gsel = [jnp.sum(jnp.where(lane == c0 + br, gates, 0.0), axis=-1, keepdims=True)
                for br in range(N_BRANCH)]
        out = gsel[0] * o_c[rs] + gsel[1] * o_s[rs] + gsel[2] * o_w[rs]
        o_ref[:, r * HEAD_DIM:(r + 1) * HEAD_DIM] = out.astype(BF16)


def _attn_prompt(q, gates, kc, vc, kvb, ovt, expand, b, t):
    nqb = t // Q_BLOCK
    rep = q.shape[1] // HEAD_DIM // N_KV
    n_cmp_pad = kc.shape[2]
    rows = lambda kind: pl.BlockSpec((None, t, HEAD_DIM), lambda i, g, j: ((kind - 2) * N_KV + g, i, 0))
    cmp_spec = pl.BlockSpec((None, None, n_cmp_pad, HEAD_DIM), lambda i, g, j: (i, g, 0, 0))
    return pl.pallas_call(
        _attn_prompt_kernel,
        grid=(b, N_KV, nqb),
        in_specs=[
            pl.BlockSpec((Q_BLOCK, rep * HEAD_DIM), lambda i, g, j: (i * nqb + j, g)),
            pl.BlockSpec((Q_BLOCK, LANES), lambda i, g, j: (i * nqb + j, 0)),
            cmp_spec, cmp_spec,
            rows(2), rows(3), rows(4), rows(5),
            pl.BlockSpec(ovt.shape, lambda i, g, j: (0, 0)),
            pl.BlockSpec(expand.shape, lambda i, g, j: (0, 0)),
        ],
        out_specs=pl.BlockSpec((Q_BLOCK, rep * HEAD_DIM), lambda i, g, j: (i * nqb + j, g)),
        out_shape=jax.ShapeDtypeStruct(q.shape, BF16),
        scratch_shapes=[
            pltpu.VMEM((rep * Q_BLOCK, LANES), F32),
            pltpu.VMEM((rep * Q_BLOCK, 2 * HEAD_DIM), F32),
            pltpu.VMEM((rep * Q_BLOCK, t), F32),
            pltpu.VMEM((t, 2 * HEAD_DIM), BF16),
        ],
        compiler_params=_params("parallel", "parallel", "arbitrary"),
        name="attn_prompt",
    )(q, gates, kc, vc, kvb, kvb, kvb, kvb, ovt, expand)


def _sample_cmp_kernel(pt_ref, *refs, pages_per_step, n_steps, t_pos, n_slc):
    del pt_ref
    page_refs = refs[:pages_per_step]
    (new_ref, w1_ref, pe_ref, w1f_ref, w2_ref, kn_ref, cos_ref, sin_ref, q_ref, ov_ref,
     oc_ref, idx_ref, fs_ref, stage_ref) = refs[pages_per_step:]
    step = pl.program_id(1)
    n_heads = N_KV * 2
    heads_per_row = 4 * N_KV
    page_rows = page_refs[0].shape[0] // heads_per_row
    sub_per_page = page_rows // CMP_STRIDE
    rows_per_step = pages_per_step * sub_per_page
    n_pad = fs_ref.shape[1]
    n_past = n_steps * rows_per_step
    row0 = pl.multiple_of(step * rows_per_step, rows_per_step)

    for c in range(n_heads):
        for k, p in enumerate(page_refs):
            stage_ref[c, k * page_rows:(k + 1) * page_rows, :] = p[pl.ds(c, page_rows, stride=heads_per_row), :]
        x = jnp.concatenate(
            [stage_ref[c, pl.ds(r, rows_per_step, stride=CMP_STRIDE), :].astype(BF16)
             for r in range(CMP_STRIDE)], axis=1)
        fs_ref[c, pl.ds(row0, rows_per_step), :] = jnp.dot(
            x, w1_ref[c // N_KV], preferred_element_type=F32)

    @pl.when(step == n_steps - 1)
    def _():
        tail = n_pad - n_past
        row_t = lax.broadcasted_iota(jnp.int32, (tail, HEAD_DIM), 0)
        rowi = lax.broadcasted_iota(jnp.int32, (n_pad, HEAD_DIM), 0)
        n_sub = (t_pos + 1 + KV_ALIGN - 1) // KV_ALIGN * KV_ALIGN // CMP_STRIDE
        outs = []
        for c in range(n_heads):
            kind = c // N_KV
            new_row = new_ref[c:c + 1, :]
            x_tail = jnp.where(row_t == 0, new_row, 0.0).astype(BF16)
            fs_ref[c, n_past:n_pad, :] = jnp.dot(x_tail, w1_ref[kind, 0:HEAD_DIM, :],
                                                 preferred_element_type=F32)
            out = _cmp_finish(fs_ref[c], _cmp_bias(pe_ref, w1f_ref, kind), w2_ref[kind], n_pad)
            if kind == 0:
                out = _rope(_rms(out, kn_ref[0:1, :]), cos_ref[...], sin_ref[...])
            outs.append(jnp.where(rowi < n_sub - 1, out, 0.0).astype(BF16))

        q = q_ref[...]
        n_q = q.shape[0]
        rep = n_q // N_KV
        head = lax.broadcasted_iota(jnp.int32, (n_q, 1), 0)
        cpos = lax.broadcasted_iota(jnp.int32, (1, n_pad), 1) * CMP_STRIDE + (CMP_BLOCK - 1)
        m_c = cpos <= t_pos
        lanes = ov_ref.shape[1]
        blk = lax.broadcasted_iota(jnp.int32, (1, lanes), 1)
        cur = t_pos // SLC_BLOCK
        causal = (blk * SLC_BLOCK <= t_pos) & (blk < n_slc)
        forced = (blk == 0) | (blk == cur) | (blk == cur - 1)
        eye_i = lax.broadcasted_iota(jnp.int32, (lanes, lanes), 0)
        eye_j = lax.broadcasted_iota(jnp.int32, (lanes, lanes), 1)
        o_c = jnp.zeros((n_q, HEAD_DIM), F32)
        for g in range(N_KV):
            kc, vc = outs[g], outs[N_KV + g]
            in_group = (head // rep) == g
            s = lax.dot_general(q, kc, _NT, preferred_element_type=F32)
            p = jnp.where(m_c, _softmax_rows(s, m_c), 0.0)
            o_g = jnp.dot(p.astype(BF16), vc, preferred_element_type=F32)
            o_c = jnp.where(in_group, o_g, o_c)
            psum = jnp.sum(jnp.where(in_group, p, 0.0), axis=0, keepdims=True)
            psum8 = jnp.broadcast_to(psum, (SUBLANES, n_pad))
            p_hi, p_lo = _split_bf16(psum8)
            imp = (jnp.dot(p_hi, ov_ref[...], preferred_element_type=F32)
                   + jnp.dot(p_lo, ov_ref[...], preferred_element_type=F32))[0:1, :]
            score = jnp.where(causal, imp + jnp.where(forced, FORCE_BONUS, 0.0), NEG)
            score_b = jnp.broadcast_to(score, (lanes, lanes))
            score_col = jnp.sum(jnp.where(eye_i == eye_j, score_b, 0.0), axis=1, keepdims=True)
            beats = (score_col > score_b) | ((score_col == score_b) & (eye_i < eye_j))
            rank = jnp.sum(jnp.where(beats, 1.0, 0.0), axis=0, keepdims=True)
            sel = jnp.where(causal & (rank < N_SELECT), 1.0, 0.0)
            sel_b = jnp.broadcast_to(sel, (lanes, lanes))
            sel_col = jnp.sum(jnp.where(eye_i == eye_j, sel_b, 0.0), axis=1, keepdims=True)
            slot = jnp.sum(jnp.where(eye_i < eye_j, sel_col, 0.0), axis=0, keepdims=True)
            slot_b = jnp.broadcast_to(slot, (N_SELECT, lanes))
            k_i = lax.broadcasted_iota(jnp.int32, (N_SELECT, lanes), 0).astype(F32)
            j_i = lax.broadcasted_iota(jnp.int32, (N_SELECT, lanes), 1).astype(F32)
            hit = (slot_b == k_i) & (jnp.broadcast_to(sel, (N_SELECT, lanes)) > 0.5)
            idx = jnp.sum(jnp.where(hit, j_i, 0.0), axis=1, keepdims=True)
            idx_ref[g] = jnp.broadcast_to(idx, (N_SELECT, LANES)).astype(jnp.int32)
        oc_ref[...] = o_c


def _sample_cmp(page_table, cache_pages, new_rows, w1cat, pe8, w1f, w2, kn, cos, sin, q3, ov,
                pages_per_step, t_pos, n_slc):
    bd, n_pages = page_table.shape
    n_steps = n_pages // pages_per_step
    flat_rows = cache_pages.shape[1]
    page_rows = flat_rows // (4 * N_KV)
    n_pad = cos.shape[0]
    n_q = q3.shape[1]
    full = lambda nd: (lambda i, s, pt: (0,) * nd)

    n_pool = cache_pages.shape[0]

    def page_spec(k):
        return pl.BlockSpec(
            (None, flat_rows, HEAD_DIM),
            lambda i, s, pt: (jnp.clip(pt[i, s * pages_per_step + k], 0, n_pool - 1), 0, 0))

    grid_spec = pltpu.PrefetchScalarGridSpec(
        num_scalar_prefetch=1,
        grid=(bd, n_steps),
        in_specs=[page_spec(k) for k in range(pages_per_step)] + [
            pl.BlockSpec((None,) + new_rows.shape[1:], lambda i, s, pt: (i, 0, 0)),
            pl.BlockSpec(w1cat.shape, full(3)),
            pl.BlockSpec(pe8.shape, full(3)),
            pl.BlockSpec(w1f.shape, full(3)),
            pl.BlockSpec(w2.shape, full(3)),
            pl.BlockSpec(kn.shape, full(2)),
            pl.BlockSpec(cos.shape, full(2)),
            pl.BlockSpec(sin.shape, full(2)),
            pl.BlockSpec((None, n_q, HEAD_DIM), lambda i, s, pt: (i, 0, 0)),
            pl.BlockSpec(ov.shape, full(2)),
        ],
        out_specs=[
            pl.BlockSpec((None, n_q, HEAD_DIM), lambda i, s, pt: (i, 0, 0)),
            pl.BlockSpec((None, N_KV, N_SELECT, LANES), lambda i, s, pt: (i, 0, 0, 0)),
        ],
        scratch_shapes=[pltpu.VMEM((N_KV * 2, n_pad, 2 * HEAD_DIM), F32),
                        pltpu.VMEM((N_KV * 2, pages_per_step * page_rows, HEAD_DIM), F32)],
    )
    kern = functools.partial(_sample_cmp_kernel, pages_per_step=pages_per_step, n_steps=n_steps,
                             t_pos=t_pos, n_slc=n_slc)
    return pl.pallas_call(
        kern,
        grid_spec=grid_spec,
        out_shape=[
            jax.ShapeDtypeStruct((bd, n_q, HEAD_DIM), F32),
            jax.ShapeDtypeStruct((bd, N_KV, N_SELECT, LANES), jnp.int32),
        ],
        compiler_params=_params("parallel", "arbitrary"),
        name="sample_cmp",
    )(page_table, *([cache_pages] * pages_per_step), new_rows, w1cat, pe8, w1f, w2, kn, cos, sin, q3, ov)


def _sample_attn_kernel(pt_ref, sel_ref, *refs, t_pos, n_past_blocks):
    del pt_ref
    n_blk = N_KV * N_SELECT
    blk_refs = refs[:n_blk]
    new_kv_ref, new_win_ref, state_ref, q_ref, gate_ref, oc_ref, o_ref = refs[n_blk:]
    n_paged = 4 * N_KV
    n_win = 2 * N_KV
    b = pl.program_id(0)
    q = q_ref[...]
    qf = q.astype(F32)
    n_q = q.shape[0]
    rep = n_q // N_KV
    head = lax.broadcasted_iota(jnp.int32, (n_q, 1), 0)
    n_keys = N_SELECT * SLC_BLOCK
    lane = lax.broadcasted_iota(jnp.int32, (1, n_keys), 1)
    row_b = lax.broadcasted_iota(jnp.int32, (SLC_BLOCK, HEAD_DIM), 0)
    w_keep = state_ref.shape[0] // n_win
    o_s = jnp.zeros((n_q, HEAD_DIM), F32)
    o_w = jnp.zeros((n_q, HEAD_DIM), F32)
    for g in range(N_KV):
        in_group = (head // rep) == g
        k_head = 2 * N_KV + g
        v_head = 3 * N_KV + g
        tail_k = jnp.where(row_b == 0, new_kv_ref[k_head:k_head + 1, :], 0.0)
        tail_v = jnp.where(row_b == 0, new_kv_ref[v_head:v_head + 1, :], 0.0)
        k_parts, v_parts = [], []
        base = jnp.zeros((1, n_keys), jnp.int32)
        for k in range(N_SELECT):
            blk = sel_ref[b, g * N_SELECT + k]
            is_tail = blk >= n_past_blocks
            blk_ref = blk_refs[g * N_SELECT + k]
            k_rows = blk_ref[pl.ds(k_head, SLC_BLOCK, stride=n_paged), :]
            v_rows = blk_ref[pl.ds(v_head, SLC_BLOCK, stride=n_paged), :]
            k_parts.append(jnp.where(is_tail, tail_k, k_rows).astype(BF16))
            v_parts.append(jnp.where(is_tail, tail_v, v_rows).astype(BF16))
            base = jnp.where(lane // SLC_BLOCK == k, blk * SLC_BLOCK, base)
        keys = jnp.concatenate(k_parts, axis=0)
        vals = jnp.concatenate(v_parts, axis=0)
        tok = base + lane % SLC_BLOCK
        s = lax.dot_general(q, keys, _NT, preferred_element_type=F32)
        p = _softmax_rows(s, tok <= t_pos)
        o_s = jnp.where(in_group, jnp.dot(p.astype(BF16), vals, preferred_element_type=F32), o_s)

        kw = state_ref[pl.ds(g, w_keep, stride=n_win), :].astype(BF16)
        vw = state_ref[pl.ds(N_KV + g, w_keep, stride=n_win), :].astype(BF16)
        kw_new = new_win_ref[g:g + 1, :]
        vw_new = new_win_ref[N_KV + g:N_KV + g + 1, :]
        dist = w_keep - lax.broadcasted_iota(jnp.int32, (1, w_keep), 1)
        m_w = (dist >= 0) & (dist < WINDOW) & (t_pos - dist >= 0)
        s_w = jnp.where(m_w, lax.dot_general(q, kw, _NT, preferred_element_type=F32), NEG)
        s_new = jnp.sum(qf * kw_new, axis=-1, keepdims=True)
        m = jnp.maximum(jnp.max(s_w, axis=-1, keepdims=True), s_new)
        e_w = jnp.exp2(s_w - m)
        e_new = jnp.exp2(s_new - m)
        denom = jnp.sum(e_w, axis=-1, keepdims=True) + e_new
        num = jnp.dot(e_w.astype(BF16), vw, preferred_element_type=F32) + e_new * vw_new
        o_w = jnp.where(in_group, num / denom, o_w)

    gates = jnp.broadcast_to(gate_ref[...], (n_q, LANES))
    glane = lax.broadcasted_iota(jnp.int32, (n_q, LANES), 1)
    gsel = [jnp.sum(jnp.where(glane == head * N_BRANCH + br, gates, 0.0), axis=-1, keepdims=True)
            for br in range(N_BRANCH)]
    o_ref[...] = (gsel[0] * oc_ref[...] + gsel[1] * o_s + gsel[2] * o_w).astype(BF16)


def _sample_attn(page_table, sel_idx, cache_blocks, new_kv, new_win, state3, q3, gates3, o_c,
                 t_pos, n_past_blocks, blocks_per_page):
    bd = page_table.shape[0]
    n_q = q3.shape[1]

    n_pool = cache_blocks.shape[0] // blocks_per_page

    def blk_spec(g, k):
        def index(i, pt, sel):
            blk = jnp.clip(sel[i, g * N_SELECT + k], 0, n_past_blocks - 1)
            page = jnp.clip(pt[i, blk // blocks_per_page], 0, n_pool - 1)
            return (page * blocks_per_page + blk % blocks_per_page, 0, 0)
        return pl.BlockSpec((None,) + cache_blocks.shape[1:], index)

    blk_specs = [blk_spec(g, k) for g in range(N_KV) for k in range(N_SELECT)]
    per_b = lambda shape: pl.BlockSpec((None,) + shape, lambda i, pt, sel: (i, 0, 0))
    grid_spec = pltpu.PrefetchScalarGridSpec(
        num_scalar_prefetch=2,
        grid=(bd,),
        in_specs=blk_specs + [
            per_b(new_kv.shape[1:]),
            per_b(new_win.shape[1:]),
            per_b(state3.shape[1:]),
            per_b((n_q, HEAD_DIM)),
            per_b((1, LANES)),
            per_b((n_q, HEAD_DIM)),
        ],
        out_specs=per_b((n_q, HEAD_DIM)),
    )
    kern = functools.partial(_sample_attn_kernel, t_pos=t_pos, n_past_blocks=n_past_blocks)
    n_blk = N_KV * N_SELECT
    return pl.pallas_call(
        kern,
        grid_spec=grid_spec,
        out_shape=jax.ShapeDtypeStruct((bd, n_q, HEAD_DIM), BF16),
        compiler_params=_params("parallel"),
        name="sample_attn",
    )(page_table, sel_idx, *([cache_blocks] * n_blk), new_kv, new_win, state3, q3, gates3, o_c)


def kernel(x_prompt, x_sample, cache_kv, state_kv_win, page_table, a_norm, a_w_in, a_v_norm, a_w_s,
           a_b_s, a_w_out, mlp_norm, mlp_w_up, mlp_w_down, kv_norm, w_kv, cmp_pe, cmp_w1, cmp_w2,
           k_norm, b_norm, b_w_in, b_q_norm, b_w_out):
    bp, t, d = x_prompt.shape
    bd, td, _ = x_sample.shape
    n_pool, page_size = cache_kv.shape[:2]
    n_pages = page_table.shape[1]
    past_len = n_pages * page_size
    w_keep = state_kv_win.shape[1]
    depth = mlp_norm.shape[0]
    n_a = a_norm.shape[0]
    n_b = b_norm.shape[0]
    d_q = b_w_out.shape[1]
    assert td == 1 and n_b == 1 and depth == n_a + n_b
    assert d // A_GROUPS == LANES and t % KV_TILE == 0 and t >= Q_BLOCK + WINDOW
    assert w_keep == WINDOW and page_size % SLC_BLOCK == 0 and past_len % KV_ALIGN == 0

    row = lambda v: v.reshape(1, -1)
    hp = x_prompt.reshape(bp * t, d)
    hs = x_sample.reshape(bd, d)
    tm = 512

    w_up = mlp_w_up.astype(BF16)
    w_down = mlp_w_down.astype(BF16)

    v_rows = []
    for l in range(n_a):
        w_in = a_w_in[l].astype(BF16)
        w_out = a_w_out[l].astype(BF16)
        zp = _gmlp_in(hp, row(a_norm[l]), w_in, 2 * tm, 512)
        hp = _gmlp_out(hp, zp, row(a_v_norm[l]), a_w_s[l], a_b_s[l].T, w_out, 256)
        hp = _mlp(hp, row(mlp_norm[l]), w_up, w_down, l, 2 * tm, 512)
        zs = _gmlp_in(hs, row(a_norm[l]), w_in, bd, 512)
        wd = jnp.repeat(a_w_s[l][:, 0, 0], d // A_GROUPS).reshape(1, d)
        bb = jnp.repeat(a_b_s[l][:, 0], d // A_GROUPS).reshape(1, d)
        hs, v_s = _gmlp_out_single(hs, zs, row(a_v_norm[l]), wd, bb, w_out)
        v_rows.append(v_s.reshape(bd, td, d))
        hs = _mlp(hs, row(mlp_norm[l]), w_up, w_down, l, bd, 512)

    w_kv_b = w_kv.astype(BF16)
    cos_p, sin_p = _rope_tables(jnp.arange(t))
    cos_pp, sin_pp = jnp.tile(cos_p, (bp, 1)), jnp.tile(sin_p, (bp, 1))
    cos_s, sin_s = _rope_tables(jnp.full((bd,), past_len))
    paged_p, win_p, cmp_rows_p, kvb_p = _kv_proj(hp, row(kv_norm), w_kv_b, k_norm, cos_pp, sin_pp, tm)
    paged_s, win_s, _, _ = _kv_proj(hs, row(kv_norm), w_kv_b, k_norm, cos_s, sin_s, bd)

    j = 0
    wq = b_w_in[j].astype(BF16)
    wg = jnp.pad(b_w_in[j][:, d_q:], ((0, 0), (0, LANES - (b_w_in.shape[2] - d_q)))).astype(BF16)
    q_p, gates_p = _q_proj(hp, row(b_norm[j]), wq, wg, row(b_q_norm[j]), cos_pp, sin_pp, tm, d_q)
    q_s, gates_s = _q_proj(hs, row(b_norm[j]), wq, wg, row(b_q_norm[j]), cos_s, sin_s, bd, d_q)

    w1cat = jnp.concatenate([cmp_w1[:, :CMP_STRIDE], cmp_w1[:, CMP_STRIDE:]], axis=-1).astype(BF16)
    w1cat = w1cat.reshape(2, CMP_STRIDE * HEAD_DIM, 2 * HEAD_DIM)
    w1f = cmp_w1.reshape(2, CMP_BLOCK * HEAD_DIM, HEAD_DIM).astype(BF16)
    pe8 = jnp.broadcast_to(cmp_pe.reshape(2, 1, CMP_BLOCK * HEAD_DIM),
                           (2, SUBLANES, CMP_BLOCK * HEAD_DIM)).astype(BF16)
    w2 = cmp_w2.astype(BF16)

    n_sub_p = t // CMP_STRIDE
    n_slc_p = t // SLC_BLOCK
    cos_c, sin_c = _rope_tables(jnp.arange(n_sub_p) * CMP_STRIDE + CMP_BLOCK - 1)
    kc_p, vc_p = _compress_prompt(cmp_rows_p, w1cat, pe8, w1f, w2, k_norm, cos_c, sin_c, bp, t)
    ovt = _overlap_matrix(n_sub_p - 1, n_slc_p, n_sub_p, n_slc_p).T
    expand = jnp.where(jnp.arange(t)[:, None] // SLC_BLOCK == jnp.arange(LANES)[None, :], NEG, 0.0).astype(BF16)
    o_p = _attn_prompt(q_p, gates_p, kc_p, vc_p, kvb_p, ovt, expand, bp, t)
    w_o = b_w_out[j].astype(BF16)
    hp = _out_proj(hp, o_p, w_o, tm)

    tp_s = -(-(past_len + td) // KV_ALIGN) * KV_ALIGN
    n_sub_s = tp_s // CMP_STRIDE
    n_slc_s = tp_s // SLC_BLOCK
    n_pad_s = -(-n_sub_s // SUBLANES) * SUBLANES
    sel_lanes = -(-n_slc_s // LANES) * LANES
    assert past_len // SLC_BLOCK + 1 >= N_SELECT
    cos_cs, sin_cs = _rope_tables(jnp.arange(n_pad_s) * CMP_STRIDE + CMP_BLOCK - 1)
    ov_s = _overlap_matrix(n_sub_s - 1, n_slc_s, n_pad_s, sel_lanes)
    n_paged = 4 * N_KV
    n_win = 2 * N_KV
    cache_pages = cache_kv.reshape(n_pool, page_size * n_paged, HEAD_DIM)
    q3 = q_s.reshape(bd, d_q // HEAD_DIM, HEAD_DIM)
    new_kv = paged_s.reshape(bd, n_paged, HEAD_DIM)
    new_win = jnp.pad(win_s.reshape(bd, n_win, HEAD_DIM), ((0, 0), (0, SUBLANES - n_win), (0, 0)))
    o_c, sel_idx = _sample_cmp(page_table, cache_pages, new_kv, w1cat, pe8, w1f, w2, k_norm, cos_cs,
                               sin_cs, q3, ov_s, min(16, n_pages), past_len, n_slc_s)
    blocks_per_page = page_size // SLC_BLOCK
    cache_blocks = cache_kv.reshape(n_pool * blocks_per_page, SLC_BLOCK * n_paged, HEAD_DIM)
    state3 = state_kv_win.reshape(bd, w_keep * n_win, HEAD_DIM)
    o_s = _sample_attn(page_table, sel_idx[..., 0].reshape(bd, N_KV * N_SELECT), cache_blocks, new_kv,
                       new_win, state3, q3, gates_s.reshape(bd, 1, LANES), o_c, past_len,
                       past_len // SLC_BLOCK, blocks_per_page)
    hs = _out_proj(hs, o_s.reshape(bd, d_q), w_o, bd)

    l = n_a
    hp = _mlp(hp, row(mlp_norm[l]), w_up, w_down, l, 2 * tm, 512)
    hs = _mlp(hs, row(mlp_norm[l]), w_up, w_down, l, bd, 512)

    y_p = hp.reshape(bp, t, d)
    y_s = hs.reshape(bd, td, d)
    kv_p = paged_p.reshape(bp, t, 4, N_KV, HEAD_DIM)
    win_all_p = win_p.reshape(bp, t, 2, N_KV, HEAD_DIM)
    win_new_p = win_all_p[:, t - min(WINDOW, t):]
    kv_s = paged_s.reshape(bd, td, 4, N_KV, HEAD_DIM)
    win_new_s = jnp.concatenate(
        [state_kv_win[:, td:], win_s.reshape(bd, td, 2, N_KV, HEAD_DIM)], axis=1)
    v_a_s = jnp.stack(v_rows, axis=0)
    return (y_p, y_s, kv_p, win_new_p, kv_s, win_new_s, v_a_s)
```

```python
import functools

import jax
import jax.numpy as jnp
from jax import lax
from jax.experimental import pallas as pl
from jax.experimental.pallas import tpu as pltpu

F32 = jnp.float32
BF16 = jnp.bfloat16

CHUNK = 128
A_GROUPS = 16
HEAD_DIM = 128
N_KV = 2
CMP_STRIDE = 16
CMP_BLOCK = 2 * CMP_STRIDE
SLC_BLOCK = 64
N_SELECT = 16
WINDOW = 512
N_BRANCH = 3
KV_ALIGN = 64
ROT_DIM = HEAD_DIM // 4
ROPE_THETA = 500000.0
Q_BLOCK = 128
EPS = 1e-6
NEG = -1e30
FORCE_BONUS = 1e4
LOG2_E = 1.4426950408889634

LANES = 128
SUBLANES = 8
VMEM_LIMIT_BYTES = 56 * 1024 * 1024

KV_TILE = 512
N_KINDS = 6

_NT = (((1,), (1,)), ((), ()))


def _params(*sem):
    return pltpu.CompilerParams(dimension_semantics=sem, vmem_limit_bytes=VMEM_LIMIT_BYTES)


def _rms(x, g):
    ms = jnp.mean(x * x, axis=-1, keepdims=True)
    return x * lax.rsqrt(ms + EPS) * g


def _rope(x, cos, sin):
    half = ROT_DIM // 2
    lane = lax.broadcasted_iota(jnp.int32, x.shape, 1)
    partner = jnp.where(lane < half, pltpu.roll(x, LANES - half, 1), pltpu.roll(x, half, 1))
    return x * cos + partner * sin


def _rope_tables(pos):
    inv = ROPE_THETA ** (-jnp.arange(0, ROT_DIM, 2, dtype=F32) / ROT_DIM)
    ang = pos.astype(F32)[:, None] * inv[None, :]
    c, s = jnp.cos(ang), jnp.sin(ang)
    n = pos.shape[0]
    pad = LANES - ROT_DIM
    cos = jnp.concatenate([c, c, jnp.ones((n, pad), F32)], axis=1)
    sin = jnp.concatenate([-s, s, jnp.zeros((n, pad), F32)], axis=1)
    return cos, sin


def _gmlp_in_kernel(x_ref, g_ref, w_ref, z_ref, xn_ref):
    @pl.when(pl.program_id(1) == 0)
    def _():
        xn_ref[...] = _rms(x_ref[...], g_ref[...]).astype(BF16)

    z_ref[...] = jax.nn.gelu(jnp.dot(xn_ref[...], w_ref[...], preferred_element_type=F32)).astype(z_ref.dtype)


def _gmlp_in(x, g, w, tm, tn):
    m, d = x.shape
    n = w.shape[1]
    return pl.pallas_call(
        _gmlp_in_kernel,
        grid=(m // tm, n // tn),
        in_specs=[
            pl.BlockSpec((tm, d), lambda i, j: (i, 0)),
            pl.BlockSpec((1, d), lambda i, j: (0, 0)),
            pl.BlockSpec((d, tn), lambda i, j: (0, j)),
        ],
        out_specs=pl.BlockSpec((tm, tn), lambda i, j: (i, j)),
        out_shape=jax.ShapeDtypeStruct((m, n), BF16),
        scratch_shapes=[pltpu.VMEM((tm, d), BF16)],
        compiler_params=_params("parallel", "arbitrary"),
        name="gmlp_in",
    )(x, g, w)


def _gmlp_out_kernel(x_ref, u_ref, v_ref, vg_ref, ws_ref, bt_ref, wo_ref, o_ref, vn_ref, y_ref):
    tm = x_ref.shape[0]
    vn_ref[...] = _rms(v_ref[...].astype(F32), vg_ref[...]).astype(BF16)
    row = lax.broadcasted_iota(jnp.int32, (CHUNK, CHUNK), 0)
    col = lax.broadcasted_iota(jnp.int32, (CHUNK, CHUNK), 1)
    causal = row >= col
    for g in range(A_GROUPS):
        wsg = jnp.where(causal, ws_ref[g], 0.0).astype(BF16)
        bias = bt_ref[:, g:g + 1]
        cs = slice(g * LANES, (g + 1) * LANES)
        for c in range(tm // CHUNK):
            rs = slice(c * CHUNK, (c + 1) * CHUNK)
            mixed = jnp.dot(wsg, vn_ref[rs, cs], preferred_element_type=F32) + bias
            y_ref[rs, cs] = (u_ref[rs, cs].astype(F32) * mixed).astype(BF16)
    o_ref[...] = x_ref[...] + jnp.dot(y_ref[...], wo_ref[...], preferred_element_type=F32)


def _gmlp_out(x, z, vg, ws, bt, wo, tm):
    m, d = x.shape
    return pl.pallas_call(
        _gmlp_out_kernel,
        grid=(m // tm,),
        in_specs=[
            pl.BlockSpec((tm, d), lambda i: (i, 0)),
            pl.BlockSpec((tm, d), lambda i: (i, 0)),
            pl.BlockSpec((tm, d), lambda i: (i, 1)),
            pl.BlockSpec((1, d), lambda i: (0, 0)),
            pl.BlockSpec((A_GROUPS, CHUNK, CHUNK), lambda i: (0, 0, 0)),
            pl.BlockSpec((CHUNK, A_GROUPS), lambda i: (0, 0)),
            pl.BlockSpec((d, d), lambda i: (0, 0)),
        ],
        out_specs=pl.BlockSpec((tm, d), lambda i: (i, 0)),
        out_shape=jax.ShapeDtypeStruct((m, d), F32),
        scratch_shapes=[pltpu.VMEM((tm, d), BF16), pltpu.VMEM((tm, d), BF16)],
        compiler_params=_params("parallel"),
        name="gmlp_out",
    )(x, z, z, vg, ws, bt, wo)


def _gmlp_out_single_kernel(x_ref, u_ref, v_ref, vg_ref, wd_ref, bb_ref, wo_ref, o_ref, vn_ref):
    vn = _rms(v_ref[...].astype(F32), vg_ref[...])
    vn_ref[...] = vn
    mixed = vn * wd_ref[...] + bb_ref[...]
    y = (u_ref[...].astype(F32) * mixed).astype(BF16)
    o_ref[...] = x_ref[...] + jnp.dot(y, wo_ref[...], preferred_element_type=F32)


def _gmlp_out_single(x, z, vg, wd, bb, wo):
    m, d = x.shape
    full = lambda i: (0, 0)
    return pl.pallas_call(
        _gmlp_out_single_kernel,
        grid=(1,),
        in_specs=[
            pl.BlockSpec((m, d), full),
            pl.BlockSpec((m, d), lambda i: (0, 0)),
            pl.BlockSpec((m, d), lambda i: (0, 1)),
            pl.BlockSpec((1, d), full),
            pl.BlockSpec((1, d), full),
            pl.BlockSpec((1, d), full),
            pl.BlockSpec((d, d), full),
        ],
        out_specs=[pl.BlockSpec((m, d), full), pl.BlockSpec((m, d), full)],
        out_shape=[jax.ShapeDtypeStruct((m, d), F32), jax.ShapeDtypeStruct((m, d), F32)],
        compiler_params=_params("arbitrary"),
        name="gmlp_out_single",
    )(x, z, z, vg, wd, bb, wo)


def _mlp_kernel(x_ref, g_ref, wu_ref, wd_ref, o_ref, xn_ref):
    @pl.when(pl.program_id(1) == 0)
    def _():
        x = x_ref[...]
        xn_ref[...] = _rms(x, g_ref[...]).astype(BF16)
        o_ref[...] = x

    h = jnp.dot(xn_ref[...], wu_ref[...], preferred_element_type=F32)
    a = jnp.square(jnp.maximum(h, 0.0)).astype(BF16)
    o_ref[...] += jnp.dot(a, wd_ref[...], preferred_element_type=F32)


def _mlp(x, g, wu, wd, layer, tm, tf):
    m, d = x.shape
    f = wu.shape[2]
    return pl.pallas_call(
        _mlp_kernel,
        grid=(m // tm, f // tf),
        in_specs=[
            pl.BlockSpec((tm, d), lambda i, j: (i, 0)),
            pl.BlockSpec((1, d), lambda i, j: (0, 0)),
            pl.BlockSpec((None, d, tf), lambda i, j: (layer, 0, j)),
            pl.BlockSpec((None, tf, d), lambda i, j: (layer, j, 0)),
        ],
        out_specs=pl.BlockSpec((tm, d), lambda i, j: (i, 0)),
        out_shape=jax.ShapeDtypeStruct((m, d), F32),
        scratch_shapes=[pltpu.VMEM((tm, d), BF16)],
        compiler_params=_params("parallel", "arbitrary"),
        name="mlp",
    )(x, g, wu, wd)


def _kv_proj_kernel(x_ref, g_ref, w_ref, kn_ref, cos_ref, sin_ref, paged_ref, win_ref, cmp_ref, kvb_ref):
    tm = x_ref.shape[0]
    n_paged = 4 * N_KV
    n_win = 2 * N_KV
    xn = _rms(x_ref[...], g_ref[...]).astype(BF16)
    kv = jnp.dot(xn, w_ref[...], preferred_element_type=F32)
    cos, sin = cos_ref[...], sin_ref[...]
    for c in range(N_KINDS * N_KV):
        kind = c // N_KV
        h = kv[:, c * LANES:(c + 1) * LANES]
        if kind == 2:
            h = _rope(_rms(h, kn_ref[1:2, :]), cos, sin)
        elif kind == 4:
            h = _rope(_rms(h, kn_ref[2:3, :]), cos, sin)
        if kind < 4:
            paged_ref[pl.ds(c, tm, stride=n_paged), :] = h
        else:
            win_ref[pl.ds(c - n_paged, tm, stride=n_win), :] = h
        if kind < 2:
            cmp_ref[c] = h
        else:
            kvb_ref[c - 2 * N_KV] = h.astype(BF16)


def _kv_proj(x, g, w, kn, cos, sin, tm):
    m, d = x.shape
    n = w.shape[1]
    n_paged = 4 * N_KV
    n_win = 2 * N_KV
    return pl.pallas_call(
        _kv_proj_kernel,
        grid=(m // tm,),
        in_specs=[
            pl.BlockSpec((tm, d), lambda i: (i, 0)),
            pl.BlockSpec((1, d), lambda i: (0, 0)),
            pl.BlockSpec((d, n), lambda i: (0, 0)),
            pl.BlockSpec((3, HEAD_DIM), lambda i: (0, 0)),
            pl.BlockSpec((tm, LANES), lambda i: (i, 0)),
            pl.BlockSpec((tm, LANES), lambda i: (i, 0)),
        ],
        out_specs=[
            pl.BlockSpec((tm * n_paged, HEAD_DIM), lambda i: (i, 0)),
            pl.BlockSpec((tm * n_win, HEAD_DIM), lambda i: (i, 0)),
            pl.BlockSpec((2 * N_KV, tm, HEAD_DIM), lambda i: (0, i, 0)),
            pl.BlockSpec((4 * N_KV, tm, HEAD_DIM), lambda i: (0, i, 0)),
        ],
        out_shape=[
            jax.ShapeDtypeStruct((m * n_paged, HEAD_DIM), F32),
            jax.ShapeDtypeStruct((m * n_win, HEAD_DIM), F32),
            jax.ShapeDtypeStruct((2 * N_KV, m, HEAD_DIM), F32),
            jax.ShapeDtypeStruct((4 * N_KV, m, HEAD_DIM), BF16),
        ],
        compiler_params=_params("parallel"),
        name="kv_proj",
    )(x, g, w, kn, cos, sin)


def _q_proj_kernel(x_ref, g_ref, wq_ref, wg_ref, qn_ref, cos_ref, sin_ref, q_ref, gate_ref):
    xn = _rms(x_ref[...], g_ref[...]).astype(BF16)
    z = jnp.dot(xn, wq_ref[...], preferred_element_type=F32)
    cos, sin = cos_ref[...], sin_ref[...]
    qn = qn_ref[...]
    scale = HEAD_DIM ** -0.5 * LOG2_E
    for h in range(z.shape[1] // HEAD_DIM):
        cs = slice(h * HEAD_DIM, (h + 1) * HEAD_DIM)
        qh = _rope(_rms(z[:, cs], qn), cos, sin) * scale
        q_ref[:, cs] = qh.astype(BF16)
    gate_ref[...] = jax.nn.sigmoid(jnp.dot(xn, wg_ref[...], preferred_element_type=F32))


def _q_proj(x, g, wq, wg, qn, cos, sin, tm, n):
    m, d = x.shape
    return pl.pallas_call(
        _q_proj_kernel,
        grid=(m // tm,),
        in_specs=[
            pl.BlockSpec((tm, d), lambda i: (i, 0)),
            pl.BlockSpec((1, d), lambda i: (0, 0)),
            pl.BlockSpec((d, n), lambda i: (0, 0)),
            pl.BlockSpec((d, LANES), lambda i: (0, 0)),
            pl.BlockSpec((1, HEAD_DIM), lambda i: (0, 0)),
            pl.BlockSpec((tm, LANES), lambda i: (i, 0)),
            pl.BlockSpec((tm, LANES), lambda i: (i, 0)),
        ],
        out_specs=[
            pl.BlockSpec((tm, n), lambda i: (i, 0)),
            pl.BlockSpec((tm, LANES), lambda i: (i, 0)),
        ],
        out_shape=[
            jax.ShapeDtypeStruct((m, n), BF16),
            jax.ShapeDtypeStruct((m, LANES), F32),
        ],
        compiler_params=_params("parallel"),
        name="q_proj",
    )(x, g, wq, wg, qn, cos, sin)


def _out_proj_kernel(h_ref, o_ref, w_ref, y_ref):
    y_ref[...] = h_ref[...] + jnp.dot(o_ref[...], w_ref[...], preferred_element_type=F32)


def _out_proj(h, o, w, tm):
    m, d = h.shape
    k = o.shape[1]
    return pl.pallas_call(
        _out_proj_kernel,
        grid=(m // tm,),
        in_specs=[
            pl.BlockSpec((tm, d), lambda i: (i, 0)),
            pl.BlockSpec((tm, k), lambda i: (i, 0)),
            pl.BlockSpec((k, d), lambda i: (0, 0)),
        ],
        out_specs=pl.BlockSpec((tm, d), lambda i: (i, 0)),
        out_shape=jax.ShapeDtypeStruct((m, d), F32),
        compiler_params=_params("parallel"),
        name="out_proj",
    )(h, o, w)


def _cmp_bias(pe_ref, w1f_ref, kind):
    return jnp.dot(pe_ref[kind], w1f_ref[kind], preferred_element_type=F32)[0:1, :]


def _cmp_finish(fs, bias, w2, n_rows):
    first = fs[:, :HEAD_DIM]
    second = pltpu.roll(fs[:, HEAD_DIM:], n_rows - 1, 0)
    h = first + second + bias
    h = h * jax.nn.sigmoid(h)
    return jnp.dot(h.astype(BF16), w2, preferred_element_type=F32)


def _compress_prompt_kernel(krows_ref, vrows_ref, w1_ref, pe_ref, w1f_ref, w2_ref, kn_ref,
                            cos_ref, sin_ref, kc_ref, vc_ref):
    n_sub = krows_ref.shape[0] // CMP_STRIDE
    rowi = lax.broadcasted_iota(jnp.int32, (n_sub, HEAD_DIM), 0)
    for kind, rows_ref, out_ref in ((0, krows_ref, kc_ref), (1, vrows_ref, vc_ref)):
        x = jnp.concatenate(
            [rows_ref[pl.ds(r, n_sub, stride=CMP_STRIDE), :].astype(BF16) for r in range(CMP_STRIDE)],
            axis=1)
        fs = jnp.dot(x, w1_ref[kind], preferred_element_type=F32)
        out = _cmp_finish(fs, _cmp_bias(pe_ref, w1f_ref, kind), w2_ref[kind], n_sub)
        if kind == 0:
            out = _rope(_rms(out, kn_ref[0:1, :]), cos_ref[...], sin_ref[...])
        out_ref[...] = jnp.where(rowi < n_sub - 1, out, 0.0).astype(BF16)


def _compress_prompt(paged, w1cat, pe8, w1f, w2, kn, cos, sin, b, t):
    n_sub = t // CMP_STRIDE
    full = lambda nd: (lambda i, g: (0,) * nd)
    out_spec = pl.BlockSpec((None, None, n_sub, HEAD_DIM), lambda i, g: (i, g, 0, 0))
    out_shape = jax.ShapeDtypeStruct((b, N_KV, n_sub, HEAD_DIM), BF16)
    return pl.pallas_call(
        _compress_prompt_kernel,
        grid=(b, N_KV),
        in_specs=[
            pl.BlockSpec((None, t, HEAD_DIM), lambda i, g: (g, i, 0)),
            pl.BlockSpec((None, t, HEAD_DIM), lambda i, g: (N_KV + g, i, 0)),
            pl.BlockSpec(w1cat.shape, full(3)),
            pl.BlockSpec(pe8.shape, full(3)),
            pl.BlockSpec(w1f.shape, full(3)),
            pl.BlockSpec(w2.shape, full(3)),
            pl.BlockSpec(kn.shape, full(2)),
            pl.BlockSpec(cos.shape, full(2)),
            pl.BlockSpec(sin.shape, full(2)),
        ],
        out_specs=[out_spec, out_spec],
        out_shape=[out_shape, out_shape],
        compiler_params=_params("parallel", "parallel"),
        name="compress_prompt",
    )(paged, paged, w1cat, pe8, w1f, w2, kn, cos, sin)


def _split_bf16(x):
    hi = x.astype(BF16)
    lo = (x - hi.astype(F32)).astype(BF16)
    return hi, lo


def _overlap_matrix(n_cmp, n_slc, rows, cols):
    ci = jnp.arange(rows)[:, None] * CMP_STRIDE
    sj = jnp.arange(cols)[None, :] * SLC_BLOCK
    ov = (ci < sj + SLC_BLOCK) & (ci + CMP_BLOCK > sj)
    ov = ov & (jnp.arange(rows)[:, None] < n_cmp) & (jnp.arange(cols)[None, :] < n_slc)
    return ov.astype(BF16)


def _softmax_rows(s, valid):
    s = jnp.where(valid, s, NEG)
    e = jnp.exp2(s - jnp.max(s, axis=-1, keepdims=True))
    return e / jnp.sum(e, axis=-1, keepdims=True)


def _attn_prompt_kernel(q_ref, gate_ref, kc_ref, vc_ref, ks_ref, vs_ref, kw_ref, vw_ref,
                        ovt_ref, nexp_ref, o_ref, m_ref, acc_ref, s_ref, kext_ref):
    qb = pl.program_id(2)
    g = pl.program_id(1)
    nq = Q_BLOCK
    rep = q_ref.shape[1] // HEAD_DIM
    n_cmp_pad = kc_ref.shape[0]
    n_slc = ovt_ref.shape[0]
    t_len = ks_ref.shape[0]
    s0 = qb * nq

    q_all = q_ref[...]
    q2 = jnp.concatenate([q_all[:, r * HEAD_DIM:(r + 1) * HEAD_DIM] for r in range(rep)], axis=0)
    t_col = s0 + lax.broadcasted_iota(jnp.int32, (nq, 1), 0)

    kc = kc_ref[...]
    vc = vc_ref[...]
    cpos = lax.broadcasted_iota(jnp.int32, (1, n_cmp_pad), 1) * CMP_STRIDE + (CMP_BLOCK - 1)
    m_c = cpos <= t_col
    s_c = lax.dot_general(q2, kc, _NT, preferred_element_type=F32)
    psum = jnp.zeros((nq, n_cmp_pad), F32)
    p_parts = []
    for r in range(rep):
        p = jnp.where(m_c, _softmax_rows(s_c[r * nq:(r + 1) * nq], m_c), 0.0)
        psum = psum + p
        p_parts.append(p.astype(BF16))
    o_c = jnp.dot(jnp.concatenate(p_parts, axis=0), vc, preferred_element_type=F32)

    ovt = ovt_ref[...]
    p_hi, p_lo = _split_bf16(psum)
    imp_t = (lax.dot_general(ovt, p_hi, _NT, preferred_element_type=F32)
             + lax.dot_general(ovt, p_lo, _NT, preferred_element_type=F32))
    t_row = s0 + lax.broadcasted_iota(jnp.int32, (n_slc, nq), 1)
    blk = lax.broadcasted_iota(jnp.int32, (n_slc, nq), 0)
    cur = t_row // SLC_BLOCK
    causal = blk * SLC_BLOCK <= t_row
    forced = (blk == 0) | (blk == cur) | (blk == cur - 1)
    score = jnp.where(causal, imp_t + jnp.where(forced, FORCE_BONUS, 0.0), NEG)
    rank = jnp.zeros((n_slc, nq), jnp.int32)
    for i in range(n_slc):
        si = score[i:i + 1, :]
        later = (blk > i).astype(jnp.int32)
        rank = rank + jnp.where(si > score, 1, 0) + jnp.where(si == score, later, 0)
    sel_t = jnp.where(causal & (rank < N_SELECT), 1.0, 0.0)
    sel_t = jnp.concatenate([sel_t, jnp.zeros((LANES - n_slc, nq), F32)], axis=0)
    unsel = 1.0 - sel_t.T

    @pl.when(qb == 0)
    def _():
        kext_ref[:, :HEAD_DIM] = ks_ref[...]
        kext_ref[:, HEAD_DIM:] = nexp_ref[...]

    blk_lane = lax.broadcasted_iota(jnp.int32, (nq, LANES), 1)
    unsel_main = jnp.where(blk_lane >= s0 // SLC_BLOCK, 1.0, unsel).astype(BF16)
    q_ext = jnp.concatenate([q2, jnp.concatenate([unsel_main] * rep, axis=0)], axis=1)
    n_main = (s0 + KV_TILE - 1) // KV_TILE
    n_chunks = KV_TILE // LANES
    m_ref[...] = jnp.full(m_ref.shape, NEG, F32)

    def max_body(j, carry):
        off = pl.multiple_of(j * KV_TILE, KV_TILE)
        s = lax.dot_general(q_ext, kext_ref[pl.ds(off, KV_TILE), :], _NT, preferred_element_type=F32)
        s_ref[:, pl.ds(off, KV_TILE)] = s
        m_run = m_ref[...]
        for c in range(n_chunks):
            m_run = jnp.maximum(m_run, s[:, c * LANES:(c + 1) * LANES])
        m_ref[...] = m_run
        return carry

    lax.fori_loop(0, n_main, max_body, 0)

    d0 = pl.multiple_of(s0, nq)
    q_i = lax.broadcasted_iota(jnp.int32, (nq, nq), 0)
    k_i = lax.broadcasted_iota(jnp.int32, (nq, nq), 1)
    bias_d = (lax.dot_general(unsel.astype(BF16), nexp_ref[pl.ds(d0, nq), :], _NT,
                              preferred_element_type=F32)
              + jnp.where(k_i <= q_i, 0.0, NEG))
    s_d = lax.dot_general(q2, ks_ref[pl.ds(d0, nq), :], _NT, preferred_element_type=F32)
    s_d = (s_d.reshape(rep, nq, nq) + bias_d[None]).reshape(rep * nq, nq)
    m_all = jnp.maximum(m_ref[...], s_d)
    m_b = jnp.broadcast_to(jnp.max(m_all, axis=-1, keepdims=True), m_all.shape)
    m_ref[...] = m_b
    v_d = jnp.concatenate([vs_ref[pl.ds(d0, nq), :], jnp.ones((nq, HEAD_DIM), BF16)], axis=1)
    acc_ref[...] = jnp.dot(jnp.exp2(s_d - m_b).astype(BF16), v_d, preferred_element_type=F32)
    ones = jnp.ones((KV_TILE, HEAD_DIM), BF16)

    def pv_body(j, carry):
        off = pl.multiple_of(j * KV_TILE, KV_TILE)
        v1 = jnp.concatenate([vs_ref[pl.ds(off, KV_TILE), :], ones], axis=1)
        m_rows = m_ref[...]
        p = jnp.concatenate(
            [jnp.exp2(s_ref[:, pl.ds(pl.multiple_of(off + c * LANES, LANES), LANES)] - m_rows).astype(BF16)
             for c in range(n_chunks)], axis=1)
        acc_ref[...] += jnp.dot(p, v1, preferred_element_type=F32)
        return carry

    lax.fori_loop(0, n_main, pv_body, 0)
    o_s = acc_ref[:, :HEAD_DIM] / acc_ref[:, HEAD_DIM:]

    slab = nq + WINDOW
    w0 = pl.multiple_of(jnp.maximum(s0 - WINDOW, 0), nq)
    kw = kw_ref[pl.ds(w0, slab), :]
    vw = jnp.concatenate([vw_ref[pl.ds(w0, slab), :], jnp.ones((slab, HEAD_DIM), BF16)], axis=1)
    dist = t_col - (w0 + lax.broadcasted_iota(jnp.int32, (nq, slab), 1))
    bias_w = jnp.where((dist >= 0) & (dist < WINDOW), 0.0, NEG)
    s_w = lax.dot_general(q2, kw, _NT, preferred_element_type=F32)
    p_parts = []
    for r in range(rep):
        s_r = s_w[r * nq:(r + 1) * nq] + bias_w
        p_parts.append(jnp.exp2(s_r - jnp.max(s_r, axis=-1, keepdims=True)).astype(BF16))
    o_w = jnp.dot(jnp.concatenate(p_parts, axis=0), vw, preferred_element_type=F32)
    o_w = o_w[:, :HEAD_DIM] / o_w[:, HEAD_DIM:]

    gates = gate_ref[...]
    for r in range(rep):
        rs = slice(r * nq, (r + 1) * nq)
        c0 = (g * rep + r) * N_BRANCH
        lane = lax.broadcasted_iota(jnp.int32, gates.shape, 1)
        gsel = [jnp.sum(jnp.where(lane == c0 + br, gates, 0.0), axis=-1, keepdims=True)
                for br in range(N_BRANCH)]
        out = gsel[0] * o_c[rs] + gsel[1] * o_s[rs] + gsel[2] * o_w[rs]
        o_ref[:, r * HEAD_DIM:(r + 1) * HEAD_DIM] = out.astype(BF16)


def _attn_prompt(q, gates, kc, vc, kvb, ovt, expand, b, t):
    nqb = t // Q_BLOCK
    rep = q.shape[1] // HEAD_DIM // N_KV
    n_cmp_pad = kc.shape[2]
    rows = lambda kind: pl.BlockSpec((None, t, HEAD_DIM), lambda i, g, j: ((kind - 2) * N_KV + g, i, 0))
    cmp_spec = pl.BlockSpec((None, None, n_cmp_pad, HEAD_DIM), lambda i, g, j: (i, g, 0, 0))
    return pl.pallas_call(
        _attn_prompt_kernel,
        grid=(b, N_KV, nqb),
        in_specs=[
            pl.BlockSpec((Q_BLOCK, rep * HEAD_DIM), lambda i, g, j: (i * nqb + j, g)),
            pl.BlockSpec((Q_BLOCK, LANES), lambda i, g, j: (i * nqb + j, 0)),
            cmp_spec, cmp_spec,
            rows(2), rows(3), rows(4), rows(5),
            pl.BlockSpec(ovt.shape, lambda i, g, j: (0, 0)),
            pl.BlockSpec(expand.shape, lambda i, g, j: (0, 0)),
        ],
        out_specs=pl.BlockSpec((Q_BLOCK, rep * HEAD_DIM), lambda i, g, j: (i * nqb + j, g)),
        out_shape=jax.ShapeDtypeStruct(q.shape, BF16),
        scratch_shapes=[
            pltpu.VMEM((rep * Q_BLOCK, LANES), F32),
            pltpu.VMEM((rep * Q_BLOCK, 2 * HEAD_DIM), F32),
            pltpu.VMEM((rep * Q_BLOCK, t), F32),
            pltpu.VMEM((t, 2 * HEAD_DIM), BF16),
        ],
        compiler_params=_params("parallel", "parallel", "arbitrary"),
        name="attn_prompt",
    )(q, gates, kc, vc, kvb, kvb, kvb, kvb, ovt, expand)


def _sample_cmp_kernel(pt_ref, *refs, pages_per_step, n_steps, t_pos, n_slc):
    del pt_ref
    page_refs = refs[:pages_per_step]
    (new_ref, w1_ref, pe_ref, w1f_ref, w2_ref, kn_ref, cos_ref, sin_ref, q_ref, ov_ref,
     oc_ref, idx_ref, fs_ref, stage_ref) = refs[pages_per_step:]
    step = pl.program_id(1)
    n_heads = N_KV * 2
    heads_per_row = 4 * N_KV
    page_rows = page_refs[0].shape[0] // heads_per_row
    sub_per_page = page_rows // CMP_STRIDE
    rows_per_step = pages_per_step * sub_per_page
    n_pad = fs_ref.shape[1]
    n_past = n_steps * rows_per_step
    row0 = pl.multiple_of(step * rows_per_step, rows_per_step)

    assert heads_per_row == SUBLANES
    slab = stage_ref.shape[0] // heads_per_row
    for k, p in enumerate(page_refs):
        for n in range(sub_per_page):
            row = k * sub_per_page + n
            for r in range(CMP_STRIDE):
                tok = p[pl.ds((n * CMP_STRIDE + r) * heads_per_row, heads_per_row), :]
                tile = (row // SUBLANES) * CMP_STRIDE + r
                stage_ref[pl.ds(tile * SUBLANES + row % SUBLANES, heads_per_row, stride=slab), :] = tok
    for c in range(n_heads):
        x = jnp.concatenate(
            [jnp.concatenate(
                [stage_ref[pl.ds(c * slab + (i * CMP_STRIDE + r) * SUBLANES, SUBLANES), :]
                 for r in range(CMP_STRIDE)], axis=1)
             for i in range(rows_per_step // SUBLANES)], axis=0)
        fs_ref[c, pl.ds(row0, rows_per_step), :] = jnp.dot(
            x.astype(BF16), w1_ref[c // N_KV], preferred_element_type=F32)

    @pl.when(step == n_steps - 1)
    def _():
        tail = n_pad - n_past
        row_t = lax.broadcasted_iota(jnp.int32, (tail, HEAD_DIM), 0)
        rowi = lax.broadcasted_iota(jnp.int32, (n_pad, HEAD_DIM), 0)
        n_sub = (t_pos + 1 + KV_ALIGN - 1) // KV_ALIGN * KV_ALIGN // CMP_STRIDE
        outs = []
        for c in range(n_heads):
            kind = c // N_KV
            new_row = new_ref[c:c + 1, :]
            x_tail = jnp.where(row_t == 0, new_row, 0.0).astype(BF16)
            fs_ref[c, n_past:n_pad, :] = jnp.dot(x_tail, w1_ref[kind, 0:HEAD_DIM, :],
                                                 preferred_element_type=F32)
            out = _cmp_finish(fs_ref[c], _cmp_bias(pe_ref, w1f_ref, kind), w2_ref[kind], n_pad)
            if kind == 0:
                out = _rope(_rms(out, kn_ref[0:1, :]), cos_ref[...], sin_ref[...])
            outs.append(jnp.where(rowi < n_sub - 1, out, 0.0).astype(BF16))

        q = q_ref[...]
        n_q = q.shape[0]
        rep = n_q // N_KV
        head = lax.broadcasted_iota(jnp.int32, (n_q, 1), 0)
        cpos = lax.broadcasted_iota(jnp.int32, (1, n_pad), 1) * CMP_STRIDE + (CMP_BLOCK - 1)
        m_c = cpos <= t_pos
        lanes = ov_ref.shape[1]
        blk = lax.broadcasted_iota(jnp.int32, (1, lanes), 1)
        cur = t_pos // SLC_BLOCK
        causal = (blk * SLC_BLOCK <= t_pos) & (blk < n_slc)
        forced = (blk == 0) | (blk == cur) | (blk == cur - 1)
        eye_i = lax.broadcasted_iota(jnp.int32, (lanes, lanes), 0)
        eye_j = lax.broadcasted_iota(jnp.int32, (lanes, lanes), 1)
        o_c = jnp.zeros((n_q, HEAD_DIM), F32)
        for g in range(N_KV):
            kc, vc = outs[g], outs[N_KV + g]
            in_group = (head // rep) == g
            s = lax.dot_general(q, kc, _NT, preferred_element_type=F32)
            p = jnp.where(m_c, _softmax_rows(s, m_c), 0.0)
            o_g = jnp.dot(p.astype(BF16), vc, preferred_element_type=F32)
            o_c = jnp.where(in_group, o_g, o_c)
            psum = jnp.sum(jnp.where(in_group, p, 0.0), axis=0, keepdims=True)
            psum8 = jnp.broadcast_to(psum, (SUBLANES, n_pad))
            p_hi, p_lo = _split_bf16(psum8)
            imp = (jnp.dot(p_hi, ov_ref[...], preferred_element_type=F32)
                   + jnp.dot(p_lo, ov_ref[...], preferred_element_type=F32))[0:1, :]
            score = jnp.where(causal, imp + jnp.where(forced, FORCE_BONUS, 0.0), NEG)
            score_b = jnp.broadcast_to(score, (lanes, lanes))
            score_col = jnp.sum(jnp.where(eye_i == eye_j, score_b, 0.0), axis=1, keepdims=True)
            beats = (score_col > score_b) | ((score_col == score_b) & (eye_i < eye_j))
            rank = jnp.sum(jnp.where(beats, 1.0, 0.0), axis=0, keepdims=True)
            sel = jnp.where(causal & (rank < N_SELECT), 1.0, 0.0)
            sel_b = jnp.broadcast_to(sel, (lanes, lanes))
            sel_col = jnp.sum(jnp.where(eye_i == eye_j, sel_b, 0.0), axis=1, keepdims=True)
            slot = jnp.sum(jnp.where(eye_i < eye_j, sel_col, 0.0), axis=0, keepdims=True)
            slot_b = jnp.broadcast_to(slot, (N_SELECT, lanes))
            k_i = lax.broadcasted_iota(jnp.int32, (N_SELECT, lanes), 0).astype(F32)
            j_i = lax.broadcasted_iota(jnp.int32, (N_SELECT, lanes), 1).astype(F32)
            hit = (slot_b == k_i) & (jnp.broadcast_to(sel, (N_SELECT, lanes)) > 0.5)
            idx = jnp.sum(jnp.where(hit, j_i, 0.0), axis=1, keepdims=True)
            idx_ref[g] = jnp.broadcast_to(idx, (N_SELECT, LANES)).astype(jnp.int32)
        oc_ref[...] = o_c


def _sample_cmp(page_table, cache_pages, new_rows, w1cat, pe8, w1f, w2, kn, cos, sin, q3, ov,
                pages_per_step, t_pos, n_slc):
    bd, n_pages = page_table.shape
    n_steps = n_pages // pages_per_step
    flat_rows = cache_pages.shape[1]
    page_rows = flat_rows // (4 * N_KV)
    n_pad = cos.shape[0]
    n_q = q3.shape[1]
    full = lambda nd: (lambda i, s, pt: (0,) * nd)

    n_pool = cache_pages.shape[0]

    def page_spec(k):
        return pl.BlockSpec(
            (None, flat_rows, HEAD_DIM),
            lambda i, s, pt: (jnp.clip(pt[i, s * pages_per_step + k], 0, n_pool - 1), 0, 0))

    grid_spec = pltpu.PrefetchScalarGridSpec(
        num_scalar_prefetch=1,
        grid=(bd, n_steps),
        in_specs=[page_spec(k) for k in range(pages_per_step)] + [
            pl.BlockSpec((None,) + new_rows.shape[1:], lambda i, s, pt: (i, 0, 0)),
            pl.BlockSpec(w1cat.shape, full(3)),
            pl.BlockSpec(pe8.shape, full(3)),
            pl.BlockSpec(w1f.shape, full(3)),
            pl.BlockSpec(w2.shape, full(3)),
            pl.BlockSpec(kn.shape, full(2)),
            pl.BlockSpec(cos.shape, full(2)),
            pl.BlockSpec(sin.shape, full(2)),
            pl.BlockSpec((None, n_q, HEAD_DIM), lambda i, s, pt: (i, 0, 0)),
            pl.BlockSpec(ov.shape, full(2)),
        ],
        out_specs=[
            pl.BlockSpec((None, n_q, HEAD_DIM), lambda i, s, pt: (i, 0, 0)),
            pl.BlockSpec((None, N_KV, N_SELECT, LANES), lambda i, s, pt: (i, 0, 0, 0)),
        ],
        scratch_shapes=[pltpu.VMEM((N_KV * 2, n_pad, 2 * HEAD_DIM), F32),
                        pltpu.VMEM((4 * N_KV * (pages_per_step * page_rows + 4), HEAD_DIM), F32)],
    )
    kern = functools.partial(_sample_cmp_kernel, pages_per_step=pages_per_step, n_steps=n_steps,
                             t_pos=t_pos, n_slc=n_slc)
    return pl.pallas_call(
        kern,
        grid_spec=grid_spec,
        out_shape=[
            jax.ShapeDtypeStruct((bd, n_q, HEAD_DIM), F32),
            jax.ShapeDtypeStruct((bd, N_KV, N_SELECT, LANES), jnp.int32),
        ],
        compiler_params=_params("parallel", "arbitrary"),
        name="sample_cmp",
    )(page_table, *([cache_pages] * pages_per_step), new_rows, w1cat, pe8, w1f, w2, kn, cos, sin, q3, ov)


def _sample_attn_kernel(pt_ref, sel_ref, *refs, t_pos, n_past_blocks):
    del pt_ref
    n_blk = N_KV * N_SELECT
    blk_refs = refs[:n_blk]
    new_kv_ref, new_win_ref, state_ref, q_ref, gate_ref, oc_ref, o_ref = refs[n_blk:]
    n_paged = 4 * N_KV
    n_win = 2 * N_KV
    b = pl.program_id(0)
    q = q_ref[...]
    qf = q.astype(F32)
    n_q = q.shape[0]
    rep = n_q // N_KV
    head = lax.broadcasted_iota(jnp.int32, (n_q, 1), 0)
    n_keys = N_SELECT * SLC_BLOCK
    lane = lax.broadcasted_iota(jnp.int32, (1, n_keys), 1)
    row_b = lax.broadcasted_iota(jnp.int32, (SLC_BLOCK, HEAD_DIM), 0)
    w_keep = state_ref.shape[0] // n_win
    o_s = jnp.zeros((n_q, HEAD_DIM), F32)
    o_w = jnp.zeros((n_q, HEAD_DIM), F32)
    for g in range(N_KV):
        in_group = (head // rep) == g
        k_head = 2 * N_KV + g
        v_head = 3 * N_KV + g
        tail_k = jnp.where(row_b == 0, new_kv_ref[k_head:k_head + 1, :], 0.0)
        tail_v = jnp.where(row_b == 0, new_kv_ref[v_head:v_head + 1, :], 0.0)
        k_parts, v_parts = [], []
        base = jnp.zeros((1, n_keys), jnp.int32)
        for k in range(N_SELECT):
            blk = sel_ref[b, g * N_SELECT + k]
            is_tail = blk >= n_past_blocks
            blk_ref = blk_refs[g * N_SELECT + k]
            k_rows = blk_ref[pl.ds(k_head, SLC_BLOCK, stride=n_paged), :]
            v_rows = blk_ref[pl.ds(v_head, SLC_BLOCK, stride=n_paged), :]
            k_parts.append(jnp.where(is_tail, tail_k, k_rows).astype(BF16))
            v_parts.append(jnp.where(is_tail, tail_v, v_rows).astype(BF16))
            base = jnp.where(lane // SLC_BLOCK == k, blk * SLC_BLOCK, base)
        keys = jnp.concatenate(k_parts, axis=0)
        vals = jnp.concatenate(v_parts, axis=0)
        tok = base + lane % SLC_BLOCK
        s = lax.dot_general(q, keys, _NT, preferred_element_type=F32)
        p = _softmax_rows(s, tok <= t_pos)
        o_s = jnp.where(in_group, jnp.dot(p.astype(BF16), vals, preferred_element_type=F32), o_s)

        kw = state_ref[pl.ds(g, w_keep, stride=n_win), :].astype(BF16)
        vw = state_ref[pl.ds(N_KV + g, w_keep, stride=n_win), :].astype(BF16)
        kw_new = new_win_ref[g:g + 1, :]
        vw_new = new_win_ref[N_KV + g:N_KV + g + 1, :]
        dist = w_keep - lax.broadcasted_iota(jnp.int32, (1, w_keep), 1)
        m_w = (dist >= 0) & (dist < WINDOW) & (t_pos - dist >= 0)
        s_w = jnp.where(m_w, lax.dot_general(q, kw, _NT, preferred_element_type=F32), NEG)
        s_new = jnp.sum(qf * kw_new, axis=-1, keepdims=True)
        m = jnp.maximum(jnp.max(s_w, axis=-1, keepdims=True), s_new)
        e_w = jnp.exp2(s_w - m)
        e_new = jnp.exp2(s_new - m)
        denom = jnp.sum(e_w, axis=-1, keepdims=True) + e_new
        num = jnp.dot(e_w.astype(BF16), vw, preferred_element_type=F32) + e_new * vw_new
        o_w = jnp.where(in_group, num / denom, o_w)

    gates = jnp.broadcast_to(gate_ref[...], (n_q, LANES))
    glane = lax.broadcasted_iota(jnp.int32, (n_q, LANES), 1)
    gsel = [jnp.sum(jnp.where(glane == head * N_BRANCH + br, gates, 0.0), axis=-1, keepdims=True)
            for br in range(N_BRANCH)]
    o_ref[...] = (gsel[0] * oc_ref[...] + gsel[1] * o_s + gsel[2] * o_w).astype(BF16)


def _sample_attn(page_table, sel_idx, cache_blocks, new_kv, new_win, state3, q3, gates3, o_c,
                 t_pos, n_past_blocks, blocks_per_page):
    bd = page_table.shape[0]
    n_q = q3.shape[1]

    n_pool = cache_blocks.shape[0] // blocks_per_page

    def blk_spec(g, k):
        def index(i, pt, sel):
            blk = jnp.clip(sel[i, g * N_SELECT + k], 0, n_past_blocks - 1)
            page = jnp.clip(pt[i, blk // blocks_per_page], 0, n_pool - 1)
            return (page * blocks_per_page + blk % blocks_per_page, 0, 0)
        return pl.BlockSpec((None,) + cache_blocks.shape[1:], index)

    blk_specs = [blk_spec(g, k) for g in range(N_KV) for k in range(N_SELECT)]
    per_b = lambda shape: pl.BlockSpec((None,) + shape, lambda i, pt, sel: (i, 0, 0))
    grid_spec = pltpu.PrefetchScalarGridSpec(
        num_scalar_prefetch=2,
        grid=(bd,),
        in_specs=blk_specs + [
            per_b(new_kv.shape[1:]),
            per_b(new_win.shape[1:]),
            per_b(state3.shape[1:]),
            per_b((n_q, HEAD_DIM)),
            per_b((1, LANES)),
            per_b((n_q, HEAD_DIM)),
        ],
        out_specs=per_b((n_q, HEAD_DIM)),
    )
    kern = functools.partial(_sample_attn_kernel, t_pos=t_pos, n_past_blocks=n_past_blocks)
    n_blk = N_KV * N_SELECT
    return pl.pallas_call(
        kern,
        grid_spec=grid_spec,
        out_shape=jax.ShapeDtypeStruct((bd, n_q, HEAD_DIM), BF16),
        compiler_params=_params("parallel"),
        name="sample_attn",
    )(page_table, sel_idx, *([cache_blocks] * n_blk), new_kv, new_win, state3, q3, gates3, o_c)


def kernel(x_prompt, x_sample, cache_kv, state_kv_win, page_table, a_norm, a_w_in, a_v_norm, a_w_s,
           a_b_s, a_w_out, mlp_norm, mlp_w_up, mlp_w_down, kv_norm, w_kv, cmp_pe, cmp_w1, cmp_w2,
           k_norm, b_norm, b_w_in, b_q_norm, b_w_out):
    bp, t, d = x_prompt.shape
    bd, td, _ = x_sample.shape
    n_pool, page_size = cache_kv.shape[:2]
    n_pages = page_table.shape[1]
    past_len = n_pages * page_size
    w_keep = state_kv_win.shape[1]
    depth = mlp_norm.shape[0]
    n_a = a_norm.shape[0]
    n_b = b_norm.shape[0]
    d_q = b_w_out.shape[1]
    assert td == 1 and n_b == 1 and depth == n_a + n_b
    assert d // A_GROUPS == LANES and t % KV_TILE == 0 and t >= Q_BLOCK + WINDOW
    assert w_keep == WINDOW and page_size % SLC_BLOCK == 0 and past_len % KV_ALIGN == 0

    row = lambda v: v.reshape(1, -1)
    hp = x_prompt.reshape(bp * t, d)
    hs = x_sample.reshape(bd, d)
    tm = 512

    w_up = mlp_w_up.astype(BF16)
    w_down = mlp_w_down.astype(BF16)

    v_rows = []
    for l in range(n_a):
        w_in = a_w_in[l].astype(BF16)
        w_out = a_w_out[l].astype(BF16)
        zp = _gmlp_in(hp, row(a_norm[l]), w_in, 2 * tm, 512)
        hp = _gmlp_out(hp, zp, row(a_v_norm[l]), a_w_s[l], a_b_s[l].T, w_out, 256)
        hp = _mlp(hp, row(mlp_norm[l]), w_up, w_down, l, 2 * tm, 512)
        zs = _gmlp_in(hs, row(a_norm[l]), w_in, bd, 512)
        wd = jnp.repeat(a_w_s[l][:, 0, 0], d // A_GROUPS).reshape(1, d)
        bb = jnp.repeat(a_b_s[l][:, 0], d // A_GROUPS).reshape(1, d)
        hs, v_s = _gmlp_out_single(hs, zs, row(a_v_norm[l]), wd, bb, w_out)
        v_rows.append(v_s.reshape(bd, td, d))
        hs = _mlp(hs, row(mlp_norm[l]), w_up, w_down, l, bd, 512)

    w_kv_b = w_kv.astype(BF16)
    cos_p, sin_p = _rope_tables(jnp.arange(t))
    cos_pp, sin_pp = jnp.tile(cos_p, (bp, 1)), jnp.tile(sin_p, (bp, 1))
    cos_s, sin_s = _rope_tables(jnp.full((bd,), past_len))
    paged_p, win_p, cmp_rows_p, kvb_p = _kv_proj(hp, row(kv_norm), w_kv_b, k_norm, cos_pp, sin_pp, tm)
    paged_s, win_s, _, _ = _kv_proj(hs, row(kv_norm), w_kv_b, k_norm, cos_s, sin_s, bd)

    j = 0
    wq = b_w_in[j].astype(BF16)
    wg = jnp.pad(b_w_in[j][:, d_q:], ((0, 0), (0, LANES - (b_w_in.shape[2] - d_q)))).astype(BF16)
    q_p, gates_p = _q_proj(hp, row(b_norm[j]), wq, wg, row(b_q_norm[j]), cos_pp, sin_pp, tm, d_q)
    q_s, gates_s = _q_proj(hs, row(b_norm[j]), wq, wg, row(b_q_norm[j]), cos_s, sin_s, bd, d_q)

    w1cat = jnp.concatenate([cmp_w1[:, :CMP_STRIDE], cmp_w1[:, CMP_STRIDE:]], axis=-1).astype(BF16)
    w1cat = w1cat.reshape(2, CMP_STRIDE * HEAD_DIM, 2 * HEAD_DIM)
    w1f = cmp_w1.reshape(2, CMP_BLOCK * HEAD_DIM, HEAD_DIM).astype(BF16)
    pe8 = jnp.broadcast_to(cmp_pe.reshape(2, 1, CMP_BLOCK * HEAD_DIM),
                           (2, SUBLANES, CMP_BLOCK * HEAD_DIM)).astype(BF16)
    w2 = cmp_w2.astype(BF16)

    n_sub_p = t // CMP_STRIDE
    n_slc_p = t // SLC_BLOCK
    cos_c, sin_c = _rope_tables(jnp.arange(n_sub_p) * CMP_STRIDE + CMP_BLOCK - 1)
    kc_p, vc_p = _compress_prompt(cmp_rows_p, w1cat, pe8, w1f, w2, k_norm, cos_c, sin_c, bp, t)
    ovt = _overlap_matrix(n_sub_p - 1, n_slc_p, n_sub_p, n_slc_p).T
    expand = jnp.where(jnp.arange(t)[:, None] // SLC_BLOCK == jnp.arange(LANES)[None, :], NEG, 0.0).astype(BF16)
    o_p = _attn_prompt(q_p, gates_p, kc_p, vc_p, kvb_p, ovt, expand, bp, t)
    w_o = b_w_out[j].astype(BF16)
    hp = _out_proj(hp, o_p, w_o, tm)

    tp_s = -(-(past_len + td) // KV_ALIGN) * KV_ALIGN
    n_sub_s = tp_s // CMP_STRIDE
    n_slc_s = tp_s // SLC_BLOCK
    n_pad_s = -(-n_sub_s // SUBLANES) * SUBLANES
    sel_lanes = -(-n_slc_s // LANES) * LANES
    assert past_len // SLC_BLOCK + 1 >= N_SELECT
    cos_cs, sin_cs = _rope_tables(jnp.arange(n_pad_s) * CMP_STRIDE + CMP_BLOCK - 1)
    ov_s = _overlap_matrix(n_sub_s - 1, n_slc_s, n_pad_s, sel_lanes)
    n_paged = 4 * N_KV
    n_win = 2 * N_KV
    cache_pages = cache_kv.reshape(n_pool, page_size * n_paged, HEAD_DIM)
    q3 = q_s.reshape(bd, d_q // HEAD_DIM, HEAD_DIM)
    new_kv = paged_s.reshape(bd, n_paged, HEAD_DIM)
    new_win = jnp.pad(win_s.reshape(bd, n_win, HEAD_DIM), ((0, 0), (0, SUBLANES - n_win), (0, 0)))
    o_c, sel_idx = _sample_cmp(page_table, cache_pages, new_kv, w1cat, pe8, w1f, w2, k_norm, cos_cs,
                               sin_cs, q3, ov_s, min(16, n_pages), past_len, n_slc_s)
    blocks_per_page = page_size // SLC_BLOCK
    cache_blocks = cache_kv.reshape(n_pool * blocks_per_page, SLC_BLOCK * n_paged, HEAD_DIM)
    state3 = state_kv_win.reshape(bd, w_keep * n_win, HEAD_DIM)
    o_s = _sample_attn(page_table, sel_idx[..., 0].reshape(bd, N_KV * N_SELECT), cache_blocks, new_kv,
                       new_win, state3, q3, gates_s.reshape(bd, 1, LANES), o_c, past_len,
                       past_len // SLC_BLOCK, blocks_per_page)
    hs = _out_proj(hs, o_s.reshape(bd, d_q), w_o, bd)

    l = n_a
    hp = _mlp(hp, row(mlp_norm[l]), w_up, w_down, l, 2 * tm, 512)
    hs = _mlp(hs, row(mlp_norm[l]), w_up, w_down, l, bd, 512)

    y_p = hp.reshape(bp, t, d)
    y_s = hs.reshape(bd, td, d)
    kv_p = paged_p.reshape(bp, t, 4, N_KV, HEAD_DIM)
    win_all_p = win_p.reshape(bp, t, 2, N_KV, HEAD_DIM)
    win_new_p = win_all_p[:, t - min(WINDOW, t):]
    kv_s = paged_s.reshape(bd, td, 4, N_KV, HEAD_DIM)
    win_new_s = jnp.concatenate(
        [state_kv_win[:, td:], win_s.reshape(bd, td, 2, N_KV, HEAD_DIM)], axis=1)
    v_a_s = jnp.stack(v_rows, axis=0)
    return (y_p, y_s, kv_p, win_new_p, kv_s, win_new_s, v_a_s)
```

```python
import functools

import jax
import jax.numpy as jnp
from jax import lax
from jax.experimental import pallas as pl
from jax.experimental.pallas import tpu as pltpu

F32 = jnp.float32
BF16 = jnp.bfloat16

CHUNK = 128
A_GROUPS = 16
HEAD_DIM = 128
N_KV = 2
CMP_STRIDE = 16
CMP_BLOCK = 2 * CMP_STRIDE
SLC_BLOCK = 64
N_SELECT = 16
WINDOW = 512
N_BRANCH = 3
KV_ALIGN = 64
ROT_DIM = HEAD_DIM // 4
ROPE_THETA = 500000.0
Q_BLOCK = 128
EPS = 1e-6
NEG = -1e30
FORCE_BONUS = 1e4
LOG2_E = 1.4426950408889634

LANES = 128
SUBLANES = 8
VMEM_LIMIT_BYTES = 56 * 1024 * 1024

KV_TILE = 512
N_KINDS = 6

_NT = (((1,), (1,)), ((), ()))


def _params(*sem):
    return pltpu.CompilerParams(dimension_semantics=sem, vmem_limit_bytes=VMEM_LIMIT_BYTES)


def _rms(x, g):
    ms = jnp.mean(x * x, axis=-1, keepdims=True)
    return x * lax.rsqrt(ms + EPS) * g


def _rope(x, cos, sin):
    half = ROT_DIM // 2
    lane = lax.broadcasted_iota(jnp.int32, x.shape, 1)
    partner = jnp.where(lane < half, pltpu.roll(x, LANES - half, 1), pltpu.roll(x, half, 1))
    return x * cos + partner * sin


def _rope_tables(pos):
    inv = ROPE_THETA ** (-jnp.arange(0, ROT_DIM, 2, dtype=F32) / ROT_DIM)
    ang = pos.astype(F32)[:, None] * inv[None, :]
    c, s = jnp.cos(ang), jnp.sin(ang)
    n = pos.shape[0]
    pad = LANES - ROT_DIM
    cos = jnp.concatenate([c, c, jnp.ones((n, pad), F32)], axis=1)
    sin = jnp.concatenate([-s, s, jnp.zeros((n, pad), F32)], axis=1)
    return cos, sin


def _gmlp_in_kernel(x_ref, g_ref, w_ref, z_ref, xn_ref):
    @pl.when(pl.program_id(1) == 0)
    def _():
        xn_ref[...] = _rms(x_ref[...], g_ref[...]).astype(BF16)

    z_ref[...] = jax.nn.gelu(jnp.dot(xn_ref[...], w_ref[...], preferred_element_type=F32)).astype(z_ref.dtype)


def _gmlp_in(x, g, w, tm, tn):
    m, d = x.shape
    n = w.shape[1]
    return pl.pallas_call(
        _gmlp_in_kernel,
        grid=(m // tm, n // tn),
        in_specs=[
            pl.BlockSpec((tm, d), lambda i, j: (i, 0)),
            pl.BlockSpec((1, d), lambda i, j: (0, 0)),
            pl.BlockSpec((d, tn), lambda i, j: (0, j)),
        ],
        out_specs=pl.BlockSpec((tm, tn), lambda i, j: (i, j)),
        out_shape=jax.ShapeDtypeStruct((m, n), BF16),
        scratch_shapes=[pltpu.VMEM((tm, d), BF16)],
        compiler_params=_params("parallel", "arbitrary"),
        name="gmlp_in",
    )(x, g, w)


def _gmlp_out_kernel(x_ref, u_ref, v_ref, vg_ref, ws_ref, bt_ref, wo_ref, o_ref, vn_ref, y_ref):
    tm = x_ref.shape[0]
    vn_ref[...] = _rms(v_ref[...].astype(F32), vg_ref[...]).astype(BF16)
    row = lax.broadcasted_iota(jnp.int32, (CHUNK, CHUNK), 0)
    col = lax.broadcasted_iota(jnp.int32, (CHUNK, CHUNK), 1)
    causal = row >= col
    for g in range(A_GROUPS):
        wsg = jnp.where(causal, ws_ref[g], 0.0).astype(BF16)
        bias = bt_ref[:, g:g + 1]
        cs = slice(g * LANES, (g + 1) * LANES)
        for c in range(tm // CHUNK):
            rs = slice(c * CHUNK, (c + 1) * CHUNK)
            mixed = jnp.dot(wsg, vn_ref[rs, cs], preferred_element_type=F32) + bias
            y_ref[rs, cs] = (u_ref[rs, cs].astype(F32) * mixed).astype(BF16)
    o_ref[...] = x_ref[...] + jnp.dot(y_ref[...], wo_ref[...], preferred_element_type=F32)


def _gmlp_out(x, z, vg, ws, bt, wo, tm):
    m, d = x.shape
    return pl.pallas_call(
        _gmlp_out_kernel,
        grid=(m // tm,),
        in_specs=[
            pl.BlockSpec((tm, d), lambda i: (i, 0)),
            pl.BlockSpec((tm, d), lambda i: (i, 0)),
            pl.BlockSpec((tm, d), lambda i: (i, 1)),
            pl.BlockSpec((1, d), lambda i: (0, 0)),
            pl.BlockSpec((A_GROUPS, CHUNK, CHUNK), lambda i: (0, 0, 0)),
            pl.BlockSpec((CHUNK, A_GROUPS), lambda i: (0, 0)),
            pl.BlockSpec((d, d), lambda i: (0, 0)),
        ],
        out_specs=pl.BlockSpec((tm, d), lambda i: (i, 0)),
        out_shape=jax.ShapeDtypeStruct((m, d), F32),
        scratch_shapes=[pltpu.VMEM((tm, d), BF16), pltpu.VMEM((tm, d), BF16)],
        compiler_params=_params("parallel"),
        name="gmlp_out",
    )(x, z, z, vg, ws, bt, wo)


def _gmlp_out_single_kernel(x_ref, u_ref, v_ref, vg_ref, wd_ref, bb_ref, wo_ref, o_ref, vn_ref):
    vn = _rms(v_ref[...].astype(F32), vg_ref[...])
    vn_ref[...] = vn
    mixed = vn * wd_ref[...] + bb_ref[...]
    y = (u_ref[...].astype(F32) * mixed).astype(BF16)
    o_ref[...] = x_ref[...] + jnp.dot(y, wo_ref[...], preferred_element_type=F32)


def _gmlp_out_single(x, z, vg, wd, bb, wo):
    m, d = x.shape
    full = lambda i: (0, 0)
    return pl.pallas_call(
        _gmlp_out_single_kernel,
        grid=(1,),
        in_specs=[
            pl.BlockSpec((m, d), full),
            pl.BlockSpec((m, d), lambda i: (0, 0)),
            pl.BlockSpec((m, d), lambda i: (0, 1)),
            pl.BlockSpec((1, d), full),
            pl.BlockSpec((1, d), full),
            pl.BlockSpec((1, d), full),
            pl.BlockSpec((d, d), full),
        ],
        out_specs=[pl.BlockSpec((m, d), full), pl.BlockSpec((m, d), full)],
        out_shape=[jax.ShapeDtypeStruct((m, d), F32), jax.ShapeDtypeStruct((m, d), F32)],
        compiler_params=_params("arbitrary"),
        name="gmlp_out_single",
    )(x, z, z, vg, wd, bb, wo)


def _mlp_kernel(x_ref, g_ref, wu_ref, wd_ref, o_ref, xn_ref):
    @pl.when(pl.program_id(1) == 0)
    def _():
        x = x_ref[...]
        xn_ref[...] = _rms(x, g_ref[...]).astype(BF16)
        o_ref[...] = x

    h = jnp.dot(xn_ref[...], wu_ref[...], preferred_element_type=F32)
    a = jnp.square(jnp.maximum(h, 0.0)).astype(BF16)
    o_ref[...] += jnp.dot(a, wd_ref[...], preferred_element_type=F32)


def _mlp(x, g, wu, wd, layer, tm, tf):
    m, d = x.shape
    f = wu.shape[2]
    return pl.pallas_call(
        _mlp_kernel,
        grid=(m // tm, f // tf),
        in_specs=[
            pl.BlockSpec((tm, d), lambda i, j: (i, 0)),
            pl.BlockSpec((1, d), lambda i, j: (0, 0)),
            pl.BlockSpec((None, d, tf), lambda i, j: (layer, 0, j)),
            pl.BlockSpec((None, tf, d), lambda i, j: (layer, j, 0)),
        ],
        out_specs=pl.BlockSpec((tm, d), lambda i, j: (i, 0)),
        out_shape=jax.ShapeDtypeStruct((m, d), F32),
        scratch_shapes=[pltpu.VMEM((tm, d), BF16)],
        compiler_params=_params("parallel", "arbitrary"),
        name="mlp",
    )(x, g, wu, wd)


def _kv_proj_kernel(x_ref, g_ref, w_ref, kn_ref, cos_ref, sin_ref, paged_ref, win_ref, cmp_ref, kvb_ref):
    tm = x_ref.shape[0]
    n_paged = 4 * N_KV
    n_win = 2 * N_KV
    xn = _rms(x_ref[...], g_ref[...]).astype(BF16)
    kv = jnp.dot(xn, w_ref[...], preferred_element_type=F32)
    cos, sin = cos_ref[...], sin_ref[...]
    for c in range(N_KINDS * N_KV):
        kind = c // N_KV
        h = kv[:, c * LANES:(c + 1) * LANES]
        if kind == 2:
            h = _rope(_rms(h, kn_ref[1:2, :]), cos, sin)
        elif kind == 4:
            h = _rope(_rms(h, kn_ref[2:3, :]), cos, sin)
        if kind < 4:
            paged_ref[pl.ds(c, tm, stride=n_paged), :] = h
        else:
            win_ref[pl.ds(c - n_paged, tm, stride=n_win), :] = h
        if kind < 2:
            cmp_ref[c] = h
        else:
            kvb_ref[c - 2 * N_KV] = h.astype(BF16)


def _kv_proj(x, g, w, kn, cos, sin, tm):
    m, d = x.shape
    n = w.shape[1]
    n_paged = 4 * N_KV
    n_win = 2 * N_KV
    return pl.pallas_call(
        _kv_proj_kernel,
        grid=(m // tm,),
        in_specs=[
            pl.BlockSpec((tm, d), lambda i: (i, 0)),
            pl.BlockSpec((1, d), lambda i: (0, 0)),
            pl.BlockSpec((d, n), lambda i: (0, 0)),
            pl.BlockSpec((3, HEAD_DIM), lambda i: (0, 0)),
            pl.BlockSpec((tm, LANES), lambda i: (i, 0)),
            pl.BlockSpec((tm, LANES), lambda i: (i, 0)),
        ],
        out_specs=[
            pl.BlockSpec((tm * n_paged, HEAD_DIM), lambda i: (i, 0)),
            pl.BlockSpec((tm * n_win, HEAD_DIM), lambda i: (i, 0)),
            pl.BlockSpec((2 * N_KV, tm, HEAD_DIM), lambda i: (0, i, 0)),
            pl.BlockSpec((4 * N_KV, tm, HEAD_DIM), lambda i: (0, i, 0)),
        ],
        out_shape=[
            jax.ShapeDtypeStruct((m * n_paged, HEAD_DIM), F32),
            jax.ShapeDtypeStruct((m * n_win, HEAD_DIM), F32),
            jax.ShapeDtypeStruct((2 * N_KV, m, HEAD_DIM), F32),
            jax.ShapeDtypeStruct((4 * N_KV, m, HEAD_DIM), BF16),
        ],
        compiler_params=_params("parallel"),
        name="kv_proj",
    )(x, g, w, kn, cos, sin)


def _q_proj_kernel(x_ref, g_ref, wq_ref, wg_ref, qn_ref, cos_ref, sin_ref, q_ref, gate_ref):
    xn = _rms(x_ref[...], g_ref[...]).astype(BF16)
    z = jnp.dot(xn, wq_ref[...], preferred_element_type=F32)
    cos, sin = cos_ref[...], sin_ref[...]
    qn = qn_ref[...]
    scale = HEAD_DIM ** -0.5 * LOG2_E
    for h in range(z.shape[1] // HEAD_DIM):
        cs = slice(h * HEAD_DIM, (h + 1) * HEAD_DIM)
        qh = _rope(_rms(z[:, cs], qn), cos, sin) * scale
        q_ref[:, cs] = qh.astype(BF16)
    gate_ref[...] = jax.nn.sigmoid(jnp.dot(xn, wg_ref[...], preferred_element_type=F32))


def _q_proj(x, g, wq, wg, qn, cos, sin, tm, n):
    m, d = x.shape
    return pl.pallas_call(
        _q_proj_kernel,
        grid=(m // tm,),
        in_specs=[
            pl.BlockSpec((tm, d), lambda i: (i, 0)),
            pl.BlockSpec((1, d), lambda i: (0, 0)),
            pl.BlockSpec((d, n), lambda i: (0, 0)),
            pl.BlockSpec((d, LANES), lambda i: (0, 0)),
            pl.BlockSpec((1, HEAD_DIM), lambda i: (0, 0)),
            pl.BlockSpec((tm, LANES), lambda i: (i, 0)),
            pl.BlockSpec((tm, LANES), lambda i: (i, 0)),
        ],
        out_specs=[
            pl.BlockSpec((tm, n), lambda i: (i, 0)),
            pl.BlockSpec((tm, LANES), lambda i: (i, 0)),
        ],
        out_shape=[
            jax.ShapeDtypeStruct((m, n), BF16),
            jax.ShapeDtypeStruct((m, LANES), F32),
        ],
        compiler_params=_params("parallel"),
        name="q_proj",
    )(x, g, wq, wg, qn, cos, sin)


def _out_proj_kernel(h_ref, o_ref, w_ref, y_ref):
    y_ref[...] = h_ref[...] + jnp.dot(o_ref[...], w_ref[...], preferred_element_type=F32)


def _out_proj(h, o, w, tm):
    m, d = h.shape
    k = o.shape[1]
    return pl.pallas_call(
        _out_proj_kernel,
        grid=(m // tm,),
        in_specs=[
            pl.BlockSpec((tm, d), lambda i: (i, 0)),
            pl.BlockSpec((tm, k), lambda i: (i, 0)),
            pl.BlockSpec((k, d), lambda i: (0, 0)),
        ],
        out_specs=pl.BlockSpec((tm, d), lambda i: (i, 0)),
        out_shape=jax.ShapeDtypeStruct((m, d), F32),
        compiler_params=_params("parallel"),
        name="out_proj",
    )(h, o, w)


def _cmp_bias(pe_ref, w1f_ref, kind):
    return jnp.dot(pe_ref[kind], w1f_ref[kind], preferred_element_type=F32)[0:1, :]


def _cmp_finish(fs, bias, w2, n_rows):
    first = fs[:, :HEAD_DIM]
    second = pltpu.roll(fs[:, HEAD_DIM:], n_rows - 1, 0)
    h = first + second + bias
    h = h * jax.nn.sigmoid(h)
    return jnp.dot(h.astype(BF16), w2, preferred_element_type=F32)


def _compress_prompt_kernel(krows_ref, vrows_ref, w1_ref, pe_ref, w1f_ref, w2_ref, kn_ref,
                            cos_ref, sin_ref, kc_ref, vc_ref):
    n_sub = krows_ref.shape[0] // CMP_STRIDE
    rowi = lax.broadcasted_iota(jnp.int32, (n_sub, HEAD_DIM), 0)
    for kind, rows_ref, out_ref in ((0, krows_ref, kc_ref), (1, vrows_ref, vc_ref)):
        x = jnp.concatenate(
            [rows_ref[pl.ds(r, n_sub, stride=CMP_STRIDE), :].astype(BF16) for r in range(CMP_STRIDE)],
            axis=1)
        fs = jnp.dot(x, w1_ref[kind], preferred_element_type=F32)
        out = _cmp_finish(fs, _cmp_bias(pe_ref, w1f_ref, kind), w2_ref[kind], n_sub)
        if kind == 0:
            out = _rope(_rms(out, kn_ref[0:1, :]), cos_ref[...], sin_ref[...])
        out_ref[...] = jnp.where(rowi < n_sub - 1, out, 0.0).astype(BF16)


def _compress_prompt(paged, w1cat, pe8, w1f, w2, kn, cos, sin, b, t):
    n_sub = t // CMP_STRIDE
    full = lambda nd: (lambda i, g: (0,) * nd)
    out_spec = pl.BlockSpec((None, None, n_sub, HEAD_DIM), lambda i, g: (i, g, 0, 0))
    out_shape = jax.ShapeDtypeStruct((b, N_KV, n_sub, HEAD_DIM), BF16)
    return pl.pallas_call(
        _compress_prompt_kernel,
        grid=(b, N_KV),
        in_specs=[
            pl.BlockSpec((None, t, HEAD_DIM), lambda i, g: (g, i, 0)),
            pl.BlockSpec((None, t, HEAD_DIM), lambda i, g: (N_KV + g, i, 0)),
            pl.BlockSpec(w1cat.shape, full(3)),
            pl.BlockSpec(pe8.shape, full(3)),
            pl.BlockSpec(w1f.shape, full(3)),
            pl.BlockSpec(w2.shape, full(3)),
            pl.BlockSpec(kn.shape, full(2)),
            pl.BlockSpec(cos.shape, full(2)),
            pl.BlockSpec(sin.shape, full(2)),
        ],
        out_specs=[out_spec, out_spec],
        out_shape=[out_shape, out_shape],
        compiler_params=_params("parallel", "parallel"),
        name="compress_prompt",
    )(paged, paged, w1cat, pe8, w1f, w2, kn, cos, sin)


def _split_bf16(x):
    hi = x.astype(BF16)
    lo = (x - hi.astype(F32)).astype(BF16)
    return hi, lo


def _overlap_matrix(n_cmp, n_slc, rows, cols):
    ci = jnp.arange(rows)[:, None] * CMP_STRIDE
    sj = jnp.arange(cols)[None, :] * SLC_BLOCK
    ov = (ci < sj + SLC_BLOCK) & (ci + CMP_BLOCK > sj)
    ov = ov & (jnp.arange(rows)[:, None] < n_cmp) & (jnp.arange(cols)[None, :] < n_slc)
    return ov.astype(BF16)


def _softmax_rows(s, valid):
    s = jnp.where(valid, s, NEG)
    e = jnp.exp2(s - jnp.max(s, axis=-1, keepdims=True))
    return e / jnp.sum(e, axis=-1, keepdims=True)


def _attn_prompt_kernel(q_ref, gate_ref, kc_ref, vc_ref, ks_ref, vs_ref, kw_ref, vw_ref,
                        ovt_ref, nexp_ref, o_ref, m_ref, acc_ref, s_ref, kext_ref):
    qb = pl.program_id(2)
    g = pl.program_id(1)
    nq = Q_BLOCK
    rep = q_ref.shape[1] // HEAD_DIM
    n_cmp_pad = kc_ref.shape[0]
    n_slc = ovt_ref.shape[0]
    t_len = ks_ref.shape[0]
    s0 = qb * nq

    q_all = q_ref[...]
    q2 = jnp.concatenate([q_all[:, r * HEAD_DIM:(r + 1) * HEAD_DIM] for r in range(rep)], axis=0)
    t_col = s0 + lax.broadcasted_iota(jnp.int32, (nq, 1), 0)

    kc = kc_ref[...]
    vc = vc_ref[...]
    cpos = lax.broadcasted_iota(jnp.int32, (1, n_cmp_pad), 1) * CMP_STRIDE + (CMP_BLOCK - 1)
    m_c = cpos <= t_col
    s_c = lax.dot_general(q2, kc, _NT, preferred_element_type=F32)
    psum = jnp.zeros((nq, n_cmp_pad), F32)
    p_parts = []
    for r in range(rep):
        p = jnp.where(m_c, _softmax_rows(s_c[r * nq:(r + 1) * nq], m_c), 0.0)
        psum = psum + p
        p_parts.append(p.astype(BF16))
    o_c = jnp.dot(jnp.concatenate(p_parts, axis=0), vc, preferred_element_type=F32)

    ovt = ovt_ref[...]
    p_hi, p_lo = _split_bf16(psum)
    imp_t = (lax.dot_general(ovt, p_hi, _NT, preferred_element_type=F32)
             + lax.dot_general(ovt, p_lo, _NT, preferred_element_type=F32))
    t_row = s0 + lax.broadcasted_iota(jnp.int32, (n_slc, nq), 1)
    blk = lax.broadcasted_iota(jnp.int32, (n_slc, nq), 0)
    cur = t_row // SLC_BLOCK
    causal = blk * SLC_BLOCK <= t_row
    forced = (blk == 0) | (blk == cur) | (blk == cur - 1)
    score = jnp.where(causal, imp_t + jnp.where(forced, FORCE_BONUS, 0.0), NEG)
    rank = jnp.zeros((n_slc, nq), jnp.int32)
    for i in range(n_slc):
        si = score[i:i + 1, :]
        later = (blk > i).astype(jnp.int32)
        rank = rank + jnp.where(si > score, 1, 0) + jnp.where(si == score, later, 0)
    sel_t = jnp.where(causal & (rank < N_SELECT), 1.0, 0.0)
    sel_t = jnp.concatenate([sel_t, jnp.zeros((LANES - n_slc, nq), F32)], axis=0)
    unsel = 1.0 - sel_t.T

    @pl.when(qb == 0)
    def _():
        kext_ref[:, :HEAD_DIM] = ks_ref[...]
        kext_ref[:, HEAD_DIM:] = nexp_ref[...]

    blk_lane = lax.broadcasted_iota(jnp.int32, (nq, LANES), 1)
    unsel_main = jnp.where(blk_lane >= s0 // SLC_BLOCK, 1.0, unsel).astype(BF16)
    q_ext = jnp.concatenate([q2, jnp.concatenate([unsel_main] * rep, axis=0)], axis=1)
    n_main = (s0 + KV_TILE - 1) // KV_TILE
    m_ref[...] = jnp.full(m_ref.shape, NEG, F32)

    def sweep1(off, width):
        s = lax.dot_general(q_ext, kext_ref[pl.ds(off, width), :], _NT, preferred_element_type=F32)
        s_ref[:, pl.ds(off, width)] = s
        m_run = m_ref[...]
        for c in range(width // LANES):
            m_run = jnp.maximum(m_run, s[:, c * LANES:(c + 1) * LANES])
        m_ref[...] = m_run

    def max_body(j, carry):
        sweep1(pl.multiple_of(j * 2 * KV_TILE, 2 * KV_TILE), 2 * KV_TILE)
        return carry

    lax.fori_loop(0, n_main // 2, max_body, 0)

    @pl.when(n_main % 2 == 1)
    def _():
        sweep1(pl.multiple_of((n_main - 1) * KV_TILE, KV_TILE), KV_TILE)

    d0 = pl.multiple_of(s0, nq)
    q_i = lax.broadcasted_iota(jnp.int32, (nq, nq), 0)
    k_i = lax.broadcasted_iota(jnp.int32, (nq, nq), 1)
    bias_d = (lax.dot_general(unsel.astype(BF16), nexp_ref[pl.ds(d0, nq), :], _NT,
                              preferred_element_type=F32)
              + jnp.where(k_i <= q_i, 0.0, NEG))
    s_d = lax.dot_general(q2, ks_ref[pl.ds(d0, nq), :], _NT, preferred_element_type=F32)
    s_d = (s_d.reshape(rep, nq, nq) + bias_d[None]).reshape(rep * nq, nq)
    m_all = jnp.maximum(m_ref[...], s_d)
    m_b = jnp.broadcast_to(jnp.max(m_all, axis=-1, keepdims=True), m_all.shape)
    m_ref[...] = m_b
    v_d = jnp.concatenate([vs_ref[pl.ds(d0, nq), :], jnp.ones((nq, HEAD_DIM), BF16)], axis=1)
    acc_ref[...] = jnp.dot(jnp.exp2(s_d - m_b).astype(BF16), v_d, preferred_element_type=F32)

    def sweep2(off, width):
        v1 = jnp.concatenate([vs_ref[pl.ds(off, width), :], jnp.ones((width, HEAD_DIM), BF16)], axis=1)
        m_rows = m_ref[...]
        p = jnp.concatenate(
            [jnp.exp2(s_ref[:, pl.ds(pl.multiple_of(off + c * LANES, LANES), LANES)] - m_rows).astype(BF16)
             for c in range(width // LANES)], axis=1)
        acc_ref[...] += jnp.dot(p, v1, preferred_element_type=F32)

    def pv_body(j, carry):
        sweep2(pl.multiple_of(j * 2 * KV_TILE, 2 * KV_TILE), 2 * KV_TILE)
        return carry

    lax.fori_loop(0, n_main // 2, pv_body, 0)

    @pl.when(n_main % 2 == 1)
    def _():
        sweep2(pl.multiple_of((n_main - 1) * KV_TILE, KV_TILE), KV_TILE)

    o_s = acc_ref[:, :HEAD_DIM] / acc_ref[:, HEAD_DIM:]

    slab = nq + WINDOW
    w0 = pl.multiple_of(jnp.maximum(s0 - WINDOW, 0), nq)
    kw = kw_ref[pl.ds(w0, slab), :]
    vw = jnp.concatenate([vw_ref[pl.ds(w0, slab), :], jnp.ones((slab, HEAD_DIM), BF16)], axis=1)
    dist = t_col - (w0 + lax.broadcasted_iota(jnp.int32, (nq, slab), 1))
    bias_w = jnp.where((dist >= 0) & (dist < WINDOW), 0.0, NEG)
    s_w = lax.dot_general(q2, kw, _NT, preferred_element_type=F32)
    p_parts = []
    for r in range(rep):
        s_r = s_w[r * nq:(r + 1) * nq] + bias_w
        p_parts.append(jnp.exp2(s_r - jnp.max(s_r, axis=-1, keepdims=True)).astype(BF16))
    o_w = jnp.dot(jnp.concatenate(p_parts, axis=0), vw, preferred_element_type=F32)
    o_w = o_w[:, :HEAD_DIM] / o_w[:, HEAD_DIM:]

    gates = gate_ref[...]
    for r in range(rep):
        rs = slice(r * nq, (r + 1) * nq)
        c0 = (g * rep + r) * N_BRANCH
        lane = lax.broadcasted_iota(jnp.int32, gates.shape, 1)
        gsel = [jnp.sum(jnp.where(lane == c0 + br, gates, 0.0), axis=-1, keepdims=True)
                for br in range(N_BRANCH)]
        out = gsel[0] * o_c[rs] + gsel[1] * o_s[rs] + gsel[2] * o_w[rs]
        o_ref[:, r * HEAD_DIM:(r + 1) * HEAD_DIM] = out.astype(BF16)


def _attn_prompt(q, gates, kc, vc, kvb, ovt, expand, b, t):
    nqb = t // Q_BLOCK
    rep = q.shape[1] // HEAD_DIM // N_KV
    n_cmp_pad = kc.shape[2]
    rows = lambda kind: pl.BlockSpec((None, t, HEAD_DIM), lambda i, g, j: ((kind - 2) * N_KV + g, i, 0))
    cmp_spec = pl.BlockSpec((None, None, n_cmp_pad, HEAD_DIM), lambda i, g, j: (i, g, 0, 0))
    return pl.pallas_call(
        _attn_prompt_kernel,
        grid=(b, N_KV, nqb),
        in_specs=[
            pl.BlockSpec((Q_BLOCK, rep * HEAD_DIM), lambda i, g, j: (i * nqb + j, g)),
            pl.BlockSpec((Q_BLOCK, LANES), lambda i, g, j: (i * nqb + j, 0)),
            cmp_spec, cmp_spec,
            rows(2), rows(3), rows(4), rows(5),
            pl.BlockSpec(ovt.shape, lambda i, g, j: (0, 0)),
            pl.BlockSpec(expand.shape, lambda i, g, j: (0, 0)),
        ],
        out_specs=pl.BlockSpec((Q_BLOCK, rep * HEAD_DIM), lambda i, g, j: (i * nqb + j, g)),
        out_shape=jax.ShapeDtypeStruct(q.shape, BF16),
        scratch_shapes=[
            pltpu.VMEM((rep * Q_BLOCK, LANES), F32),
            pltpu.VMEM((rep * Q_BLOCK, 2 * HEAD_DIM), F32),
            pltpu.VMEM((rep * Q_BLOCK, t), F32),
            pltpu.VMEM((t, 2 * HEAD_DIM), BF16),
        ],
        compiler_params=_params("parallel", "parallel", "arbitrary"),
        name="attn_prompt",
    )(q, gates, kc, vc, kvb, kvb, kvb, kvb, ovt, expand)


def _sample_cmp_kernel(pt_ref, *refs, pages_per_step, n_steps, t_pos, n_slc):
    del pt_ref
    page_refs = refs[:pages_per_step]
    (new_ref, w1_ref, pe_ref, w1f_ref, w2_ref, kn_ref, cos_ref, sin_ref, q_ref, ov_ref,
     oc_ref, idx_ref, fs_ref, stage_ref) = refs[pages_per_step:]
    step = pl.program_id(1)
    n_heads = N_KV * 2
    heads_per_row = 4 * N_KV
    page_rows = page_refs[0].shape[0] // heads_per_row
    sub_per_page = page_rows // CMP_STRIDE
    rows_per_step = pages_per_step * sub_per_page
    n_pad = fs_ref.shape[1]
    n_past = n_steps * rows_per_step
    row0 = pl.multiple_of(step * rows_per_step, rows_per_step)

    assert heads_per_row == SUBLANES
    slab = stage_ref.shape[0] // heads_per_row
    for k, p in enumerate(page_refs):
        for n in range(sub_per_page):
            row = k * sub_per_page + n
            for r in range(CMP_STRIDE):
                tok = p[pl.ds((n * CMP_STRIDE + r) * heads_per_row, heads_per_row), :]
                tile = (row // SUBLANES) * CMP_STRIDE + r
                stage_ref[pl.ds(tile * SUBLANES + row % SUBLANES, heads_per_row, stride=slab), :] = tok
    for c in range(n_heads):
        x = jnp.concatenate(
            [jnp.concatenate(
                [stage_ref[pl.ds(c * slab + (i * CMP_STRIDE + r) * SUBLANES, SUBLANES), :]
                 for r in range(CMP_STRIDE)], axis=1)
             for i in range(rows_per_step // SUBLANES)], axis=0)
        fs_ref[c, pl.ds(row0, rows_per_step), :] = jnp.dot(
            x.astype(BF16), w1_ref[c // N_KV], preferred_element_type=F32)

    @pl.when(step == n_steps - 1)
    def _():
        tail = n_pad - n_past
        row_t = lax.broadcasted_iota(jnp.int32, (tail, HEAD_DIM), 0)
        rowi = lax.broadcasted_iota(jnp.int32, (n_pad, HEAD_DIM), 0)
        n_sub = (t_pos + 1 + KV_ALIGN - 1) // KV_ALIGN * KV_ALIGN // CMP_STRIDE
        outs = []
        for c in range(n_heads):
            kind = c // N_KV
            new_row = new_ref[c:c + 1, :]
            x_tail = jnp.where(row_t == 0, new_row, 0.0).astype(BF16)
            fs_ref[c, n_past:n_pad, :] = jnp.dot(x_tail, w1_ref[kind, 0:HEAD_DIM, :],
                                                 preferred_element_type=F32)
            out = _cmp_finish(fs_ref[c], _cmp_bias(pe_ref, w1f_ref, kind), w2_ref[kind], n_pad)
            if kind == 0:
                out = _rope(_rms(out, kn_ref[0:1, :]), cos_ref[...], sin_ref[...])
            outs.append(jnp.where(rowi < n_sub - 1, out, 0.0).astype(BF16))

        q = q_ref[...]
        n_q = q.shape[0]
        rep = n_q // N_KV
        head = lax.broadcasted_iota(jnp.int32, (n_q, 1), 0)
        cpos = lax.broadcasted_iota(jnp.int32, (1, n_pad), 1) * CMP_STRIDE + (CMP_BLOCK - 1)
        m_c = cpos <= t_pos
        lanes = ov_ref.shape[1]
        blk = lax.broadcasted_iota(jnp.int32, (1, lanes), 1)
        cur = t_pos // SLC_BLOCK
        causal = (blk * SLC_BLOCK <= t_pos) & (blk < n_slc)
        forced = (blk == 0) | (blk == cur) | (blk == cur - 1)
        eye_i = lax.broadcasted_iota(jnp.int32, (lanes, lanes), 0)
        eye_j = lax.broadcasted_iota(jnp.int32, (lanes, lanes), 1)
        o_c = jnp.zeros((n_q, HEAD_DIM), F32)
        for g in range(N_KV):
            kc, vc = outs[g], outs[N_KV + g]
            in_group = (head // rep) == g
            s = lax.dot_general(q, kc, _NT, preferred_element_type=F32)
            p = jnp.where(m_c, _softmax_rows(s, m_c), 0.0)
            o_g = jnp.dot(p.astype(BF16), vc, preferred_element_type=F32)
            o_c = jnp.where(in_group, o_g, o_c)
            psum = jnp.sum(jnp.where(in_group, p, 0.0), axis=0, keepdims=True)
            psum8 = jnp.broadcast_to(psum, (SUBLANES, n_pad))
            p_hi, p_lo = _split_bf16(psum8)
            imp = (jnp.dot(p_hi, ov_ref[...], preferred_element_type=F32)
                   + jnp.dot(p_lo, ov_ref[...], preferred_element_type=F32))[0:1, :]
            score = jnp.where(causal, imp + jnp.where(forced, FORCE_BONUS, 0.0), NEG)
            score_b = jnp.broadcast_to(score, (lanes, lanes))
            score_col = jnp.sum(jnp.where(eye_i == eye_j, score_b, 0.0), axis=1, keepdims=True)
            beats = (score_col > score_b) | ((score_col == score_b) & (eye_i < eye_j))
            rank = jnp.sum(jnp.where(beats, 1.0, 0.0), axis=0, keepdims=True)
            sel = jnp.where(causal & (rank < N_SELECT), 1.0, 0.0)
            sel_b = jnp.broadcast_to(sel, (lanes, lanes))
            sel_col = jnp.sum(jnp.where(eye_i == eye_j, sel_b, 0.0), axis=1, keepdims=True)
            slot = jnp.sum(jnp.where(eye_i < eye_j, sel_col, 0.0), axis=0, keepdims=True)
            slot_b = jnp.broadcast_to(slot, (N_SELECT, lanes))
            k_i = lax.broadcasted_iota(jnp.int32, (N_SELECT, lanes), 0).astype(F32)
            j_i = lax.broadcasted_iota(jnp.int32, (N_SELECT, lanes), 1).astype(F32)
            hit = (slot_b == k_i) & (jnp.broadcast_to(sel, (N_SELECT, lanes)) > 0.5)
            idx = jnp.sum(jnp.where(hit, j_i, 0.0), axis=1, keepdims=True)
            idx_ref[g] = jnp.broadcast_to(idx, (N_SELECT, LANES)).astype(jnp.int32)
        oc_ref[...] = o_c


def _sample_cmp(page_table, cache_pages, new_rows, w1cat, pe8, w1f, w2, kn, cos, sin, q3, ov,
                pages_per_step, t_pos, n_slc):
    bd, n_pages = page_table.shape
    n_steps = n_pages // pages_per_step
    flat_rows = cache_pages.shape[1]
    page_rows = flat_rows // (4 * N_KV)
    n_pad = cos.shape[0]
    n_q = q3.shape[1]
    full = lambda nd: (lambda i, s, pt: (0,) * nd)

    n_pool = cache_pages.shape[0]

    def page_spec(k):
        return pl.BlockSpec(
            (None, flat_rows, HEAD_DIM),
            lambda i, s, pt: (jnp.clip(pt[i, s * pages_per_step + k], 0, n_pool - 1), 0, 0))

    grid_spec = pltpu.PrefetchScalarGridSpec(
        num_scalar_prefetch=1,
        grid=(bd, n_steps),
        in_specs=[page_spec(k) for k in range(pages_per_step)] + [
            pl.BlockSpec((None,) + new_rows.shape[1:], lambda i, s, pt: (i, 0, 0)),
            pl.BlockSpec(w1cat.shape, full(3)),
            pl.BlockSpec(pe8.shape, full(3)),
            pl.BlockSpec(w1f.shape, full(3)),
            pl.BlockSpec(w2.shape, full(3)),
            pl.BlockSpec(kn.shape, full(2)),
            pl.BlockSpec(cos.shape, full(2)),
            pl.BlockSpec(sin.shape, full(2)),
            pl.BlockSpec((None, n_q, HEAD_DIM), lambda i, s, pt: (i, 0, 0)),
            pl.BlockSpec(ov.shape, full(2)),
        ],
        out_specs=[
            pl.BlockSpec((None, n_q, HEAD_DIM), lambda i, s, pt: (i, 0, 0)),
            pl.BlockSpec((None, N_KV, N_SELECT, LANES), lambda i, s, pt: (i, 0, 0, 0)),
        ],
        scratch_shapes=[pltpu.VMEM((N_KV * 2, n_pad, 2 * HEAD_DIM), F32),
                        pltpu.VMEM((4 * N_KV * (pages_per_step * page_rows + 4), HEAD_DIM), F32)],
    )
    kern = functools.partial(_sample_cmp_kernel, pages_per_step=pages_per_step, n_steps=n_steps,
                             t_pos=t_pos, n_slc=n_slc)
    return pl.pallas_call(
        kern,
        grid_spec=grid_spec,
        out_shape=[
            jax.ShapeDtypeStruct((bd, n_q, HEAD_DIM), F32),
            jax.ShapeDtypeStruct((bd, N_KV, N_SELECT, LANES), jnp.int32),
        ],
        compiler_params=_params("parallel", "arbitrary"),
        name="sample_cmp",
    )(page_table, *([cache_pages] * pages_per_step), new_rows, w1cat, pe8, w1f, w2, kn, cos, sin, q3, ov)


def _sample_attn_kernel(pt_ref, sel_ref, *refs, t_pos, n_past_blocks):
    del pt_ref
    n_blk = N_KV * N_SELECT
    blk_refs = refs[:n_blk]
    new_kv_ref, new_win_ref, state_ref, q_ref, gate_ref, oc_ref, o_ref = refs[n_blk:]
    n_paged = 4 * N_KV
    n_win = 2 * N_KV
    b = pl.program_id(0)
    q = q_ref[...]
    qf = q.astype(F32)
    n_q = q.shape[0]
    rep = n_q // N_KV
    head = lax.broadcasted_iota(jnp.int32, (n_q, 1), 0)
    n_keys = N_SELECT * SLC_BLOCK
    lane = lax.broadcasted_iota(jnp.int32, (1, n_keys), 1)
    row_b = lax.broadcasted_iota(jnp.int32, (SLC_BLOCK, HEAD_DIM), 0)
    w_keep = state_ref.shape[0] // n_win
    o_s = jnp.zeros((n_q, HEAD_DIM), F32)
    o_w = jnp.zeros((n_q, HEAD_DIM), F32)
    for g in range(N_KV):
        in_group = (head // rep) == g
        k_head = 2 * N_KV + g
        v_head = 3 * N_KV + g
        tail_k = jnp.where(row_b == 0, new_kv_ref[k_head:k_head + 1, :], 0.0)
        tail_v = jnp.where(row_b == 0, new_kv_ref[v_head:v_head + 1, :], 0.0)
        k_parts, v_parts = [], []
        base = jnp.zeros((1, n_keys), jnp.int32)
        for k in range(N_SELECT):
            blk = sel_ref[b, g * N_SELECT + k]
            is_tail = blk >= n_past_blocks
            blk_ref = blk_refs[g * N_SELECT + k]
            k_rows = blk_ref[pl.ds(k_head, SLC_BLOCK, stride=n_paged), :]
            v_rows = blk_ref[pl.ds(v_head, SLC_BLOCK, stride=n_paged), :]
            k_parts.append(jnp.where(is_tail, tail_k, k_rows).astype(BF16))
            v_parts.append(jnp.where(is_tail, tail_v, v_rows).astype(BF16))
            base = jnp.where(lane // SLC_BLOCK == k, blk * SLC_BLOCK, base)
        keys = jnp.concatenate(k_parts, axis=0)
        vals = jnp.concatenate(v_parts, axis=0)
        tok = base + lane % SLC_BLOCK
        s = lax.dot_general(q, keys, _NT, preferred_element_type=F32)
        p = _softmax_rows(s, tok <= t_pos)
        o_s = jnp.where(in_group, jnp.dot(p.astype(BF16), vals, preferred_element_type=F32), o_s)

        kw = state_ref[pl.ds(g, w_keep, stride=n_win), :].astype(BF16)
        vw = state_ref[pl.ds(N_KV + g, w_keep, stride=n_win), :].astype(BF16)
        kw_new = new_win_ref[g:g + 1, :]
        vw_new = new_win_ref[N_KV + g:N_KV + g + 1, :]
        dist = w_keep - lax.broadcasted_iota(jnp.int32, (1, w_keep), 1)
        m_w = (dist >= 0) & (dist < WINDOW) & (t_pos - dist >= 0)
        s_w = jnp.where(m_w, lax.dot_general(q, kw, _NT, preferred_element_type=F32), NEG)
        s_new = jnp.sum(qf * kw_new, axis=-1, keepdims=True)
        m = jnp.maximum(jnp.max(s_w, axis=-1, keepdims=True), s_new)
        e_w = jnp.exp2(s_w - m)
        e_new = jnp.exp2(s_new - m)
        denom = jnp.sum(e_w, axis=-1, keepdims=True) + e_new
        num = jnp.dot(e_w.astype(BF16), vw, preferred_element_type=F32) + e_new * vw_new
        o_w = jnp.where(in_group, num / denom, o_w)

    gates = jnp.broadcast_to(gate_ref[...], (n_q, LANES))
    glane = lax.broadcasted_iota(jnp.int32, (n_q, LANES), 1)
    gsel = [jnp.sum(jnp.where(glane == head * N_BRANCH + br, gates, 0.0), axis=-1, keepdims=True)
            for br in range(N_BRANCH)]
    o_ref[...] = (gsel[0] * oc_ref[...] + gsel[1] * o_s + gsel[2] * o_w).astype(BF16)


def _sample_attn(page_table, sel_idx, cache_blocks, new_kv, new_win, state3, q3, gates3, o_c,
                 t_pos, n_past_blocks, blocks_per_page):
    bd = page_table.shape[0]
    n_q = q3.shape[1]

    n_pool = cache_blocks.shape[0] // blocks_per_page

    def blk_spec(g, k):
        def index(i, pt, sel):
            blk = jnp.clip(sel[i, g * N_SELECT + k], 0, n_past_blocks - 1)
            page = jnp.clip(pt[i, blk // blocks_per_page], 0, n_pool - 1)
            return (page * blocks_per_page + blk % blocks_per_page, 0, 0)
        return pl.BlockSpec((None,) + cache_blocks.shape[1:], index)

    blk_specs = [blk_spec(g, k) for g in range(N_KV) for k in range(N_SELECT)]
    per_b = lambda shape: pl.BlockSpec((None,) + shape, lambda i, pt, sel: (i, 0, 0))
    grid_spec = pltpu.PrefetchScalarGridSpec(
        num_scalar_prefetch=2,
        grid=(bd,),
        in_specs=blk_specs + [
            per_b(new_kv.shape[1:]),
            per_b(new_win.shape[1:]),
            per_b(state3.shape[1:]),
            per_b((n_q, HEAD_DIM)),
            per_b((1, LANES)),
            per_b((n_q, HEAD_DIM)),
        ],
        out_specs=per_b((n_q, HEAD_DIM)),
    )
    kern = functools.partial(_sample_attn_kernel, t_pos=t_pos, n_past_blocks=n_past_blocks)
    n_blk = N_KV * N_SELECT
    return pl.pallas_call(
        kern,
        grid_spec=grid_spec,
        out_shape=jax.ShapeDtypeStruct((bd, n_q, HEAD_DIM), BF16),
        compiler_params=_params("parallel"),
        name="sample_attn",
    )(page_table, sel_idx, *([cache_blocks] * n_blk), new_kv, new_win, state3, q3, gates3, o_c)


def kernel(x_prompt, x_sample, cache_kv, state_kv_win, page_table, a_norm, a_w_in, a_v_norm, a_w_s,
           a_b_s, a_w_out, mlp_norm, mlp_w_up, mlp_w_down, kv_norm, w_kv, cmp_pe, cmp_w1, cmp_w2,
           k_norm, b_norm, b_w_in, b_q_norm, b_w_out):
    bp, t, d = x_prompt.shape
    bd, td, _ = x_sample.shape
    n_pool, page_size = cache_kv.shape[:2]
    n_pages = page_table.shape[1]
    past_len = n_pages * page_size
    w_keep = state_kv_win.shape[1]
    depth = mlp_norm.shape[0]
    n_a = a_norm.shape[0]
    n_b = b_norm.shape[0]
    d_q = b_w_out.shape[1]
    assert td == 1 and n_b == 1 and depth == n_a + n_b
    assert d // A_GROUPS == LANES and t % KV_TILE == 0 and t >= Q_BLOCK + WINDOW
    assert w_keep == WINDOW and page_size % SLC_BLOCK == 0 and past_len % KV_ALIGN == 0

    row = lambda v: v.reshape(1, -1)
    hp = x_prompt.reshape(bp * t, d)
    hs = x_sample.reshape(bd, d)
    tm = 512

    w_up = mlp_w_up.astype(BF16)
    w_down = mlp_w_down.astype(BF16)

    v_rows = []
    for l in range(n_a):
        w_in = a_w_in[l].astype(BF16)
        w_out = a_w_out[l].astype(BF16)
        zp = _gmlp_in(hp, row(a_norm[l]), w_in, 2 * tm, 512)
        hp = _gmlp_out(hp, zp, row(a_v_norm[l]), a_w_s[l], a_b_s[l].T, w_out, 256)
        hp = _mlp(hp, row(mlp_norm[l]), w_up, w_down, l, 2 * tm, 512)
        zs = _gmlp_in(hs, row(a_norm[l]), w_in, bd, 512)
        wd = jnp.repeat(a_w_s[l][:, 0, 0], d // A_GROUPS).reshape(1, d)
        bb = jnp.repeat(a_b_s[l][:, 0], d // A_GROUPS).reshape(1, d)
        hs, v_s = _gmlp_out_single(hs, zs, row(a_v_norm[l]), wd, bb, w_out)
        v_rows.append(v_s.reshape(bd, td, d))
        hs = _mlp(hs, row(mlp_norm[l]), w_up, w_down, l, bd, 512)

    w_kv_b = w_kv.astype(BF16)
    cos_p, sin_p = _rope_tables(jnp.arange(t))
    cos_pp, sin_pp = jnp.tile(cos_p, (bp, 1)), jnp.tile(sin_p, (bp, 1))
    cos_s, sin_s = _rope_tables(jnp.full((bd,), past_len))
    paged_p, win_p, cmp_rows_p, kvb_p = _kv_proj(hp, row(kv_norm), w_kv_b, k_norm, cos_pp, sin_pp, tm)
    paged_s, win_s, _, _ = _kv_proj(hs, row(kv_norm), w_kv_b, k_norm, cos_s, sin_s, bd)

    j = 0
    wq = b_w_in[j].astype(BF16)
    wg = jnp.pad(b_w_in[j][:, d_q:], ((0, 0), (0, LANES - (b_w_in.shape[2] - d_q)))).astype(BF16)
    q_p, gates_p = _q_proj(hp, row(b_norm[j]), wq, wg, row(b_q_norm[j]), cos_pp, sin_pp, tm, d_q)
    q_s, gates_s = _q_proj(hs, row(b_norm[j]), wq, wg, row(b_q_norm[j]), cos_s, sin_s, bd, d_q)

    w1cat = jnp.concatenate([cmp_w1[:, :CMP_STRIDE], cmp_w1[:, CMP_STRIDE:]], axis=-1).astype(BF16)
    w1cat = w1cat.reshape(2, CMP_STRIDE * HEAD_DIM, 2 * HEAD_DIM)
    w1f = cmp_w1.reshape(2, CMP_BLOCK * HEAD_DIM, HEAD_DIM).astype(BF16)
    pe8 = jnp.broadcast_to(cmp_pe.reshape(2, 1, CMP_BLOCK * HEAD_DIM),
                           (2, SUBLANES, CMP_BLOCK * HEAD_DIM)).astype(BF16)
    w2 = cmp_w2.astype(BF16)

    n_sub_p = t // CMP_STRIDE
    n_slc_p = t // SLC_BLOCK
    cos_c, sin_c = _rope_tables(jnp.arange(n_sub_p) * CMP_STRIDE + CMP_BLOCK - 1)
    kc_p, vc_p = _compress_prompt(cmp_rows_p, w1cat, pe8, w1f, w2, k_norm, cos_c, sin_c, bp, t)
    ovt = _overlap_matrix(n_sub_p - 1, n_slc_p, n_sub_p, n_slc_p).T
    expand = jnp.where(jnp.arange(t)[:, None] // SLC_BLOCK == jnp.arange(LANES)[None, :], NEG, 0.0).astype(BF16)
    o_p = _attn_prompt(q_p, gates_p, kc_p, vc_p, kvb_p, ovt, expand, bp, t)
    w_o = b_w_out[j].astype(BF16)
    hp = _out_proj(hp, o_p, w_o, tm)

    tp_s = -(-(past_len + td) // KV_ALIGN) * KV_ALIGN
    n_sub_s = tp_s // CMP_STRIDE
    n_slc_s = tp_s // SLC_BLOCK
    n_pad_s = -(-n_sub_s // SUBLANES) * SUBLANES
    sel_lanes = -(-n_slc_s // LANES) * LANES
    assert past_len // SLC_BLOCK + 1 >= N_SELECT
    cos_cs, sin_cs = _rope_tables(jnp.arange(n_pad_s) * CMP_STRIDE + CMP_BLOCK - 1)
    ov_s = _overlap_matrix(n_sub_s - 1, n_slc_s, n_pad_s, sel_lanes)
    n_paged = 4 * N_KV
    n_win = 2 * N_KV
    cache_pages = cache_kv.reshape(n_pool, page_size * n_paged, HEAD_DIM)
    q3 = q_s.reshape(bd, d_q // HEAD_DIM, HEAD_DIM)
    new_kv = paged_s.reshape(bd, n_paged, HEAD_DIM)
    new_win = jnp.pad(win_s.reshape(bd, n_win, HEAD_DIM), ((0, 0), (0, SUBLANES - n_win), (0, 0)))
    o_c, sel_idx = _sample_cmp(page_table, cache_pages, new_kv, w1cat, pe8, w1f, w2, k_norm, cos_cs,
                               sin_cs, q3, ov_s, min(16, n_pages), past_len, n_slc_s)
    blocks_per_page = page_size // SLC_BLOCK
    cache_blocks = cache_kv.reshape(n_pool * blocks_per_page, SLC_BLOCK * n_paged, HEAD_DIM)
    state3 = state_kv_win.reshape(bd, w_keep * n_win, HEAD_DIM)
    o_s = _sample_attn(page_table, sel_idx[..., 0].reshape(bd, N_KV * N_SELECT), cache_blocks, new_kv,
                       new_win, state3, q3, gates_s.reshape(bd, 1, LANES), o_c, past_len,
                       past_len // SLC_BLOCK, blocks_per_page)
    hs = _out_proj(hs, o_s.reshape(bd, d_q), w_o, bd)

    l = n_a
    hp = _mlp(hp, row(mlp_norm[l]), w_up, w_down, l, 2 * tm, 512)
    hs = _mlp(hs, row(mlp_norm[l]), w_up, w_down, l, bd, 512)

    y_p = hp.reshape(bp, t, d)
    y_s = hs.reshape(bd, td, d)
    kv_p = paged_p.reshape(bp, t, 4, N_KV, HEAD_DIM)
    win_all_p = win_p.reshape(bp, t, 2, N_KV, HEAD_DIM)
    win_new_p = win_all_p[:, t - min(WINDOW, t):]
    kv_s = paged_s.reshape(bd, td, 4, N_KV, HEAD_DIM)
    win_new_s = jnp.concatenate(
        [state_kv_win[:, td:], win_s.reshape(bd, td, 2, N_KV, HEAD_DIM)], axis=1)
    v_a_s = jnp.stack(v_rows, axis=0)
    return (y_p, y_s, kv_p, win_new_p, kv_s, win_new_s, v_a_s)
```

```python
import functools

import jax
import jax.numpy as jnp
from jax import lax
from jax.experimental import pallas as pl
from jax.experimental.pallas import tpu as pltpu

F32 = jnp.float32
BF16 = jnp.bfloat16

CHUNK = 128
A_GROUPS = 16
HEAD_DIM = 128
N_KV = 2
CMP_STRIDE = 16
CMP_BLOCK = 2 * CMP_STRIDE
SLC_BLOCK = 64
N_SELECT = 16
WINDOW = 512
N_BRANCH = 3
KV_ALIGN = 64
ROT_DIM = HEAD_DIM // 4
ROPE_THETA = 500000.0
Q_BLOCK = 128
EPS = 1e-6
NEG = -1e30
FORCE_BONUS = 1e4
LOG2_E = 1.4426950408889634

LANES = 128
SUBLANES = 8
VMEM_LIMIT_BYTES = 56 * 1024 * 1024

KV_TILE = 512
N_KINDS = 6

_NT = (((1,), (1,)), ((), ()))


def _params(*sem):
    return pltpu.CompilerParams(dimension_semantics=sem, vmem_limit_bytes=VMEM_LIMIT_BYTES)


def _rms(x, g):
    ms = jnp.mean(x * x, axis=-1, keepdims=True)
    return x * lax.rsqrt(ms + EPS) * g


def _rope(x, cos, sin):
    half = ROT_DIM // 2
    lane = lax.broadcasted_iota(jnp.int32, x.shape, 1)
    partner = jnp.where(lane < half, pltpu.roll(x, LANES - half, 1), pltpu.roll(x, half, 1))
    return x * cos + partner * sin


def _rope_tables(pos):
    inv = ROPE_THETA ** (-jnp.arange(0, ROT_DIM, 2, dtype=F32) / ROT_DIM)
    ang = pos.astype(F32)[:, None] * inv[None, :]
    c, s = jnp.cos(ang), jnp.sin(ang)
    n = pos.shape[0]
    pad = LANES - ROT_DIM
    cos = jnp.concatenate([c, c, jnp.ones((n, pad), F32)], axis=1)
    sin = jnp.concatenate([-s, s, jnp.zeros((n, pad), F32)], axis=1)
    return cos, sin


def _gmlp_in_kernel(x_ref, g_ref, w_ref, z_ref, xn_ref):
    @pl.when(pl.program_id(1) == 0)
    def _():
        xn_ref[...] = _rms(x_ref[...], g_ref[...]).astype(BF16)

    z_ref[...] = jax.nn.gelu(jnp.dot(xn_ref[...], w_ref[...], preferred_element_type=F32)).astype(z_ref.dtype)


def _gmlp_in(x, g, w, tm, tn):
    m, d = x.shape
    n = w.shape[1]
    return pl.pallas_call(
        _gmlp_in_kernel,
        grid=(m // tm, n // tn),
        in_specs=[
            pl.BlockSpec((tm, d), lambda i, j: (i, 0)),
            pl.BlockSpec((1, d), lambda i, j: (0, 0)),
            pl.BlockSpec((d, tn), lambda i, j: (0, j)),
        ],
        out_specs=pl.BlockSpec((tm, tn), lambda i, j: (i, j)),
        out_shape=jax.ShapeDtypeStruct((m, n), BF16),
        scratch_shapes=[pltpu.VMEM((tm, d), BF16)],
        compiler_params=_params("parallel", "arbitrary"),
        name="gmlp_in",
    )(x, g, w)


def _gmlp_out_kernel(x_ref, u_ref, v_ref, vg_ref, ws_ref, bt_ref, wo_ref, o_ref, vn_ref, y_ref):
    tm = x_ref.shape[0]
    vn_ref[...] = _rms(v_ref[...].astype(F32), vg_ref[...]).astype(BF16)
    row = lax.broadcasted_iota(jnp.int32, (CHUNK, CHUNK), 0)
    col = lax.broadcasted_iota(jnp.int32, (CHUNK, CHUNK), 1)
    causal = row >= col
    for g in range(A_GROUPS):
        wsg = jnp.where(causal, ws_ref[g], 0.0).astype(BF16)
        bias = bt_ref[:, g:g + 1]
        cs = slice(g * LANES, (g + 1) * LANES)
        for c in range(tm // CHUNK):
            rs = slice(c * CHUNK, (c + 1) * CHUNK)
            mixed = jnp.dot(wsg, vn_ref[rs, cs], preferred_element_type=F32) + bias
            y_ref[rs, cs] = (u_ref[rs, cs].astype(F32) * mixed).astype(BF16)
    o_ref[...] = x_ref[...] + jnp.dot(y_ref[...], wo_ref[...], preferred_element_type=F32)


def _gmlp_out(x, z, vg, ws, bt, wo, tm):
    m, d = x.shape
    return pl.pallas_call(
        _gmlp_out_kernel,
        grid=(m // tm,),
        in_specs=[
            pl.BlockSpec((tm, d), lambda i: (i, 0)),
            pl.BlockSpec((tm, d), lambda i: (i, 0)),
            pl.BlockSpec((tm, d), lambda i: (i, 1)),
            pl.BlockSpec((1, d), lambda i: (0, 0)),
            pl.BlockSpec((A_GROUPS, CHUNK, CHUNK), lambda i: (0, 0, 0)),
            pl.BlockSpec((CHUNK, A_GROUPS), lambda i: (0, 0)),
            pl.BlockSpec((d, d), lambda i: (0, 0)),
        ],
        out_specs=pl.BlockSpec((tm, d), lambda i: (i, 0)),
        out_shape=jax.ShapeDtypeStruct((m, d), F32),
        scratch_shapes=[pltpu.VMEM((tm, d), BF16), pltpu.VMEM((tm, d), BF16)],
        compiler_params=_params("parallel"),
        name="gmlp_out",
    )(x, z, z, vg, ws, bt, wo)


def _gmlp_out_single_kernel(x_ref, u_ref, v_ref, vg_ref, wd_ref, bb_ref, wo_ref, o_ref, vn_ref):
    vn = _rms(v_ref[...].astype(F32), vg_ref[...])
    vn_ref[...] = vn
    mixed = vn * wd_ref[...] + bb_ref[...]
    y = (u_ref[...].astype(F32) * mixed).astype(BF16)
    o_ref[...] = x_ref[...] + jnp.dot(y, wo_ref[...], preferred_element_type=F32)


def _gmlp_out_single(x, z, vg, wd, bb, wo):
    m, d = x.shape
    full = lambda i: (0, 0)
    return pl.pallas_call(
        _gmlp_out_single_kernel,
        grid=(1,),
        in_specs=[
            pl.BlockSpec((m, d), full),
            pl.BlockSpec((m, d), lambda i: (0, 0)),
            pl.BlockSpec((m, d), lambda i: (0, 1)),
            pl.BlockSpec((1, d), full),
            pl.BlockSpec((1, d), full),
            pl.BlockSpec((1, d), full),
            pl.BlockSpec((d, d), full),
        ],
        out_specs=[pl.BlockSpec((m, d), full), pl.BlockSpec((m, d), full)],
        out_shape=[jax.ShapeDtypeStruct((m, d), F32), jax.ShapeDtypeStruct((m, d), F32)],
        compiler_params=_params("arbitrary"),
        name="gmlp_out_single",
    )(x, z, z, vg, wd, bb, wo)


def _mlp_kernel(x_ref, g_ref, wu_ref, wd_ref, o_ref, xn_ref):
    @pl.when(pl.program_id(1) == 0)
    def _():
        x = x_ref[...]
        xn_ref[...] = _rms(x, g_ref[...]).astype(BF16)
        o_ref[...] = x

    h = jnp.dot(xn_ref[...], wu_ref[...], preferred_element_type=F32)
    a = jnp.square(jnp.maximum(h, 0.0)).astype(BF16)
    o_ref[...] += jnp.dot(a, wd_ref[...], preferred_element_type=F32)


def _mlp(x, g, wu, wd, layer, tm, tf):
    m, d = x.shape
    f = wu.shape[2]
    return pl.pallas_call(
        _mlp_kernel,
        grid=(m // tm, f // tf),
        in_specs=[
            pl.BlockSpec((tm, d), lambda i, j: (i, 0)),
            pl.BlockSpec((1, d), lambda i, j: (0, 0)),
            pl.BlockSpec((None, d, tf), lambda i, j: (layer, 0, j)),
            pl.BlockSpec((None, tf, d), lambda i, j: (layer, j, 0)),
        ],
        out_specs=pl.BlockSpec((tm, d), lambda i, j: (i, 0)),
        out_shape=jax.ShapeDtypeStruct((m, d), F32),
        scratch_shapes=[pltpu.VMEM((tm, d), BF16)],
        compiler_params=_params("parallel", "arbitrary"),
        name="mlp",
    )(x, g, wu, wd)


def _kv_proj_kernel(x_ref, g_ref, w_ref, kn_ref, cos_ref, sin_ref, paged_ref, win_ref, cmp_ref, kvb_ref):
    tm = x_ref.shape[0]
    n_paged = 4 * N_KV
    n_win = 2 * N_KV
    xn = _rms(x_ref[...], g_ref[...]).astype(BF16)
    kv = jnp.dot(xn, w_ref[...], preferred_element_type=F32)
    cos, sin = cos_ref[...], sin_ref[...]
    for c in range(N_KINDS * N_KV):
        kind = c // N_KV
        h = kv[:, c * LANES:(c + 1) * LANES]
        if kind == 2:
            h = _rope(_rms(h, kn_ref[1:2, :]), cos, sin)
        elif kind == 4:
            h = _rope(_rms(h, kn_ref[2:3, :]), cos, sin)
        if kind < 4:
            paged_ref[pl.ds(c, tm, stride=n_paged), :] = h
        else:
            win_ref[pl.ds(c - n_paged, tm, stride=n_win), :] = h
        if kind < 2:
            cmp_ref[c] = h
        else:
            kvb_ref[c - 2 * N_KV] = h.astype(BF16)


def _kv_proj(x, g, w, kn, cos, sin, tm):
    m, d = x.shape
    n = w.shape[1]
    n_paged = 4 * N_KV
    n_win = 2 * N_KV
    return pl.pallas_call(
        _kv_proj_kernel,
        grid=(m // tm,),
        in_specs=[
            pl.BlockSpec((tm, d), lambda i: (i, 0)),
            pl.BlockSpec((1, d), lambda i: (0, 0)),
            pl.BlockSpec((d, n), lambda i: (0, 0)),
            pl.BlockSpec((3, HEAD_DIM), lambda i: (0, 0)),
            pl.BlockSpec((tm, LANES), lambda i: (i, 0)),
            pl.BlockSpec((tm, LANES), lambda i: (i, 0)),
        ],
        out_specs=[
            pl.BlockSpec((tm * n_paged, HEAD_DIM), lambda i: (i, 0)),
            pl.BlockSpec((tm * n_win, HEAD_DIM), lambda i: (i, 0)),
            pl.BlockSpec((2 * N_KV, tm, HEAD_DIM), lambda i: (0, i, 0)),
            pl.BlockSpec((4 * N_KV, tm, HEAD_DIM), lambda i: (0, i, 0)),
        ],
        out_shape=[
            jax.ShapeDtypeStruct((m * n_paged, HEAD_DIM), F32),
            jax.ShapeDtypeStruct((m * n_win, HEAD_DIM), F32),
            jax.ShapeDtypeStruct((2 * N_KV, m, HEAD_DIM), F32),
            jax.ShapeDtypeStruct((4 * N_KV, m, HEAD_DIM), BF16),
        ],
        compiler_params=_params("parallel"),
        name="kv_proj",
    )(x, g, w, kn, cos, sin)


def _q_proj_kernel(x_ref, g_ref, wq_ref, wg_ref, qn_ref, cos_ref, sin_ref, q_ref, gate_ref):
    xn = _rms(x_ref[...], g_ref[...]).astype(BF16)
    z = jnp.dot(xn, wq_ref[...], preferred_element_type=F32)
    cos, sin = cos_ref[...], sin_ref[...]
    qn = qn_ref[...]
    scale = HEAD_DIM ** -0.5 * LOG2_E
    for h in range(z.shape[1] // HEAD_DIM):
        cs = slice(h * HEAD_DIM, (h + 1) * HEAD_DIM)
        qh = _rope(_rms(z[:, cs], qn), cos, sin) * scale
        q_ref[:, cs] = qh.astype(BF16)
    gate_ref[...] = jax.nn.sigmoid(jnp.dot(xn, wg_ref[...], preferred_element_type=F32))


def _q_proj(x, g, wq, wg, qn, cos, sin, tm, n):
    m, d = x.shape
    return pl.pallas_call(
        _q_proj_kernel,
        grid=(m // tm,),
        in_specs=[
            pl.BlockSpec((tm, d), lambda i: (i, 0)),
            pl.BlockSpec((1, d), lambda i: (0, 0)),
            pl.BlockSpec((d, n), lambda i: (0, 0)),
            pl.BlockSpec((d, LANES), lambda i: (0, 0)),
            pl.BlockSpec((1, HEAD_DIM), lambda i: (0, 0)),
            pl.BlockSpec((tm, LANES), lambda i: (i, 0)),
            pl.BlockSpec((tm, LANES), lambda i: (i, 0)),
        ],
        out_specs=[
            pl.BlockSpec((tm, n), lambda i: (i, 0)),
            pl.BlockSpec((tm, LANES), lambda i: (i, 0)),
        ],
        out_shape=[
            jax.ShapeDtypeStruct((m, n), BF16),
            jax.ShapeDtypeStruct((m, LANES), F32),
        ],
        compiler_params=_params("parallel"),
        name="q_proj",
    )(x, g, wq, wg, qn, cos, sin)


def _out_proj_kernel(h_ref, o_ref, w_ref, y_ref):
    y_ref[...] = h_ref[...] + jnp.dot(o_ref[...], w_ref[...], preferred_element_type=F32)


def _out_proj(h, o, w, tm):
    m, d = h.shape
    k = o.shape[1]
    return pl.pallas_call(
        _out_proj_kernel,
        grid=(m // tm,),
        in_specs=[
            pl.BlockSpec((tm, d), lambda i: (i, 0)),
            pl.BlockSpec((tm, k), lambda i: (i, 0)),
            pl.BlockSpec((k, d), lambda i: (0, 0)),
        ],
        out_specs=pl.BlockSpec((tm, d), lambda i: (i, 0)),
        out_shape=jax.ShapeDtypeStruct((m, d), F32),
        compiler_params=_params("parallel"),
        name="out_proj",
    )(h, o, w)


def _cmp_bias(pe_ref, w1f_ref, kind):
    return jnp.dot(pe_ref[kind], w1f_ref[kind], preferred_element_type=F32)[0:1, :]


def _cmp_finish(fs, bias, w2, n_rows):
    first = fs[:, :HEAD_DIM]
    second = pltpu.roll(fs[:, HEAD_DIM:], n_rows - 1, 0)
    h = first + second + bias
    h = h * jax.nn.sigmoid(h)
    return jnp.dot(h.astype(BF16), w2, preferred_element_type=F32)


def _compress_prompt_kernel(krows_ref, vrows_ref, w1_ref, pe_ref, w1f_ref, w2_ref, kn_ref,
                            cos_ref, sin_ref, kc_ref, vc_ref):
    n_sub = krows_ref.shape[0] // CMP_STRIDE
    rowi = lax.broadcasted_iota(jnp.int32, (n_sub, HEAD_DIM), 0)
    for kind, rows_ref, out_ref in ((0, krows_ref, kc_ref), (1, vrows_ref, vc_ref)):
        x = jnp.concatenate(
            [rows_ref[pl.ds(r, n_sub, stride=CMP_STRIDE), :].astype(BF16) for r in range(CMP_STRIDE)],
            axis=1)
        fs = jnp.dot(x, w1_ref[kind], preferred_element_type=F32)
        out = _cmp_finish(fs, _cmp_bias(pe_ref, w1f_ref, kind), w2_ref[kind], n_sub)
        if kind == 0:
            out = _rope(_rms(out, kn_ref[0:1, :]), cos_ref[...], sin_ref[...])
        out_ref[...] = jnp.where(rowi < n_sub - 1, out, 0.0).astype(BF16)


def _compress_prompt(paged, w1cat, pe8, w1f, w2, kn, cos, sin, b, t):
    n_sub = t // CMP_STRIDE
    full = lambda nd: (lambda i, g: (0,) * nd)
    out_spec = pl.BlockSpec((None, None, n_sub, HEAD_DIM), lambda i, g: (i, g, 0, 0))
    out_shape = jax.ShapeDtypeStruct((b, N_KV, n_sub, HEAD_DIM), BF16)
    return pl.pallas_call(
        _compress_prompt_kernel,
        grid=(b, N_KV),
        in_specs=[
            pl.BlockSpec((None, t, HEAD_DIM), lambda i, g: (g, i, 0)),
            pl.BlockSpec((None, t, HEAD_DIM), lambda i, g: (N_KV + g, i, 0)),
            pl.BlockSpec(w1cat.shape, full(3)),
            pl.BlockSpec(pe8.shape, full(3)),
            pl.BlockSpec(w1f.shape, full(3)),
            pl.BlockSpec(w2.shape, full(3)),
            pl.BlockSpec(kn.shape, full(2)),
            pl.BlockSpec(cos.shape, full(2)),
            pl.BlockSpec(sin.shape, full(2)),
        ],
        out_specs=[out_spec, out_spec],
        out_shape=[out_shape, out_shape],
        compiler_params=_params("parallel", "parallel"),
        name="compress_prompt",
    )(paged, paged, w1cat, pe8, w1f, w2, kn, cos, sin)


def _split_bf16(x):
    hi = x.astype(BF16)
    lo = (x - hi.astype(F32)).astype(BF16)
    return hi, lo


def _overlap_matrix(n_cmp, n_slc, rows, cols):
    ci = jnp.arange(rows)[:, None] * CMP_STRIDE
    sj = jnp.arange(cols)[None, :] * SLC_BLOCK
    ov = (ci < sj + SLC_BLOCK) & (ci + CMP_BLOCK > sj)
    ov = ov & (jnp.arange(rows)[:, None] < n_cmp) & (jnp.arange(cols)[None, :] < n_slc)
    return ov.astype(BF16)


def _softmax_rows(s, valid):
    s = jnp.where(valid, s, NEG)
    e = jnp.exp2(s - jnp.max(s, axis=-1, keepdims=True))
    return e / jnp.sum(e, axis=-1, keepdims=True)


def _attn_prompt_kernel(q_ref, gate_ref, kc_ref, vc_ref, ks_ref, vs_ref, kw_ref, vw_ref,
                        ovt_ref, nexp_ref, o_ref, m_ref, acc_ref, s_ref, kext_ref, ow_ref):
    qb = pl.program_id(2)
    g = pl.program_id(1)
    nq = Q_BLOCK
    rep = q_ref.shape[1] // HEAD_DIM
    n_cmp_pad = kc_ref.shape[0]
    n_slc = ovt_ref.shape[0]
    t_len = ks_ref.shape[0]
    s0 = qb * nq

    q_all = q_ref[...]
    q2 = jnp.concatenate([q_all[:, r * HEAD_DIM:(r + 1) * HEAD_DIM] for r in range(rep)], axis=0)
    t_col = s0 + lax.broadcasted_iota(jnp.int32, (nq, 1), 0)

    slab = nq + WINDOW
    w0 = pl.multiple_of(jnp.maximum(s0 - WINDOW, 0), nq)
    kw = kw_ref[pl.ds(w0, slab), :]
    vw = jnp.concatenate([vw_ref[pl.ds(w0, slab), :], jnp.ones((slab, HEAD_DIM), BF16)], axis=1)
    dist = t_col - (w0 + lax.broadcasted_iota(jnp.int32, (nq, slab), 1))
    bias_w = jnp.where((dist >= 0) & (dist < WINDOW), 0.0, NEG)
    s_w = lax.dot_general(q2, kw, _NT, preferred_element_type=F32)
    p_parts = []
    for r in range(rep):
        s_r = s_w[r * nq:(r + 1) * nq] + bias_w
        p_parts.append(jnp.exp2(s_r - jnp.max(s_r, axis=-1, keepdims=True)).astype(BF16))
    o_w = jnp.dot(jnp.concatenate(p_parts, axis=0), vw, preferred_element_type=F32)
    ow_ref[...] = o_w[:, :HEAD_DIM] / o_w[:, HEAD_DIM:]

    kc = kc_ref[...]
    vc = vc_ref[...]
    cpos = lax.broadcasted_iota(jnp.int32, (1, n_cmp_pad), 1) * CMP_STRIDE + (CMP_BLOCK - 1)
    m_c = cpos <= t_col
    s_c = lax.dot_general(q2, kc, _NT, preferred_element_type=F32)
    psum = jnp.zeros((nq, n_cmp_pad), F32)
    p_parts = []
    for r in range(rep):
        p = jnp.where(m_c, _softmax_rows(s_c[r * nq:(r + 1) * nq], m_c), 0.0)
        psum = psum + p
        p_parts.append(p.astype(BF16))
    o_c = jnp.dot(jnp.concatenate(p_parts, axis=0), vc, preferred_element_type=F32)

    ovt = ovt_ref[...]
    p_hi, p_lo = _split_bf16(psum)
    imp_t = (lax.dot_general(ovt, p_hi, _NT, preferred_element_type=F32)
             + lax.dot_general(ovt, p_lo, _NT, preferred_element_type=F32))
    t_row = s0 + lax.broadcasted_iota(jnp.int32, (n_slc, nq), 1)
    blk = lax.broadcasted_iota(jnp.int32, (n_slc, nq), 0)
    cur = t_row // SLC_BLOCK
    causal = blk * SLC_BLOCK <= t_row
    forced = (blk == 0) | (blk == cur) | (blk == cur - 1)
    score = jnp.where(causal, imp_t + jnp.where(forced, FORCE_BONUS, 0.0), NEG)
    rank = jnp.zeros((n_slc, nq), jnp.int32)
    for i in range(n_slc):
        si = score[i:i + 1, :]
        later = (blk > i).astype(jnp.int32)
        rank = rank + jnp.where(si > score, 1, 0) + jnp.where(si == score, later, 0)
    sel_t = jnp.where(causal & (rank < N_SELECT), 1.0, 0.0)
    sel_t = jnp.concatenate([sel_t, jnp.zeros((LANES - n_slc, nq), F32)], axis=0)
    unsel = 1.0 - sel_t.T

    @pl.when(qb == 0)
    def _():
        kext_ref[:, :HEAD_DIM] = ks_ref[...]
        kext_ref[:, HEAD_DIM:] = nexp_ref[...]

    blk_lane = lax.broadcasted_iota(jnp.int32, (nq, LANES), 1)
    unsel_main = jnp.where(blk_lane >= s0 // SLC_BLOCK, 1.0, unsel).astype(BF16)
    q_ext = jnp.concatenate([q2, jnp.concatenate([unsel_main] * rep, axis=0)], axis=1)
    n_main = (s0 + KV_TILE - 1) // KV_TILE
    m_ref[...] = jnp.full(m_ref.shape, NEG, F32)

    def sweep1(off, width):
        s = lax.dot_general(q_ext, kext_ref[pl.ds(off, width), :], _NT, preferred_element_type=F32)
        s_ref[:, pl.ds(off, width)] = s
        m_run = m_ref[...]
        for c in range(width // LANES):
            m_run = jnp.maximum(m_run, s[:, c * LANES:(c + 1) * LANES])
        m_ref[...] = m_run

    def max_body(j, carry):
        sweep1(pl.multiple_of(j * 2 * KV_TILE, 2 * KV_TILE), 2 * KV_TILE)
        return carry

    lax.fori_loop(0, n_main // 2, max_body, 0)

    @pl.when(n_main % 2 == 1)
    def _():
        sweep1(pl.multiple_of((n_main - 1) * KV_TILE, KV_TILE), KV_TILE)

    d0 = pl.multiple_of(s0, nq)
    q_i = lax.broadcasted_iota(jnp.int32, (nq, nq), 0)
    k_i = lax.broadcasted_iota(jnp.int32, (nq, nq), 1)
    bias_d = (lax.dot_general(unsel.astype(BF16), nexp_ref[pl.ds(d0, nq), :], _NT,
                              preferred_element_type=F32)
              + jnp.where(k_i <= q_i, 0.0, NEG))
    s_d = lax.dot_general(q2, ks_ref[pl.ds(d0, nq), :], _NT, preferred_element_type=F32)
    s_d = (s_d.reshape(rep, nq, nq) + bias_d[None]).reshape(rep * nq, nq)
    m_all = jnp.maximum(m_ref[...], s_d)
    m_b = jnp.broadcast_to(jnp.max(m_all, axis=-1, keepdims=True), m_all.shape)
    m_ref[...] = m_b
    v_d = jnp.concatenate([vs_ref[pl.ds(d0, nq), :], jnp.ones((nq, HEAD_DIM), BF16)], axis=1)
    acc_ref[...] = jnp.dot(jnp.exp2(s_d - m_b).astype(BF16), v_d, preferred_element_type=F32)

    def sweep2(off, width):
        v1 = jnp.concatenate([vs_ref[pl.ds(off, width), :], jnp.ones((width, HEAD_DIM), BF16)], axis=1)
        m_rows = m_ref[...]
        p = jnp.concatenate(
            [jnp.exp2(s_ref[:, pl.ds(pl.multiple_of(off + c * LANES, LANES), LANES)] - m_rows).astype(BF16)
             for c in range(width // LANES)], axis=1)
        acc_ref[...] += jnp.dot(p, v1, preferred_element_type=F32)

    def pv_body(j, carry):
        sweep2(pl.multiple_of(j * 2 * KV_TILE, 2 * KV_TILE), 2 * KV_TILE)
        return carry

    lax.fori_loop(0, n_main // 2, pv_body, 0)

    @pl.when(n_main % 2 == 1)
    def _():
        sweep2(pl.multiple_of((n_main - 1) * KV_TILE, KV_TILE), KV_TILE)

    o_s = acc_ref[:, :HEAD_DIM] / acc_ref[:, HEAD_DIM:]

    gates = gate_ref[...]
    for r in range(rep):
        rs = slice(r * nq, (r + 1) * nq)
        c0 = (g * rep + r) * N_BRANCH
        lane = lax.broadcasted_iota(jnp.int32, gates.shape, 1)
        gsel = [jnp.sum(jnp.where(lane == c0 + br, gates, 0.0), axis=-1, keepdims=True)
                for br in range(N_BRANCH)]
        out = gsel[0] * o_c[rs] + gsel[1] * o_s[rs] + gsel[2] * ow_ref[rs, :]
        o_ref[:, r * HEAD_DIM:(r + 1) * HEAD_DIM] = out.astype(BF16)


def _attn_prompt(q, gates, kc, vc, kvb, ovt, expand, b, t):
    nqb = t // Q_BLOCK
    rep = q.shape[1] // HEAD_DIM // N_KV
    n_cmp_pad = kc.shape[2]
    rows = lambda kind: pl.BlockSpec((None, t, HEAD_DIM), lambda i, g, j: ((kind - 2) * N_KV + g, i, 0))
    cmp_spec = pl.BlockSpec((None, None, n_cmp_pad, HEAD_DIM), lambda i, g, j: (i, g, 0, 0))
    return pl.pallas_call(
        _attn_prompt_kernel,
        grid=(b, N_KV, nqb),
        in_specs=[
            pl.BlockSpec((Q_BLOCK, rep * HEAD_DIM), lambda i, g, j: (i * nqb + j, g)),
            pl.BlockSpec((Q_BLOCK, LANES), lambda i, g, j: (i * nqb + j, 0)),
            cmp_spec, cmp_spec,
            rows(2), rows(3), rows(4), rows(5),
            pl.BlockSpec(ovt.shape, lambda i, g, j: (0, 0)),
            pl.BlockSpec(expand.shape, lambda i, g, j: (0, 0)),
        ],
        out_specs=pl.BlockSpec((Q_BLOCK, rep * HEAD_DIM), lambda i, g, j: (i * nqb + j, g)),
        out_shape=jax.ShapeDtypeStruct(q.shape, BF16),
        scratch_shapes=[
            pltpu.VMEM((rep * Q_BLOCK, LANES), F32),
            pltpu.VMEM((rep * Q_BLOCK, 2 * HEAD_DIM), F32),
            pltpu.VMEM((rep * Q_BLOCK, t), F32),
            pltpu.VMEM((t, 2 * HEAD_DIM), BF16),
            pltpu.VMEM((rep * Q_BLOCK, HEAD_DIM), F32),
        ],
        compiler_params=_params("parallel", "parallel", "arbitrary"),
        name="attn_prompt",
    )(q, gates, kc, vc, kvb, kvb, kvb, kvb, ovt, expand)


def _sample_cmp_kernel(pt_ref, *refs, pages_per_step, n_steps, t_pos, n_slc):
    del pt_ref
    page_refs = refs[:pages_per_step]
    (new_ref, w1_ref, pe_ref, w1f_ref, w2_ref, kn_ref, cos_ref, sin_ref, q_ref, ov_ref,
     oc_ref, idx_ref, fs_ref, stage_ref) = refs[pages_per_step:]
    step = pl.program_id(1)
    n_heads = N_KV * 2
    heads_per_row = 4 * N_KV
    page_rows = page_refs[0].shape[0] // heads_per_row
    sub_per_page = page_rows // CMP_STRIDE
    rows_per_step = pages_per_step * sub_per_page
    n_pad = fs_ref.shape[1]
    n_past = n_steps * rows_per_step
    row0 = pl.multiple_of(step * rows_per_step, rows_per_step)

    assert heads_per_row == SUBLANES
    slab = stage_ref.shape[0] // heads_per_row
    for k, p in enumerate(page_refs):
        for n in range(sub_per_page):
            row = k * sub_per_page + n
            for r in range(CMP_STRIDE):
                tok = p[pl.ds((n * CMP_STRIDE + r) * heads_per_row, heads_per_row), :]
                tile = (row // SUBLANES) * CMP_STRIDE + r
                stage_ref[pl.ds(tile * SUBLANES + row % SUBLANES, heads_per_row, stride=slab), :] = tok
    for c in range(n_heads):
        x = jnp.concatenate(
            [jnp.concatenate(
                [stage_ref[pl.ds(c * slab + (i * CMP_STRIDE + r) * SUBLANES, SUBLANES), :]
                 for r in range(CMP_STRIDE)], axis=1)
             for i in range(rows_per_step // SUBLANES)], axis=0)
        fs_ref[c, pl.ds(row0, rows_per_step), :] = jnp.dot(
            x.astype(BF16), w1_ref[c // N_KV], preferred_element_type=F32)

    @pl.when(step == n_steps - 1)
    def _():
        tail = n_pad - n_past
        row_t = lax.broadcasted_iota(jnp.int32, (tail, HEAD_DIM), 0)
        rowi = lax.broadcasted_iota(jnp.int32, (n_pad, HEAD_DIM), 0)
        n_sub = (t_pos + 1 + KV_ALIGN - 1) // KV_ALIGN * KV_ALIGN // CMP_STRIDE
        outs = []
        for c in range(n_heads):
            kind = c // N_KV
            new_row = new_ref[c:c + 1, :]
            x_tail = jnp.where(row_t == 0, new_row, 0.0).astype(BF16)
            fs_ref[c, n_past:n_pad, :] = jnp.dot(x_tail, w1_ref[kind, 0:HEAD_DIM, :],
                                                 preferred_element_type=F32)
            out = _cmp_finish(fs_ref[c], _cmp_bias(pe_ref, w1f_ref, kind), w2_ref[kind], n_pad)
            if kind == 0:
                out = _rope(_rms(out, kn_ref[0:1, :]), cos_ref[...], sin_ref[...])
            outs.append(jnp.where(rowi < n_sub - 1, out, 0.0).astype(BF16))

        q = q_ref[...]
        n_q = q.shape[0]
        rep = n_q // N_KV
        head = lax.broadcasted_iota(jnp.int32, (n_q, 1), 0)
        cpos = lax.broadcasted_iota(jnp.int32, (1, n_pad), 1) * CMP_STRIDE + (CMP_BLOCK - 1)
        m_c = cpos <= t_pos
        lanes = ov_ref.shape[1]
        blk = lax.broadcasted_iota(jnp.int32, (1, lanes), 1)
        cur = t_pos // SLC_BLOCK
        causal = (blk * SLC_BLOCK <= t_pos) & (blk < n_slc)
        forced = (blk == 0) | (blk == cur) | (blk == cur - 1)
        eye_i = lax.broadcasted_iota(jnp.int32, (lanes, lanes), 0)
        eye_j = lax.broadcasted_iota(jnp.int32, (lanes, lanes), 1)
        o_c = jnp.zeros((n_q, HEAD_DIM), F32)
        for g in range(N_KV):
            kc, vc = outs[g], outs[N_KV + g]
            in_group = (head // rep) == g
            s = lax.dot_general(q, kc, _NT, preferred_element_type=F32)
            p = jnp.where(m_c, _softmax_rows(s, m_c), 0.0)
            o_g = jnp.dot(p.astype(BF16), vc, preferred_element_type=F32)
            o_c = jnp.where(in_group, o_g, o_c)
            psum = jnp.sum(jnp.where(in_group, p, 0.0), axis=0, keepdims=True)
            psum8 = jnp.broadcast_to(psum, (SUBLANES, n_pad))
            p_hi, p_lo = _split_bf16(psum8)
            imp = (jnp.dot(p_hi, ov_ref[...], preferred_element_type=F32)
                   + jnp.dot(p_lo, ov_ref[...], preferred_element_type=F32))[0:1, :]
            score = jnp.where(causal, imp + jnp.where(forced, FORCE_BONUS, 0.0), NEG)
            score_b = jnp.broadcast_to(score, (lanes, lanes))
            score_col = jnp.sum(jnp.where(eye_i == eye_j, score_b, 0.0), axis=1, keepdims=True)
            beats = (score_col > score_b) | ((score_col == score_b) & (eye_i < eye_j))
            rank = jnp.sum(jnp.where(beats, 1.0, 0.0), axis=0, keepdims=True)
            sel = jnp.where(causal & (rank < N_SELECT), 1.0, 0.0)
            sel_b = jnp.broadcast_to(sel, (lanes, lanes))
            sel_col = jnp.sum(jnp.where(eye_i == eye_j, sel_b, 0.0), axis=1, keepdims=True)
            slot = jnp.sum(jnp.where(eye_i < eye_j, sel_col, 0.0), axis=0, keepdims=True)
            slot_b = jnp.broadcast_to(slot, (N_SELECT, lanes))
            k_i = lax.broadcasted_iota(jnp.int32, (N_SELECT, lanes), 0).astype(F32)
            j_i = lax.broadcasted_iota(jnp.int32, (N_SELECT, lanes), 1).astype(F32)
            hit = (slot_b == k_i) & (jnp.broadcast_to(sel, (N_SELECT, lanes)) > 0.5)
            idx = jnp.sum(jnp.where(hit, j_i, 0.0), axis=1, keepdims=True)
            idx_ref[g] = jnp.broadcast_to(idx, (N_SELECT, LANES)).astype(jnp.int32)
        oc_ref[...] = o_c


def _sample_cmp(page_table, cache_pages, new_rows, w1cat, pe8, w1f, w2, kn, cos, sin, q3, ov,
                pages_per_step, t_pos, n_slc):
    bd, n_pages = page_table.shape
    n_steps = n_pages // pages_per_step
    flat_rows = cache_pages.shape[1]
    page_rows = flat_rows // (4 * N_KV)
    n_pad = cos.shape[0]
    n_q = q3.shape[1]
    full = lambda nd: (lambda i, s, pt: (0,) * nd)

    n_pool = cache_pages.shape[0]

    def page_spec(k):
        return pl.BlockSpec(
            (None, flat_rows, HEAD_DIM),
            lambda i, s, pt: (jnp.clip(pt[i, s * pages_per_step + k], 0, n_pool - 1), 0, 0))

    grid_spec = pltpu.PrefetchScalarGridSpec(
        num_scalar_prefetch=1,
        grid=(bd, n_steps),
        in_specs=[page_spec(k) for k in range(pages_per_step)] + [
            pl.BlockSpec((None,) + new_rows.shape[1:], lambda i, s, pt: (i, 0, 0)),
            pl.BlockSpec(w1cat.shape, full(3)),
            pl.BlockSpec(pe8.shape, full(3)),
            pl.BlockSpec(w1f.shape, full(3)),
            pl.BlockSpec(w2.shape, full(3)),
            pl.BlockSpec(kn.shape, full(2)),
            pl.BlockSpec(cos.shape, full(2)),
            pl.BlockSpec(sin.shape, full(2)),
            pl.BlockSpec((None, n_q, HEAD_DIM), lambda i, s, pt: (i, 0, 0)),
            pl.BlockSpec(ov.shape, full(2)),
        ],
        out_specs=[
            pl.BlockSpec((None, n_q, HEAD_DIM), lambda i, s, pt: (i, 0, 0)),
            pl.BlockSpec((None, N_KV, N_SELECT, LANES), lambda i, s, pt: (i, 0, 0, 0)),
        ],
        scratch_shapes=[pltpu.VMEM((N_KV * 2, n_pad, 2 * HEAD_DIM), F32),
                        pltpu.VMEM((4 * N_KV * (pages_per_step * page_rows + 4), HEAD_DIM), F32)],
    )
    kern = functools.partial(_sample_cmp_kernel, pages_per_step=pages_per_step, n_steps=n_steps,
                             t_pos=t_pos, n_slc=n_slc)
    return pl.pallas_call(
        kern,
        grid_spec=grid_spec,
        out_shape=[
            jax.ShapeDtypeStruct((bd, n_q, HEAD_DIM), F32),
            jax.ShapeDtypeStruct((bd, N_KV, N_SELECT, LANES), jnp.int32),
        ],
        compiler_params=_params("parallel", "arbitrary"),
        name="sample_cmp",
    )(page_table, *([cache_pages] * pages_per_step), new_rows, w1cat, pe8, w1f, w2, kn, cos, sin, q3, ov)


def _sample_attn_kernel(pt_ref, sel_ref, *refs, t_pos, n_past_blocks):
    del pt_ref
    n_blk = N_KV * N_SELECT
    blk_refs = refs[:n_blk]
    new_kv_ref, new_win_ref, state_ref, q_ref, gate_ref, oc_ref, o_ref = refs[n_blk:]
    n_paged = 4 * N_KV
    n_win = 2 * N_KV
    b = pl.program_id(0)
    q = q_ref[...]
    qf = q.astype(F32)
    n_q = q.shape[0]
    rep = n_q // N_KV
    head = lax.broadcasted_iota(jnp.int32, (n_q, 1), 0)
    n_keys = N_SELECT * SLC_BLOCK
    lane = lax.broadcasted_iota(jnp.int32, (1, n_keys), 1)
    row_b = lax.broadcasted_iota(jnp.int32, (SLC_BLOCK, HEAD_DIM), 0)
    w_keep = state_ref.shape[0] // n_win
    o_s = jnp.zeros((n_q, HEAD_DIM), F32)
    o_w = jnp.zeros((n_q, HEAD_DIM), F32)
    for g in range(N_KV):
        in_group = (head // rep) == g
        k_head = 2 * N_KV + g
        v_head = 3 * N_KV + g
        tail_k = jnp.where(row_b == 0, new_kv_ref[k_head:k_head + 1, :], 0.0)
        tail_v = jnp.where(row_b == 0, new_kv_ref[v_head:v_head + 1, :], 0.0)
        k_parts, v_parts = [], []
        base = jnp.zeros((1, n_keys), jnp.int32)
        for k in range(N_SELECT):
            blk = sel_ref[b, g * N_SELECT + k]
            is_tail = blk >= n_past_blocks
            blk_ref = blk_refs[g * N_SELECT + k]
            k_rows = blk_ref[pl.ds(k_head, SLC_BLOCK, stride=n_paged), :]
            v_rows = blk_ref[pl.ds(v_head, SLC_BLOCK, stride=n_paged), :]
            k_parts.append(jnp.where(is_tail, tail_k, k_rows).astype(BF16))
            v_parts.append(jnp.where(is_tail, tail_v, v_rows).astype(BF16))
            base = jnp.where(lane // SLC_BLOCK == k, blk * SLC_BLOCK, base)
        keys = jnp.concatenate(k_parts, axis=0)
        vals = jnp.concatenate(v_parts, axis=0)
        tok = base + lane % SLC_BLOCK
        s = lax.dot_general(q, keys, _NT, preferred_element_type=F32)
        p = _softmax_rows(s, tok <= t_pos)
        o_s = jnp.where(in_group, jnp.dot(p.astype(BF16), vals, preferred_element_type=F32), o_s)

        kw = state_ref[pl.ds(g, w_keep, stride=n_win), :].astype(BF16)
        vw = state_ref[pl.ds(N_KV + g, w_keep, stride=n_win), :].astype(BF16)
        kw_new = new_win_ref[g:g + 1, :]
        vw_new = new_win_ref[N_KV + g:N_KV + g + 1, :]
        dist = w_keep - lax.broadcasted_iota(jnp.int32, (1, w_keep), 1)
        m_w = (dist >= 0) & (dist < WINDOW) & (t_pos - dist >= 0)
        s_w = jnp.where(m_w, lax.dot_general(q, kw, _NT, preferred_element_type=F32), NEG)
        s_new = jnp.sum(qf * kw_new, axis=-1, keepdims=True)
        m = jnp.maximum(jnp.max(s_w, axis=-1, keepdims=True), s_new)
        e_w = jnp.exp2(s_w - m)
        e_new = jnp.exp2(s_new - m)
        denom = jnp.sum(e_w, axis=-1, keepdims=True) + e_new
        num = jnp.dot(e_w.astype(BF16), vw, preferred_element_type=F32) + e_new * vw_new
        o_w = jnp.where(in_group, num / denom, o_w)

    gates = jnp.broadcast_to(gate_ref[...], (n_q, LANES))
    glane = lax.broadcasted_iota(jnp.int32, (n_q, LANES), 1)
    gsel = [jnp.sum(jnp.where(glane == head * N_BRANCH + br, gates, 0.0), axis=-1, keepdims=True)
            for br in range(N_BRANCH)]
    o_ref[...] = (gsel[0] * oc_ref[...] + gsel[1] * o_s + gsel[2] * o_w).astype(BF16)


def _sample_attn(page_table, sel_idx, cache_blocks, new_kv, new_win, state3, q3, gates3, o_c,
                 t_pos, n_past_blocks, blocks_per_page):
    bd = page_table.shape[0]
    n_q = q3.shape[1]

    n_pool = cache_blocks.shape[0] // blocks_per_page

    def blk_spec(g, k):
        def index(i, pt, sel):
            blk = jnp.clip(sel[i, g * N_SELECT + k], 0, n_past_blocks - 1)
            page = jnp.clip(pt[i, blk // blocks_per_page], 0, n_pool - 1)
            return (page * blocks_per_page + blk % blocks_per_page, 0, 0)
        return pl.BlockSpec((None,) + cache_blocks.shape[1:], index)

    blk_specs = [blk_spec(g, k) for g in range(N_KV) for k in range(N_SELECT)]
    per_b = lambda shape: pl.BlockSpec((None,) + shape, lambda i, pt, sel: (i, 0, 0))
    grid_spec = pltpu.PrefetchScalarGridSpec(
        num_scalar_prefetch=2,
        grid=(bd,),
        in_specs=blk_specs + [
            per_b(new_kv.shape[1:]),
            per_b(new_win.shape[1:]),
            per_b(state3.shape[1:]),
            per_b((n_q, HEAD_DIM)),
            per_b((1, LANES)),
            per_b((n_q, HEAD_DIM)),
        ],
        out_specs=per_b((n_q, HEAD_DIM)),
    )
    kern = functools.partial(_sample_attn_kernel, t_pos=t_pos, n_past_blocks=n_past_blocks)
    n_blk = N_KV * N_SELECT
    return pl.pallas_call(
        kern,
        grid_spec=grid_spec,
        out_shape=jax.ShapeDtypeStruct((bd, n_q, HEAD_DIM), BF16),
        compiler_params=_params("parallel"),
        name="sample_attn",
    )(page_table, sel_idx, *([cache_blocks] * n_blk), new_kv, new_win, state3, q3, gates3, o_c)


def kernel(x_prompt, x_sample, cache_kv, state_kv_win, page_table, a_norm, a_w_in, a_v_norm, a_w_s,
           a_b_s, a_w_out, mlp_norm, mlp_w_up, mlp_w_down, kv_norm, w_kv, cmp_pe, cmp_w1, cmp_w2,
           k_norm, b_norm, b_w_in, b_q_norm, b_w_out):
    bp, t, d = x_prompt.shape
    bd, td, _ = x_sample.shape
    n_pool, page_size = cache_kv.shape[:2]
    n_pages = page_table.shape[1]
    past_len = n_pages * page_size
    w_keep = state_kv_win.shape[1]
    depth = mlp_norm.shape[0]
    n_a = a_norm.shape[0]
    n_b = b_norm.shape[0]
    d_q = b_w_out.shape[1]
    assert td == 1 and n_b == 1 and depth == n_a + n_b
    assert d // A_GROUPS == LANES and t % KV_TILE == 0 and t >= Q_BLOCK + WINDOW
    assert w_keep == WINDOW and page_size % SLC_BLOCK == 0 and past_len % KV_ALIGN == 0

    row = lambda v: v.reshape(1, -1)
    hp = x_prompt.reshape(bp * t, d)
    hs = x_sample.reshape(bd, d)
    tm = 512

    w_up = mlp_w_up.astype(BF16)
    w_down = mlp_w_down.astype(BF16)

    v_rows = []
    for l in range(n_a):
        w_in = a_w_in[l].astype(BF16)
        w_out = a_w_out[l].astype(BF16)
        zp = _gmlp_in(hp, row(a_norm[l]), w_in, 2 * tm, 512)
        hp = _gmlp_out(hp, zp, row(a_v_norm[l]), a_w_s[l], a_b_s[l].T, w_out, 256)
        hp = _mlp(hp, row(mlp_norm[l]), w_up, w_down, l, 2 * tm, 512)
        zs = _gmlp_in(hs, row(a_norm[l]), w_in, bd, 512)
        wd = jnp.repeat(a_w_s[l][:, 0, 0], d // A_GROUPS).reshape(1, d)
        bb = jnp.repeat(a_b_s[l][:, 0], d // A_GROUPS).reshape(1, d)
        hs, v_s = _gmlp_out_single(hs, zs, row(a_v_norm[l]), wd, bb, w_out)
        v_rows.append(v_s.reshape(bd, td, d))
        hs = _mlp(hs, row(mlp_norm[l]), w_up, w_down, l, bd, 512)

    w_kv_b = w_kv.astype(BF16)
    cos_p, sin_p = _rope_tables(jnp.arange(t))
    cos_pp, sin_pp = jnp.tile(cos_p, (bp, 1)), jnp.tile(sin_p, (bp, 1))
    cos_s, sin_s = _rope_tables(jnp.full((bd,), past_len))
    paged_p, win_p, cmp_rows_p, kvb_p = _kv_proj(hp, row(kv_norm), w_kv_b, k_norm, cos_pp, sin_pp, tm)
    paged_s, win_s, _, _ = _kv_proj(hs, row(kv_norm), w_kv_b, k_norm, cos_s, sin_s, bd)

    j = 0
    wq = b_w_in[j].astype(BF16)
    wg = jnp.pad(b_w_in[j][:, d_q:], ((0, 0), (0, LANES - (b_w_in.shape[2] - d_q)))).astype(BF16)
    q_p, gates_p = _q_proj(hp, row(b_norm[j]), wq, wg, row(b_q_norm[j]), cos_pp, sin_pp, tm, d_q)
    q_s, gates_s = _q_proj(hs, row(b_norm[j]), wq, wg, row(b_q_norm[j]), cos_s, sin_s, bd, d_q)

    w1cat = jnp.concatenate([cmp_w1[:, :CMP_STRIDE], cmp_w1[:, CMP_STRIDE:]], axis=-1).astype(BF16)
    w1cat = w1cat.reshape(2, CMP_STRIDE * HEAD_DIM, 2 * HEAD_DIM)
    w1f = cmp_w1.reshape(2, CMP_BLOCK * HEAD_DIM, HEAD_DIM).astype(BF16)
    pe8 = jnp.broadcast_to(cmp_pe.reshape(2, 1, CMP_BLOCK * HEAD_DIM),
                           (2, SUBLANES, CMP_BLOCK * HEAD_DIM)).astype(BF16)
    w2 = cmp_w2.astype(BF16)

    n_sub_p = t // CMP_STRIDE
    n_slc_p = t // SLC_BLOCK
    cos_c, sin_c = _rope_tables(jnp.arange(n_sub_p) * CMP_STRIDE + CMP_BLOCK - 1)
    kc_p, vc_p = _compress_prompt(cmp_rows_p, w1cat, pe8, w1f, w2, k_norm, cos_c, sin_c, bp, t)
    ovt = _overlap_matrix(n_sub_p - 1, n_slc_p, n_sub_p, n_slc_p).T
    expand = jnp.where(jnp.arange(t)[:, None] // SLC_BLOCK == jnp.arange(LANES)[None, :], NEG, 0.0).astype(BF16)
    o_p = _attn_prompt(q_p, gates_p, kc_p, vc_p, kvb_p, ovt, expand, bp, t)
    w_o = b_w_out[j].astype(BF16)
    hp = _out_proj(hp, o_p, w_o, tm)

    tp_s = -(-(past_len + td) // KV_ALIGN) * KV_ALIGN
    n_sub_s = tp_s // CMP_STRIDE
    n_slc_s = tp_s // SLC_BLOCK
    n_pad_s = -(-n_sub_s // SUBLANES) * SUBLANES
    sel_lanes = -(-n_slc_s // LANES) * LANES
    assert past_len // SLC_BLOCK + 1 >= N_SELECT
    cos_cs, sin_cs = _rope_tables(jnp.arange(n_pad_s) * CMP_STRIDE + CMP_BLOCK - 1)
    ov_s = _overlap_matrix(n_sub_s - 1, n_slc_s, n_pad_s, sel_lanes)
    n_paged = 4 * N_KV
    n_win = 2 * N_KV
    cache_pages = cache_kv.reshape(n_pool, page_size * n_paged, HEAD_DIM)
    q3 = q_s.reshape(bd, d_q // HEAD_DIM, HEAD_DIM)
    new_kv = paged_s.reshape(bd, n_paged, HEAD_DIM)
    new_win = jnp.pad(win_s.reshape(bd, n_win, HEAD_DIM), ((0, 0), (0, SUBLANES - n_win), (0, 0)))
    o_c, sel_idx = _sample_cmp(page_table, cache_pages, new_kv, w1cat, pe8, w1f, w2, k_norm, cos_cs,
                               sin_cs, q3, ov_s, min(16, n_pages), past_len, n_slc_s)
    blocks_per_page = page_size // SLC_BLOCK
    cache_blocks = cache_kv.reshape(n_pool * blocks_per_page, SLC_BLOCK * n_paged, HEAD_DIM)
    state3 = state_kv_win.reshape(bd, w_keep * n_win, HEAD_DIM)
    o_s = _sample_attn(page_table, sel_idx[..., 0].reshape(bd, N_KV * N_SELECT), cache_blocks, new_kv,
                       new_win, state3, q3, gates_s.reshape(bd, 1, LANES), o_c, past_len,
                       past_len // SLC_BLOCK, blocks_per_page)
    hs = _out_proj(hs, o_s.reshape(bd, d_q), w_o, bd)

    l = n_a
    hp = _mlp(hp, row(mlp_norm[l]), w_up, w_down, l, 2 * tm, 512)
    hs = _mlp(hs, row(mlp_norm[l]), w_up, w_down, l, bd, 512)

    y_p = hp.reshape(bp, t, d)
    y_s = hs.reshape(bd, td, d)
    kv_p = paged_p.reshape(bp, t, 4, N_KV, HEAD_DIM)
    win_all_p = win_p.reshape(bp, t, 2, N_KV, HEAD_DIM)
    win_new_p = win_all_p[:, t - min(WINDOW, t):]
    kv_s = paged_s.reshape(bd, td, 4, N_KV, HEAD_DIM)
    win_new_s = jnp.concatenate(
        [state_kv_win[:, td:], win_s.reshape(bd, td, 2, N_KV, HEAD_DIM)], axis=1)
    v_a_s = jnp.stack(v_rows, axis=0)
    return (y_p, y_s, kv_p, win_new_p, kv_s, win_new_s, v_a_s)
```

```python
import functools

import jax
import jax.numpy as jnp
from jax import lax
from jax.experimental import pallas as pl
from jax.experimental.pallas import tpu as pltpu

F32 = jnp.float32
BF16 = jnp.bfloat16

CHUNK = 128
A_GROUPS = 16
HEAD_DIM = 128
N_KV = 2
CMP_STRIDE = 16
CMP_BLOCK = 2 * CMP_STRIDE
SLC_BLOCK = 64
N_SELECT = 16
WINDOW = 512
N_BRANCH = 3
KV_ALIGN = 64
ROT_DIM = HEAD_DIM // 4
ROPE_THETA = 500000.0
Q_BLOCK = 128
EPS = 1e-6
NEG = -1e30
FORCE_BONUS = 1e4
LOG2_E = 1.4426950408889634

LANES = 128
SUBLANES = 8
VMEM_LIMIT_BYTES = 56 * 1024 * 1024

KV_TILE = 512
N_KINDS = 6

_NT = (((1,), (1,)), ((), ()))


def _params(*sem):
    return pltpu.CompilerParams(dimension_semantics=sem, vmem_limit_bytes=VMEM_LIMIT_BYTES)


def _rms(x, g):
    ms = jnp.mean(x * x, axis=-1, keepdims=True)
    return x * lax.rsqrt(ms + EPS) * g


def _rope(x, cos, sin):
    half = ROT_DIM // 2
    lane = lax.broadcasted_iota(jnp.int32, x.shape, 1)
    partner = jnp.where(lane < half, pltpu.roll(x, LANES - half, 1), pltpu.roll(x, half, 1))
    return x * cos + partner * sin


def _rope_tables(pos):
    inv = ROPE_THETA ** (-jnp.arange(0, ROT_DIM, 2, dtype=F32) / ROT_DIM)
    ang = pos.astype(F32)[:, None] * inv[None, :]
    c, s = jnp.cos(ang), jnp.sin(ang)
    n = pos.shape[0]
    pad = LANES - ROT_DIM
    cos = jnp.concatenate([c, c, jnp.ones((n, pad), F32)], axis=1)
    sin = jnp.concatenate([-s, s, jnp.zeros((n, pad), F32)], axis=1)
    return cos, sin


def _gmlp_in_kernel(x_ref, g_ref, w_ref, z_ref, xn_ref):
    @pl.when(pl.program_id(1) == 0)
    def _():
        xn_ref[...] = _rms(x_ref[...], g_ref[...]).astype(BF16)

    z_ref[...] = jax.nn.gelu(jnp.dot(xn_ref[...], w_ref[...], preferred_element_type=F32)).astype(z_ref.dtype)


def _gmlp_in(x, g, w, tm, tn):
    m, d = x.shape
    n = w.shape[1]
    return pl.pallas_call(
        _gmlp_in_kernel,
        grid=(m // tm, n // tn),
        in_specs=[
            pl.BlockSpec((tm, d), lambda i, j: (i, 0)),
            pl.BlockSpec((1, d), lambda i, j: (0, 0)),
            pl.BlockSpec((d, tn), lambda i, j: (0, j)),
        ],
        out_specs=pl.BlockSpec((tm, tn), lambda i, j: (i, j)),
        out_shape=jax.ShapeDtypeStruct((m, n), BF16),
        scratch_shapes=[pltpu.VMEM((tm, d), BF16)],
        compiler_params=_params("parallel", "arbitrary"),
        name="gmlp_in",
    )(x, g, w)


def _gmlp_out_kernel(x_ref, u_ref, v_ref, vg_ref, ws_ref, bt_ref, wo_ref, o_ref, vn_ref, y_ref):
    tm = x_ref.shape[0]
    vn_ref[...] = _rms(v_ref[...].astype(F32), vg_ref[...]).astype(BF16)
    row = lax.broadcasted_iota(jnp.int32, (CHUNK, CHUNK), 0)
    col = lax.broadcasted_iota(jnp.int32, (CHUNK, CHUNK), 1)
    causal = row >= col
    for g in range(A_GROUPS):
        wsg = jnp.where(causal, ws_ref[g], 0.0).astype(BF16)
        bias = bt_ref[:, g:g + 1]
        cs = slice(g * LANES, (g + 1) * LANES)
        for c in range(tm // CHUNK):
            rs = slice(c * CHUNK, (c + 1) * CHUNK)
            mixed = jnp.dot(wsg, vn_ref[rs, cs], preferred_element_type=F32) + bias
            y_ref[rs, cs] = (u_ref[rs, cs].astype(F32) * mixed).astype(BF16)
    o_ref[...] = x_ref[...] + jnp.dot(y_ref[...], wo_ref[...], preferred_element_type=F32)


def _gmlp_out(x, z, vg, ws, bt, wo, tm):
    m, d = x.shape
    return pl.pallas_call(
        _gmlp_out_kernel,
        grid=(m // tm,),
        in_specs=[
            pl.BlockSpec((tm, d), lambda i: (i, 0)),
            pl.BlockSpec((tm, d), lambda i: (i, 0)),
            pl.BlockSpec((tm, d), lambda i: (i, 1)),
            pl.BlockSpec((1, d), lambda i: (0, 0)),
            pl.BlockSpec((A_GROUPS, CHUNK, CHUNK), lambda i: (0, 0, 0)),
            pl.BlockSpec((CHUNK, A_GROUPS), lambda i: (0, 0)),
            pl.BlockSpec((d, d), lambda i: (0, 0)),
        ],
        out_specs=pl.BlockSpec((tm, d), lambda i: (i, 0)),
        out_shape=jax.ShapeDtypeStruct((m, d), F32),
        scratch_shapes=[pltpu.VMEM((tm, d), BF16), pltpu.VMEM((tm, d), BF16)],
        compiler_params=_params("parallel"),
        name="gmlp_out",
    )(x, z, z, vg, ws, bt, wo)


def _gmlp_out_single_kernel(x_ref, u_ref, v_ref, vg_ref, wd_ref, bb_ref, wo_ref, o_ref, vn_ref):
    vn = _rms(v_ref[...].astype(F32), vg_ref[...])
    vn_ref[...] = vn
    mixed = vn * wd_ref[...] + bb_ref[...]
    y = (u_ref[...].astype(F32) * mixed).astype(BF16)
    o_ref[...] = x_ref[...] + jnp.dot(y, wo_ref[...], preferred_element_type=F32)


def _gmlp_out_single(x, z, vg, wd, bb, wo):
    m, d = x.shape
    full = lambda i: (0, 0)
    return pl.pallas_call(
        _gmlp_out_single_kernel,
        grid=(1,),
        in_specs=[
            pl.BlockSpec((m, d), full),
            pl.BlockSpec((m, d), lambda i: (0, 0)),
            pl.BlockSpec((m, d), lambda i: (0, 1)),
            pl.BlockSpec((1, d), full),
            pl.BlockSpec((1, d), full),
            pl.BlockSpec((1, d), full),
            pl.BlockSpec((d, d), full),
        ],
        out_specs=[pl.BlockSpec((m, d), full), pl.BlockSpec((m, d), full)],
        out_shape=[jax.ShapeDtypeStruct((m, d), F32), jax.ShapeDtypeStruct((m, d), F32)],
        compiler_params=_params("arbitrary"),
        name="gmlp_out_single",
    )(x, z, z, vg, wd, bb, wo)


def _mlp_kernel(x_ref, g_ref, wu_ref, wd_ref, o_ref, xn_ref):
    @pl.when(pl.program_id(1) == 0)
    def _():
        x = x_ref[...]
        xn_ref[...] = _rms(x, g_ref[...]).astype(BF16)
        o_ref[...] = x

    h = jnp.dot(xn_ref[...], wu_ref[...], preferred_element_type=F32)
    a = jnp.square(jnp.maximum(h, 0.0)).astype(BF16)
    o_ref[...] += jnp.dot(a, wd_ref[...], preferred_element_type=F32)


def _mlp(x, g, wu, wd, tm, tf):
    m, d = x.shape
    f = wu.shape[1]
    return pl.pallas_call(
        _mlp_kernel,
        grid=(m // tm, f // tf),
        in_specs=[
            pl.BlockSpec((tm, d), lambda i, j: (i, 0)),
            pl.BlockSpec((1, d), lambda i, j: (0, 0)),
            pl.BlockSpec((d, tf), lambda i, j: (0, j)),
            pl.BlockSpec((tf, d), lambda i, j: (j, 0)),
        ],
        out_specs=pl.BlockSpec((tm, d), lambda i, j: (i, 0)),
        out_shape=jax.ShapeDtypeStruct((m, d), F32),
        scratch_shapes=[pltpu.VMEM((tm, d), BF16)],
        compiler_params=_params("parallel", "arbitrary"),
        name="mlp",
    )(x, g, wu, wd)


def _mlp_cast_kernel(x_ref, g_ref, wu_ref, wd_ref, o_ref, wub_ref, wdb_ref, xn_ref):
    @pl.when(pl.program_id(0) == 0)
    def _():
        x = x_ref[...]
        xn_ref[...] = _rms(x, g_ref[...]).astype(BF16)
        o_ref[...] = x

    wu = wu_ref[...].astype(BF16)
    wd = wd_ref[...].astype(BF16)
    wub_ref[...] = wu
    wdb_ref[...] = wd
    h = jnp.dot(xn_ref[...], wu, preferred_element_type=F32)
    a = jnp.square(jnp.maximum(h, 0.0)).astype(BF16)
    o_ref[...] += jnp.dot(a, wd, preferred_element_type=F32)


def _mlp_cast(x, g, wu_all, wd_all, layer, tf):
    m, d = x.shape
    f = wu_all.shape[2]
    return pl.pallas_call(
        _mlp_cast_kernel,
        grid=(f // tf,),
        in_specs=[
            pl.BlockSpec((m, d), lambda j: (0, 0)),
            pl.BlockSpec((1, d), lambda j: (0, 0)),
            pl.BlockSpec((None, d, tf), lambda j: (layer, 0, j)),
            pl.BlockSpec((None, tf, d), lambda j: (layer, j, 0)),
        ],
        out_specs=[
            pl.BlockSpec((m, d), lambda j: (0, 0)),
            pl.BlockSpec((d, tf), lambda j: (0, j)),
            pl.BlockSpec((tf, d), lambda j: (j, 0)),
        ],
        out_shape=[
            jax.ShapeDtypeStruct((m, d), F32),
            jax.ShapeDtypeStruct((d, f), BF16),
            jax.ShapeDtypeStruct((f, d), BF16),
        ],
        scratch_shapes=[pltpu.VMEM((m, d), BF16)],
        compiler_params=_params("arbitrary"),
        name="mlp_cast",
    )(x, g, wu_all, wd_all)


def _kv_proj_kernel(x_ref, g_ref, w_ref, kn_ref, cos_ref, sin_ref, paged_ref, win_ref, cmp_ref, kvb_ref):
    tm = x_ref.shape[0]
    n_paged = 4 * N_KV
    n_win = 2 * N_KV
    xn = _rms(x_ref[...], g_ref[...]).astype(BF16)
    kv = jnp.dot(xn, w_ref[...], preferred_element_type=F32)
    cos, sin = cos_ref[...], sin_ref[...]
    for c in range(N_KINDS * N_KV):
        kind = c // N_KV
        h = kv[:, c * LANES:(c + 1) * LANES]
        if kind == 2:
            h = _rope(_rms(h, kn_ref[1:2, :]), cos, sin)
        elif kind == 4:
            h = _rope(_rms(h, kn_ref[2:3, :]), cos, sin)
        if kind < 4:
            paged_ref[pl.ds(c, tm, stride=n_paged), :] = h
        else:
            win_ref[pl.ds(c - n_paged, tm, stride=n_win), :] = h
        if kind < 2:
            cmp_ref[c] = h
        else:
            kvb_ref[c - 2 * N_KV] = h.astype(BF16)


def _kv_proj(x, g, w, kn, cos, sin, tm):
    m, d = x.shape
    n = w.shape[1]
    n_paged = 4 * N_KV
    n_win = 2 * N_KV
    return pl.pallas_call(
        _kv_proj_kernel,
        grid=(m // tm,),
        in_specs=[
            pl.BlockSpec((tm, d), lambda i: (i, 0)),
            pl.BlockSpec((1, d), lambda i: (0, 0)),
            pl.BlockSpec((d, n), lambda i: (0, 0)),
            pl.BlockSpec((3, HEAD_DIM), lambda i: (0, 0)),
            pl.BlockSpec((tm, LANES), lambda i: (i, 0)),
            pl.BlockSpec((tm, LANES), lambda i: (i, 0)),
        ],
        out_specs=[
            pl.BlockSpec((tm * n_paged, HEAD_DIM), lambda i: (i, 0)),
            pl.BlockSpec((tm * n_win, HEAD_DIM), lambda i: (i, 0)),
            pl.BlockSpec((2 * N_KV, tm, HEAD_DIM), lambda i: (0, i, 0)),
            pl.BlockSpec((4 * N_KV, tm, HEAD_DIM), lambda i: (0, i, 0)),
        ],
        out_shape=[
            jax.ShapeDtypeStruct((m * n_paged, HEAD_DIM), F32),
            jax.ShapeDtypeStruct((m * n_win, HEAD_DIM), F32),
            jax.ShapeDtypeStruct((2 * N_KV, m, HEAD_DIM), F32),
            jax.ShapeDtypeStruct((4 * N_KV, m, HEAD_DIM), BF16),
        ],
        compiler_params=_params("parallel"),
        name="kv_proj",
    )(x, g, w, kn, cos, sin)


def _q_proj_kernel(x_ref, g_ref, wq_ref, wg_ref, qn_ref, cos_ref, sin_ref, q_ref, gate_ref):
    xn = _rms(x_ref[...], g_ref[...]).astype(BF16)
    z = jnp.dot(xn, wq_ref[...], preferred_element_type=F32)
    cos, sin = cos_ref[...], sin_ref[...]
    qn = qn_ref[...]
    scale = HEAD_DIM ** -0.5 * LOG2_E
    for h in range(z.shape[1] // HEAD_DIM):
        cs = slice(h * HEAD_DIM, (h + 1) * HEAD_DIM)
        qh = _rope(_rms(z[:, cs], qn), cos, sin) * scale
        q_ref[:, cs] = qh.astype(BF16)
    gate_ref[...] = jax.nn.sigmoid(jnp.dot(xn, wg_ref[...], preferred_element_type=F32))


def _q_proj(x, g, wq, wg, qn, cos, sin, tm, n):
    m, d = x.shape
    return pl.pallas_call(
        _q_proj_kernel,
        grid=(m // tm,),
        in_specs=[
            pl.BlockSpec((tm, d), lambda i: (i, 0)),
            pl.BlockSpec((1, d), lambda i: (0, 0)),
            pl.BlockSpec((d, n), lambda i: (0, 0)),
            pl.BlockSpec((d, LANES), lambda i: (0, 0)),
            pl.BlockSpec((1, HEAD_DIM), lambda i: (0, 0)),
            pl.BlockSpec((tm, LANES), lambda i: (i, 0)),
            pl.BlockSpec((tm, LANES), lambda i: (i, 0)),
        ],
        out_specs=[
            pl.BlockSpec((tm, n), lambda i: (i, 0)),
            pl.BlockSpec((tm, LANES), lambda i: (i, 0)),
        ],
        out_shape=[
            jax.ShapeDtypeStruct((m, n), BF16),
            jax.ShapeDtypeStruct((m, LANES), F32),
        ],
        compiler_params=_params("parallel"),
        name="q_proj",
    )(x, g, wq, wg, qn, cos, sin)


def _out_proj_kernel(h_ref, o_ref, w_ref, y_ref):
    y_ref[...] = h_ref[...] + jnp.dot(o_ref[...], w_ref[...], preferred_element_type=F32)


def _out_proj(h, o, w, tm):
    m, d = h.shape
    k = o.shape[1]
    return pl.pallas_call(
        _out_proj_kernel,
        grid=(m // tm,),
        in_specs=[
            pl.BlockSpec((tm, d), lambda i: (i, 0)),
            pl.BlockSpec((tm, k), lambda i: (i, 0)),
            pl.BlockSpec((k, d), lambda i: (0, 0)),
        ],
        out_specs=pl.BlockSpec((tm, d), lambda i: (i, 0)),
        out_shape=jax.ShapeDtypeStruct((m, d), F32),
        compiler_params=_params("parallel"),
        name="out_proj",
    )(h, o, w)


def _cmp_bias(pe_ref, w1f_ref, kind):
    return jnp.dot(pe_ref[kind], w1f_ref[kind], preferred_element_type=F32)[0:1, :]


def _cmp_finish(fs, bias, w2, n_rows):
    first = fs[:, :HEAD_DIM]
    second = pltpu.roll(fs[:, HEAD_DIM:], n_rows - 1, 0)
    h = first + second + bias
    h = h * jax.nn.sigmoid(h)
    return jnp.dot(h.astype(BF16), w2, preferred_element_type=F32)


def _compress_prompt_kernel(krows_ref, vrows_ref, w1_ref, pe_ref, w1f_ref, w2_ref, kn_ref,
                            cos_ref, sin_ref, kc_ref, vc_ref):
    n_sub = krows_ref.shape[0] // CMP_STRIDE
    rowi = lax.broadcasted_iota(jnp.int32, (n_sub, HEAD_DIM), 0)
    for kind, rows_ref, out_ref in ((0, krows_ref, kc_ref), (1, vrows_ref, vc_ref)):
        x = jnp.concatenate(
            [rows_ref[pl.ds(r, n_sub, stride=CMP_STRIDE), :].astype(BF16) for r in range(CMP_STRIDE)],
            axis=1)
        fs = jnp.dot(x, w1_ref[kind], preferred_element_type=F32)
        out = _cmp_finish(fs, _cmp_bias(pe_ref, w1f_ref, kind), w2_ref[kind], n_sub)
        if kind == 0:
            out = _rope(_rms(out, kn_ref[0:1, :]), cos_ref[...], sin_ref[...])
        out_ref[...] = jnp.where(rowi < n_sub - 1, out, 0.0).astype(BF16)


def _compress_prompt(paged, w1cat, pe8, w1f, w2, kn, cos, sin, b, t):
    n_sub = t // CMP_STRIDE
    full = lambda nd: (lambda i, g: (0,) * nd)
    out_spec = pl.BlockSpec((None, None, n_sub, HEAD_DIM), lambda i, g: (i, g, 0, 0))
    out_shape = jax.ShapeDtypeStruct((b, N_KV, n_sub, HEAD_DIM), BF16)
    return pl.pallas_call(
        _compress_prompt_kernel,
        grid=(b, N_KV),
        in_specs=[
            pl.BlockSpec((None, t, HEAD_DIM), lambda i, g: (g, i, 0)),
            pl.BlockSpec((None, t, HEAD_DIM), lambda i, g: (N_KV + g, i, 0)),
            pl.BlockSpec(w1cat.shape, full(3)),
            pl.BlockSpec(pe8.shape, full(3)),
            pl.BlockSpec(w1f.shape, full(3)),
            pl.BlockSpec(w2.shape, full(3)),
            pl.BlockSpec(kn.shape, full(2)),
            pl.BlockSpec(cos.shape, full(2)),
            pl.BlockSpec(sin.shape, full(2)),
        ],
        out_specs=[out_spec, out_spec],
        out_shape=[out_shape, out_shape],
        compiler_params=_params("parallel", "parallel"),
        name="compress_prompt",
    )(paged, paged, w1cat, pe8, w1f, w2, kn, cos, sin)


def _split_bf16(x):
    hi = x.astype(BF16)
    lo = (x - hi.astype(F32)).astype(BF16)
    return hi, lo


def _overlap_matrix(n_cmp, n_slc, rows, cols):
    ci = jnp.arange(rows)[:, None] * CMP_STRIDE
    sj = jnp.arange(cols)[None, :] * SLC_BLOCK
    ov = (ci < sj + SLC_BLOCK) & (ci + CMP_BLOCK > sj)
    ov = ov & (jnp.arange(rows)[:, None] < n_cmp) & (jnp.arange(cols)[None, :] < n_slc)
    return ov.astype(BF16)


def _softmax_rows(s, valid):
    s = jnp.where(valid, s, NEG)
    e = jnp.exp2(s - jnp.max(s, axis=-1, keepdims=True))
    return e / jnp.sum(e, axis=-1, keepdims=True)


def _attn_prompt_kernel(q_ref, gate_ref, kc_ref, vc_ref, ks_ref, vs_ref, kw_ref, vw_ref,
                        ovt_ref, nexp_ref, o_ref, m_ref, acc_ref, s_ref, kext_ref, ow_ref, gate_b_ref):
    qb = pl.program_id(2)
    g = pl.program_id(1)
    nq = Q_BLOCK
    rep = q_ref.shape[1] // HEAD_DIM
    n_cmp_pad = kc_ref.shape[0]
    n_slc = ovt_ref.shape[0]
    t_len = ks_ref.shape[0]
    s0 = qb * nq

    q_all = q_ref[...]
    q2 = jnp.concatenate([q_all[:, r * HEAD_DIM:(r + 1) * HEAD_DIM] for r in range(rep)], axis=0)
    t_col = s0 + lax.broadcasted_iota(jnp.int32, (nq, 1), 0)

    gates = gate_ref[...]
    g_lane = lax.broadcasted_iota(jnp.int32, gates.shape, 1)
    for r in range(rep):
        for br in range(N_BRANCH):
            col = (g * rep + r) * N_BRANCH + br
            picked = jnp.sum(jnp.where(g_lane == col, gates, 0.0), axis=-1, keepdims=True)
            gate_b_ref[r * N_BRANCH + br] = jnp.broadcast_to(picked, (nq, LANES))

    slab = nq + WINDOW
    w0 = pl.multiple_of(jnp.maximum(s0 - WINDOW, 0), nq)
    kw = kw_ref[pl.ds(w0, slab), :]
    vw = jnp.concatenate([vw_ref[pl.ds(w0, slab), :], jnp.ones((slab, HEAD_DIM), BF16)], axis=1)
    dist = t_col - (w0 + lax.broadcasted_iota(jnp.int32, (nq, slab), 1))
    bias_w = jnp.where((dist >= 0) & (dist < WINDOW), 0.0, NEG)
    s_w = lax.dot_general(q2, kw, _NT, preferred_element_type=F32)
    p_parts = []
    for r in range(rep):
        s_r = s_w[r * nq:(r + 1) * nq] + bias_w
        p_parts.append(jnp.exp2(s_r - jnp.max(s_r, axis=-1, keepdims=True)).astype(BF16))
    o_w = jnp.dot(jnp.concatenate(p_parts, axis=0), vw, preferred_element_type=F32)
    ow_ref[...] = o_w[:, :HEAD_DIM] / o_w[:, HEAD_DIM:]

    kc = kc_ref[...]
    vc = vc_ref[...]
    cpos = lax.broadcasted_iota(jnp.int32, (1, n_cmp_pad), 1) * CMP_STRIDE + (CMP_BLOCK - 1)
    m_c = cpos <= t_col
    s_c = lax.dot_general(q2, kc, _NT, preferred_element_type=F32)
    psum = jnp.zeros((nq, n_cmp_pad), F32)
    p_parts = []
    for r in range(rep):
        p = jnp.where(m_c, _softmax_rows(s_c[r * nq:(r + 1) * nq], m_c), 0.0)
        psum = psum + p
        p_parts.append(p.astype(BF16))
    o_c = jnp.dot(jnp.concatenate(p_parts, axis=0), vc, preferred_element_type=F32)

    ovt = ovt_ref[...]
    p_hi, p_lo = _split_bf16(psum)
    imp_t = (lax.dot_general(ovt, p_hi, _NT, preferred_element_type=F32)
             + lax.dot_general(ovt, p_lo, _NT, preferred_element_type=F32))
    t_row = s0 + lax.broadcasted_iota(jnp.int32, (n_slc, nq), 1)
    blk = lax.broadcasted_iota(jnp.int32, (n_slc, nq), 0)
    cur = t_row // SLC_BLOCK
    causal = blk * SLC_BLOCK <= t_row
    forced = (blk == 0) | (blk == cur) | (blk == cur - 1)
    score = jnp.where(causal, imp_t + jnp.where(forced, FORCE_BONUS, 0.0), NEG)
    rank = jnp.zeros((n_slc, nq), jnp.int32)
    for i in range(n_slc):
        si = score[i:i + 1, :]
        later = (blk > i).astype(jnp.int32)
        rank = rank + jnp.where(si > score, 1, 0) + jnp.where(si == score, later, 0)
    sel_t = jnp.where(causal & (rank < N_SELECT), 1.0, 0.0)
    sel_t = jnp.concatenate([sel_t, jnp.zeros((LANES - n_slc, nq), F32)], axis=0)
    unsel = 1.0 - sel_t.T

    @pl.when(qb == 0)
    def _():
        kext_ref[:, :HEAD_DIM] = ks_ref[...]
        kext_ref[:, HEAD_DIM:] = nexp_ref[...]

    blk_lane = lax.broadcasted_iota(jnp.int32, (nq, LANES), 1)
    unsel_main = jnp.where(blk_lane >= s0 // SLC_BLOCK, 1.0, unsel).astype(BF16)
    q_ext = jnp.concatenate([q2, jnp.concatenate([unsel_main] * rep, axis=0)], axis=1)
    n_main = (s0 + KV_TILE - 1) // KV_TILE
    m_ref[...] = jnp.full(m_ref.shape, NEG, F32)

    def sweep1(off, width):
        s = lax.dot_general(q_ext, kext_ref[pl.ds(off, width), :], _NT, preferred_element_type=F32)
        s_ref[:, pl.ds(off, width)] = s
        m_run = m_ref[...]
        for c in range(width // LANES):
            m_run = jnp.maximum(m_run, s[:, c * LANES:(c + 1) * LANES])
        m_ref[...] = m_run

    def max_body(j, carry):
        sweep1(pl.multiple_of(j * 2 * KV_TILE, 2 * KV_TILE), 2 * KV_TILE)
        return carry

    lax.fori_loop(0, n_main // 2, max_body, 0)

    @pl.when(n_main % 2 == 1)
    def _():
        sweep1(pl.multiple_of((n_main - 1) * KV_TILE, KV_TILE), KV_TILE)

    d0 = pl.multiple_of(s0, nq)
    q_i = lax.broadcasted_iota(jnp.int32, (nq, nq), 0)
    k_i = lax.broadcasted_iota(jnp.int32, (nq, nq), 1)
    bias_d = (lax.dot_general(unsel.astype(BF16), nexp_ref[pl.ds(d0, nq), :], _NT,
                              preferred_element_type=F32)
              + jnp.where(k_i <= q_i, 0.0, NEG))
    s_d = lax.dot_general(q2, ks_ref[pl.ds(d0, nq), :], _NT, preferred_element_type=F32)
    s_d = (s_d.reshape(rep, nq, nq) + bias_d[None]).reshape(rep * nq, nq)
    m_all = jnp.maximum(m_ref[...], s_d)
    m_b = jnp.broadcast_to(jnp.max(m_all, axis=-1, keepdims=True), m_all.shape)
    m_ref[...] = m_b
    v_d = jnp.concatenate([vs_ref[pl.ds(d0, nq), :], jnp.ones((nq, HEAD_DIM), BF16)], axis=1)
    acc_ref[...] = jnp.dot(jnp.exp2(s_d - m_b).astype(BF16), v_d, preferred_element_type=F32)

    def sweep2(off, width):
        v1 = jnp.concatenate([vs_ref[pl.ds(off, width), :], jnp.ones((width, HEAD_DIM), BF16)], axis=1)
        m_rows = m_ref[...]
        p = jnp.concatenate(
            [jnp.exp2(s_ref[:, pl.ds(pl.multiple_of(off + c * LANES, LANES), LANES)] - m_rows).astype(BF16)
             for c in range(width // LANES)], axis=1)
        acc_ref[...] += jnp.dot(p, v1, preferred_element_type=F32)

    def pv_body(j, carry):
        sweep2(pl.multiple_of(j * 2 * KV_TILE, 2 * KV_TILE), 2 * KV_TILE)
        return carry

    lax.fori_loop(0, n_main // 2, pv_body, 0)

    @pl.when(n_main % 2 == 1)
    def _():
        sweep2(pl.multiple_of((n_main - 1) * KV_TILE, KV_TILE), KV_TILE)

    o_s = acc_ref[:, :HEAD_DIM] / acc_ref[:, HEAD_DIM:]

    for r in range(rep):
        rs = slice(r * nq, (r + 1) * nq)
        out = (gate_b_ref[r * N_BRANCH] * o_c[rs] + gate_b_ref[r * N_BRANCH + 1] * o_s[rs]
               + gate_b_ref[r * N_BRANCH + 2] * ow_ref[rs, :])
        o_ref[:, r * HEAD_DIM:(r + 1) * HEAD_DIM] = out.astype(BF16)


def _attn_prompt(q, gates, kc, vc, kvb, ovt, expand, b, t):
    nqb = t // Q_BLOCK
    rep = q.shape[1] // HEAD_DIM // N_KV
    n_cmp_pad = kc.shape[2]
    rows = lambda kind: pl.BlockSpec((None, t, HEAD_DIM), lambda i, g, j: ((kind - 2) * N_KV + g, i, 0))
    cmp_spec = pl.BlockSpec((None, None, n_cmp_pad, HEAD_DIM), lambda i, g, j: (i, g, 0, 0))
    return pl.pallas_call(
        _attn_prompt_kernel,
        grid=(b, N_KV, nqb),
        in_specs=[
            pl.BlockSpec((Q_BLOCK, rep * HEAD_DIM), lambda i, g, j: (i * nqb + j, g)),
            pl.BlockSpec((Q_BLOCK, LANES), lambda i, g, j: (i * nqb + j, 0)),
            cmp_spec, cmp_spec,
            rows(2), rows(3), rows(4), rows(5),
            pl.BlockSpec(ovt.shape, lambda i, g, j: (0, 0)),
            pl.BlockSpec(expand.shape, lambda i, g, j: (0, 0)),
        ],
        out_specs=pl.BlockSpec((Q_BLOCK, rep * HEAD_DIM), lambda i, g, j: (i * nqb + j, g)),
        out_shape=jax.ShapeDtypeStruct(q.shape, BF16),
        scratch_shapes=[
            pltpu.VMEM((rep * Q_BLOCK, LANES), F32),
            pltpu.VMEM((rep * Q_BLOCK, 2 * HEAD_DIM), F32),
            pltpu.VMEM((rep * Q_BLOCK, t), F32),
            pltpu.VMEM((t, 2 * HEAD_DIM), BF16),
            pltpu.VMEM((rep * Q_BLOCK, HEAD_DIM), F32),
            pltpu.VMEM((rep * N_BRANCH, Q_BLOCK, LANES), F32),
        ],
        compiler_params=_params("parallel", "parallel", "arbitrary"),
        name="attn_prompt",
    )(q, gates, kc, vc, kvb, kvb, kvb, kvb, ovt, expand)


def _sample_cmp_kernel(pt_ref, *refs, pages_per_step, n_steps, t_pos, n_slc):
    del pt_ref
    page_refs = refs[:pages_per_step]
    (new_ref, w1_ref, pe_ref, w1f_ref, w2_ref, kn_ref, cos_ref, sin_ref, q_ref, ov_ref,
     oc_ref, idx_ref, fs_ref, stage_ref) = refs[pages_per_step:]
    step = pl.program_id(1)
    n_heads = N_KV * 2
    heads_per_row = 4 * N_KV
    page_rows = page_refs[0].shape[0] // heads_per_row
    sub_per_page = page_rows // CMP_STRIDE
    rows_per_step = pages_per_step * sub_per_page
    n_pad = fs_ref.shape[1]
    n_past = n_steps * rows_per_step
    row0 = pl.multiple_of(step * rows_per_step, rows_per_step)

    assert heads_per_row == SUBLANES
    slab = stage_ref.shape[0] // heads_per_row
    for k, p in enumerate(page_refs):
        for n in range(sub_per_page):
            row = k * sub_per_page + n
            for r in range(CMP_STRIDE):
                tok = p[pl.ds((n * CMP_STRIDE + r) * heads_per_row, heads_per_row), :]
                tile = (row // SUBLANES) * CMP_STRIDE + r
                stage_ref[pl.ds(tile * SUBLANES + row % SUBLANES, heads_per_row, stride=slab), :] = tok
    for c in range(n_heads):
        x = jnp.concatenate(
            [jnp.concatenate(
                [stage_ref[pl.ds(c * slab + (i * CMP_STRIDE + r) * SUBLANES, SUBLANES), :]
                 for r in range(CMP_STRIDE)], axis=1)
             for i in range(rows_per_step // SUBLANES)], axis=0)
        fs_ref[c, pl.ds(row0, rows_per_step), :] = jnp.dot(
            x.astype(BF16), w1_ref[c // N_KV], preferred_element_type=F32)

    @pl.when(step == n_steps - 1)
    def _():
        tail = n_pad - n_past
        row_t = lax.broadcasted_iota(jnp.int32, (tail, HEAD_DIM), 0)
        rowi = lax.broadcasted_iota(jnp.int32, (n_pad, HEAD_DIM), 0)
        n_sub = (t_pos + 1 + KV_ALIGN - 1) // KV_ALIGN * KV_ALIGN // CMP_STRIDE
        outs = []
        for c in range(n_heads):
            kind = c // N_KV
            new_row = new_ref[c:c + 1, :]
            x_tail = jnp.where(row_t == 0, new_row, 0.0).astype(BF16)
            fs_ref[c, n_past:n_pad, :] = jnp.dot(x_tail, w1_ref[kind, 0:HEAD_DIM, :],
                                                 preferred_element_type=F32)
            out = _cmp_finish(fs_ref[c], _cmp_bias(pe_ref, w1f_ref, kind), w2_ref[kind], n_pad)
            if kind == 0:
                out = _rope(_rms(out, kn_ref[0:1, :]), cos_ref[...], sin_ref[...])
            outs.append(jnp.where(rowi < n_sub - 1, out, 0.0).astype(BF16))

        q = q_ref[...]
        n_q = q.shape[0]
        rep = n_q // N_KV
        head = lax.broadcasted_iota(jnp.int32, (n_q, 1), 0)
        cpos = lax.broadcasted_iota(jnp.int32, (1, n_pad), 1) * CMP_STRIDE + (CMP_BLOCK - 1)
        m_c = cpos <= t_pos
        lanes = ov_ref.shape[1]
        blk = lax.broadcasted_iota(jnp.int32, (1, lanes), 1)
        cur = t_pos // SLC_BLOCK
        causal = (blk * SLC_BLOCK <= t_pos) & (blk < n_slc)
        forced = (blk == 0) | (blk == cur) | (blk == cur - 1)
        eye_i = lax.broadcasted_iota(jnp.int32, (lanes, lanes), 0)
        eye_j = lax.broadcasted_iota(jnp.int32, (lanes, lanes), 1)
        o_c = jnp.zeros((n_q, HEAD_DIM), F32)
        for g in range(N_KV):
            kc, vc = outs[g], outs[N_KV + g]
            in_group = (head // rep) == g
            s = lax.dot_general(q, kc, _NT, preferred_element_type=F32)
            p = jnp.where(m_c, _softmax_rows(s, m_c), 0.0)
            o_g = jnp.dot(p.astype(BF16), vc, preferred_element_type=F32)
            o_c = jnp.where(in_group, o_g, o_c)
            psum = jnp.sum(jnp.where(in_group, p, 0.0), axis=0, keepdims=True)
            psum8 = jnp.broadcast_to(psum, (SUBLANES, n_pad))
            p_hi, p_lo = _split_bf16(psum8)
            imp = (jnp.dot(p_hi, ov_ref[...], preferred_element_type=F32)
                   + jnp.dot(p_lo, ov_ref[...], preferred_element_type=F32))[0:1, :]
            score = jnp.where(causal, imp + jnp.where(forced, FORCE_BONUS, 0.0), NEG)
            score_b = jnp.broadcast_to(score, (lanes, lanes))
            score_col = jnp.sum(jnp.where(eye_i == eye_j, score_b, 0.0), axis=1, keepdims=True)
            beats = (score_col > score_b) | ((score_col == score_b) & (eye_i < eye_j))
            rank = jnp.sum(jnp.where(beats, 1.0, 0.0), axis=0, keepdims=True)
            sel = jnp.where(causal & (rank < N_SELECT), 1.0, 0.0)
            sel_b = jnp.broadcast_to(sel, (lanes, lanes))
            sel_col = jnp.sum(jnp.where(eye_i == eye_j, sel_b, 0.0), axis=1, keepdims=True)
            slot = jnp.sum(jnp.where(eye_i < eye_j, sel_col, 0.0), axis=0, keepdims=True)
            slot_b = jnp.broadcast_to(slot, (N_SELECT, lanes))
            k_i = lax.broadcasted_iota(jnp.int32, (N_SELECT, lanes), 0).astype(F32)
            j_i = lax.broadcasted_iota(jnp.int32, (N_SELECT, lanes), 1).astype(F32)
            hit = (slot_b == k_i) & (jnp.broadcast_to(sel, (N_SELECT, lanes)) > 0.5)
            idx = jnp.sum(jnp.where(hit, j_i, 0.0), axis=1, keepdims=True)
            idx_ref[g] = jnp.broadcast_to(idx, (N_SELECT, LANES)).astype(jnp.int32)
        oc_ref[...] = o_c


def _sample_cmp(page_table, cache_pages, new_rows, w1cat, pe8, w1f, w2, kn, cos, sin, q3, ov,
                pages_per_step, t_pos, n_slc):
    bd, n_pages = page_table.shape
    n_steps = n_pages // pages_per_step
    flat_rows = cache_pages.shape[1]
    page_rows = flat_rows // (4 * N_KV)
    n_pad = cos.shape[0]
    n_q = q3.shape[1]
    full = lambda nd: (lambda i, s, pt: (0,) * nd)

    n_pool = cache_pages.shape[0]

    def page_spec(k):
        return pl.BlockSpec(
            (None, flat_rows, HEAD_DIM),
            lambda i, s, pt: (jnp.clip(pt[i, s * pages_per_step + k], 0, n_pool - 1), 0, 0))

    grid_spec = pltpu.PrefetchScalarGridSpec(
        num_scalar_prefetch=1,
        grid=(bd, n_steps),
        in_specs=[page_spec(k) for k in range(pages_per_step)] + [
            pl.BlockSpec((None,) + new_rows.shape[1:], lambda i, s, pt: (i, 0, 0)),
            pl.BlockSpec(w1cat.shape, full(3)),
            pl.BlockSpec(pe8.shape, full(3)),
            pl.BlockSpec(w1f.shape, full(3)),
            pl.BlockSpec(w2.shape, full(3)),
            pl.BlockSpec(kn.shape, full(2)),
            pl.BlockSpec(cos.shape, full(2)),
            pl.BlockSpec(sin.shape, full(2)),
            pl.BlockSpec((None, n_q, HEAD_DIM), lambda i, s, pt: (i, 0, 0)),
            pl.BlockSpec(ov.shape, full(2)),
        ],
        out_specs=[
            pl.BlockSpec((None, n_q, HEAD_DIM), lambda i, s, pt: (i, 0, 0)),
            pl.BlockSpec((None, N_KV, N_SELECT, LANES), lambda i, s, pt: (i, 0, 0, 0)),
        ],
        scratch_shapes=[pltpu.VMEM((N_KV * 2, n_pad, 2 * HEAD_DIM), F32),
                        pltpu.VMEM((4 * N_KV * (pages_per_step * page_rows + 4), HEAD_DIM), F32)],
    )
    kern = functools.partial(_sample_cmp_kernel, pages_per_step=pages_per_step, n_steps=n_steps,
                             t_pos=t_pos, n_slc=n_slc)
    return pl.pallas_call(
        kern,
        grid_spec=grid_spec,
        out_shape=[
            jax.ShapeDtypeStruct((bd, n_q, HEAD_DIM), F32),
            jax.ShapeDtypeStruct((bd, N_KV, N_SELECT, LANES), jnp.int32),
        ],
        compiler_params=_params("parallel", "arbitrary"),
        name="sample_cmp",
    )(page_table, *([cache_pages] * pages_per_step), new_rows, w1cat, pe8, w1f, w2, kn, cos, sin, q3, ov)


def _sample_attn_kernel(pt_ref, sel_ref, *refs, t_pos, n_past_blocks):
    del pt_ref
    n_blk = N_KV * N_SELECT
    blk_refs = refs[:n_blk]
    new_kv_ref, new_win_ref, state_ref, q_ref, gate_ref, oc_ref, o_ref = refs[n_blk:]
    n_paged = 4 * N_KV
    n_win = 2 * N_KV
    b = pl.program_id(0)
    q = q_ref[...]
    qf = q.astype(F32)
    n_q = q.shape[0]
    rep = n_q // N_KV
    head = lax.broadcasted_iota(jnp.int32, (n_q, 1), 0)
    n_keys = N_SELECT * SLC_BLOCK
    lane = lax.broadcasted_iota(jnp.int32, (1, n_keys), 1)
    row_b = lax.broadcasted_iota(jnp.int32, (SLC_BLOCK, HEAD_DIM), 0)
    w_keep = state_ref.shape[0] // n_win
    o_s = jnp.zeros((n_q, HEAD_DIM), F32)
    o_w = jnp.zeros((n_q, HEAD_DIM), F32)
    for g in range(N_KV):
        in_group = (head // rep) == g
        k_head = 2 * N_KV + g
        v_head = 3 * N_KV + g
        tail_k = jnp.where(row_b == 0, new_kv_ref[k_head:k_head + 1, :], 0.0)
        tail_v = jnp.where(row_b == 0, new_kv_ref[v_head:v_head + 1, :], 0.0)
        k_parts, v_parts = [], []
        base = jnp.zeros((1, n_keys), jnp.int32)
        for k in range(N_SELECT):
            blk = sel_ref[b, g * N_SELECT + k]
            is_tail = blk >= n_past_blocks
            blk_ref = blk_refs[g * N_SELECT + k]
            k_rows = blk_ref[pl.ds(k_head, SLC_BLOCK, stride=n_paged), :]
            v_rows = blk_ref[pl.ds(v_head, SLC_BLOCK, stride=n_paged), :]
            k_parts.append(jnp.where(is_tail, tail_k, k_rows).astype(BF16))
            v_parts.append(jnp.where(is_tail, tail_v, v_rows).astype(BF16))
            base = jnp.where(lane // SLC_BLOCK == k, blk * SLC_BLOCK, base)
        keys = jnp.concatenate(k_parts, axis=0)
        vals = jnp.concatenate(v_parts, axis=0)
        tok = base + lane % SLC_BLOCK
        s = lax.dot_general(q, keys, _NT, preferred_element_type=F32)
        p = _softmax_rows(s, tok <= t_pos)
        o_s = jnp.where(in_group, jnp.dot(p.astype(BF16), vals, preferred_element_type=F32), o_s)

        kw = state_ref[pl.ds(g, w_keep, stride=n_win), :].astype(BF16)
        vw = state_ref[pl.ds(N_KV + g, w_keep, stride=n_win), :].astype(BF16)
        kw_new = new_win_ref[g:g + 1, :]
        vw_new = new_win_ref[N_KV + g:N_KV + g + 1, :]
        dist = w_keep - lax.broadcasted_iota(jnp.int32, (1, w_keep), 1)
        m_w = (dist >= 0) & (dist < WINDOW) & (t_pos - dist >= 0)
        s_w = jnp.where(m_w, lax.dot_general(q, kw, _NT, preferred_element_type=F32), NEG)
        s_new = jnp.sum(qf * kw_new, axis=-1, keepdims=True)
        m = jnp.maximum(jnp.max(s_w, axis=-1, keepdims=True), s_new)
        e_w = jnp.exp2(s_w - m)
        e_new = jnp.exp2(s_new - m)
        denom = jnp.sum(e_w, axis=-1, keepdims=True) + e_new
        num = jnp.dot(e_w.astype(BF16), vw, preferred_element_type=F32) + e_new * vw_new
        o_w = jnp.where(in_group, num / denom, o_w)

    gates = jnp.broadcast_to(gate_ref[...], (n_q, LANES))
    glane = lax.broadcasted_iota(jnp.int32, (n_q, LANES), 1)
    gsel = [jnp.sum(jnp.where(glane == head * N_BRANCH + br, gates, 0.0), axis=-1, keepdims=True)
            for br in range(N_BRANCH)]
    o_ref[...] = (gsel[0] * oc_ref[...] + gsel[1] * o_s + gsel[2] * o_w).astype(BF16)


def _sample_attn(page_table, sel_idx, cache_blocks, new_kv, new_win, state3, q3, gates3, o_c,
                 t_pos, n_past_blocks, blocks_per_page):
    bd = page_table.shape[0]
    n_q = q3.shape[1]

    n_pool = cache_blocks.shape[0] // blocks_per_page

    def blk_spec(g, k):
        def index(i, pt, sel):
            blk = jnp.clip(sel[i, g * N_SELECT + k], 0, n_past_blocks - 1)
            page = jnp.clip(pt[i, blk // blocks_per_page], 0, n_pool - 1)
            return (page * blocks_per_page + blk % blocks_per_page, 0, 0)
        return pl.BlockSpec((None,) + cache_blocks.shape[1:], index)

    blk_specs = [blk_spec(g, k) for g in range(N_KV) for k in range(N_SELECT)]
    per_b = lambda shape: pl.BlockSpec((None,) + shape, lambda i, pt, sel: (i, 0, 0))
    grid_spec = pltpu.PrefetchScalarGridSpec(
        num_scalar_prefetch=2,
        grid=(bd,),
        in_specs=blk_specs + [
            per_b(new_kv.shape[1:]),
            per_b(new_win.shape[1:]),
            per_b(state3.shape[1:]),
            per_b((n_q, HEAD_DIM)),
            per_b((1, LANES)),
            per_b((n_q, HEAD_DIM)),
        ],
        out_specs=per_b((n_q, HEAD_DIM)),
    )
    kern = functools.partial(_sample_attn_kernel, t_pos=t_pos, n_past_blocks=n_past_blocks)
    n_blk = N_KV * N_SELECT
    return pl.pallas_call(
        kern,
        grid_spec=grid_spec,
        out_shape=jax.ShapeDtypeStruct((bd, n_q, HEAD_DIM), BF16),
        compiler_params=_params("parallel"),
        name="sample_attn",
    )(page_table, sel_idx, *([cache_blocks] * n_blk), new_kv, new_win, state3, q3, gates3, o_c)


def kernel(x_prompt, x_sample, cache_kv, state_kv_win, page_table, a_norm, a_w_in, a_v_norm, a_w_s,
           a_b_s, a_w_out, mlp_norm, mlp_w_up, mlp_w_down, kv_norm, w_kv, cmp_pe, cmp_w1, cmp_w2,
           k_norm, b_norm, b_w_in, b_q_norm, b_w_out):
    bp, t, d = x_prompt.shape
    bd, td, _ = x_sample.shape
    n_pool, page_size = cache_kv.shape[:2]
    n_pages = page_table.shape[1]
    past_len = n_pages * page_size
    w_keep = state_kv_win.shape[1]
    depth = mlp_norm.shape[0]
    n_a = a_norm.shape[0]
    n_b = b_norm.shape[0]
    d_q = b_w_out.shape[1]
    assert td == 1 and n_b == 1 and depth == n_a + n_b
    assert d // A_GROUPS == LANES and t % KV_TILE == 0 and t >= Q_BLOCK + WINDOW
    assert w_keep == WINDOW and page_size % SLC_BLOCK == 0 and past_len % KV_ALIGN == 0

    row = lambda v: v.reshape(1, -1)
    hp = x_prompt.reshape(bp * t, d)
    hs = x_sample.reshape(bd, d)
    tm = 512

    v_rows = []
    for l in range(n_a):
        w_in = a_w_in[l].astype(BF16)
        w_out = a_w_out[l].astype(BF16)
        zs = _gmlp_in(hs, row(a_norm[l]), w_in, bd, 512)
        wd = jnp.repeat(a_w_s[l][:, 0, 0], d // A_GROUPS).reshape(1, d)
        bb = jnp.repeat(a_b_s[l][:, 0], d // A_GROUPS).reshape(1, d)
        hs, v_s = _gmlp_out_single(hs, zs, row(a_v_norm[l]), wd, bb, w_out)
        v_rows.append(v_s.reshape(bd, td, d))
        hs, w_up, w_down = _mlp_cast(hs, row(mlp_norm[l]), mlp_w_up, mlp_w_down, l, 512)
        zp = _gmlp_in(hp, row(a_norm[l]), w_in, 2 * tm, 512)
        hp = _gmlp_out(hp, zp, row(a_v_norm[l]), a_w_s[l], a_b_s[l].T, w_out, 256)
        hp = _mlp(hp, row(mlp_norm[l]), w_up, w_down, 2 * tm, 512)

    w_kv_b = w_kv.astype(BF16)
    cos_p, sin_p = _rope_tables(jnp.arange(t))
    cos_pp, sin_pp = jnp.tile(cos_p, (bp, 1)), jnp.tile(sin_p, (bp, 1))
    cos_s, sin_s = _rope_tables(jnp.full((bd,), past_len))
    paged_p, win_p, cmp_rows_p, kvb_p = _kv_proj(hp, row(kv_norm), w_kv_b, k_norm, cos_pp, sin_pp, tm)
    paged_s, win_s, _, _ = _kv_proj(hs, row(kv_norm), w_kv_b, k_norm, cos_s, sin_s, bd)

    j = 0
    wq = b_w_in[j].astype(BF16)
    wg = jnp.pad(b_w_in[j][:, d_q:], ((0, 0), (0, LANES - (b_w_in.shape[2] - d_q)))).astype(BF16)
    q_p, gates_p = _q_proj(hp, row(b_norm[j]), wq, wg, row(b_q_norm[j]), cos_pp, sin_pp, tm, d_q)
    q_s, gates_s = _q_proj(hs, row(b_norm[j]), wq, wg, row(b_q_norm[j]), cos_s, sin_s, bd, d_q)

    w1cat = jnp.concatenate([cmp_w1[:, :CMP_STRIDE], cmp_w1[:, CMP_STRIDE:]], axis=-1).astype(BF16)
    w1cat = w1cat.reshape(2, CMP_STRIDE * HEAD_DIM, 2 * HEAD_DIM)
    w1f = cmp_w1.reshape(2, CMP_BLOCK * HEAD_DIM, HEAD_DIM).astype(BF16)
    pe8 = jnp.broadcast_to(cmp_pe.reshape(2, 1, CMP_BLOCK * HEAD_DIM),
                           (2, SUBLANES, CMP_BLOCK * HEAD_DIM)).astype(BF16)
    w2 = cmp_w2.astype(BF16)

    n_sub_p = t // CMP_STRIDE
    n_slc_p = t // SLC_BLOCK
    cos_c, sin_c = _rope_tables(jnp.arange(n_sub_p) * CMP_STRIDE + CMP_BLOCK - 1)
    kc_p, vc_p = _compress_prompt(cmp_rows_p, w1cat, pe8, w1f, w2, k_norm, cos_c, sin_c, bp, t)
    ovt = _overlap_matrix(n_sub_p - 1, n_slc_p, n_sub_p, n_slc_p).T
    expand = jnp.where(jnp.arange(t)[:, None] // SLC_BLOCK == jnp.arange(LANES)[None, :], NEG, 0.0).astype(BF16)
    o_p = _attn_prompt(q_p, gates_p, kc_p, vc_p, kvb_p, ovt, expand, bp, t)
    w_o = b_w_out[j].astype(BF16)
    hp = _out_proj(hp, o_p, w_o, tm)

    tp_s = -(-(past_len + td) // KV_ALIGN) * KV_ALIGN
    n_sub_s = tp_s // CMP_STRIDE
    n_slc_s = tp_s // SLC_BLOCK
    n_pad_s = -(-n_sub_s // SUBLANES) * SUBLANES
    sel_lanes = -(-n_slc_s // LANES) * LANES
    assert past_len // SLC_BLOCK + 1 >= N_SELECT
    cos_cs, sin_cs = _rope_tables(jnp.arange(n_pad_s) * CMP_STRIDE + CMP_BLOCK - 1)
    ov_s = _overlap_matrix(n_sub_s - 1, n_slc_s, n_pad_s, sel_lanes)
    n_paged = 4 * N_KV
    n_win = 2 * N_KV
    cache_pages = cache_kv.reshape(n_pool, page_size * n_paged, HEAD_DIM)
    q3 = q_s.reshape(bd, d_q // HEAD_DIM, HEAD_DIM)
    new_kv = paged_s.reshape(bd, n_paged, HEAD_DIM)
    new_win = jnp.pad(win_s.reshape(bd, n_win, HEAD_DIM), ((0, 0), (0, SUBLANES - n_win), (0, 0)))
    o_c, sel_idx = _sample_cmp(page_table, cache_pages, new_kv, w1cat, pe8, w1f, w2, k_norm, cos_cs,
                               sin_cs, q3, ov_s, min(16, n_pages), past_len, n_slc_s)
    blocks_per_page = page_size // SLC_BLOCK
    cache_blocks = cache_kv.reshape(n_pool * blocks_per_page, SLC_BLOCK * n_paged, HEAD_DIM)
    state3 = state_kv_win.reshape(bd, w_keep * n_win, HEAD_DIM)
    o_s = _sample_attn(page_table, sel_idx[..., 0].reshape(bd, N_KV * N_SELECT), cache_blocks, new_kv,
                       new_win, state3, q3, gates_s.reshape(bd, 1, LANES), o_c, past_len,
                       past_len // SLC_BLOCK, blocks_per_page)
    hs = _out_proj(hs, o_s.reshape(bd, d_q), w_o, bd)

    l = n_a
    hs, w_up, w_down = _mlp_cast(hs, row(mlp_norm[l]), mlp_w_up, mlp_w_down, l, 512)
    hp = _mlp(hp, row(mlp_norm[l]), w_up, w_down, 2 * tm, 512)

    y_p = hp.reshape(bp, t, d)
    y_s = hs.reshape(bd, td, d)
    kv_p = paged_p.reshape(bp, t, 4, N_KV, HEAD_DIM)
    win_all_p = win_p.reshape(bp, t, 2, N_KV, HEAD_DIM)
    win_new_p = win_all_p[:, t - min(WINDOW, t):]
    kv_s = paged_s.reshape(bd, td, 4, N_KV, HEAD_DIM)
    win_new_s = jnp.concatenate(
        [state_kv_win[:, td:], win_s.reshape(bd, td, 2, N_KV, HEAD_DIM)], axis=1)
    v_a_s = jnp.stack(v_rows, axis=0)
    return (y_p, y_s, kv_p, win_new_p, kv_s, win_new_s, v_a_s)
```

```python
import functools

import jax
import jax.numpy as jnp
from jax import lax
from jax.experimental import pallas as pl
from jax.experimental.pallas import tpu as pltpu

F32 = jnp.float32
BF16 = jnp.bfloat16

CHUNK = 128
A_GROUPS = 16
HEAD_DIM = 128
N_KV = 2
CMP_STRIDE = 16
CMP_BLOCK = 2 * CMP_STRIDE
SLC_BLOCK = 64
N_SELECT = 16
WINDOW = 512
N_BRANCH = 3
KV_ALIGN = 64
ROT_DIM = HEAD_DIM // 4
ROPE_THETA = 500000.0
Q_BLOCK = 128
EPS = 1e-6
NEG = -1e30
FORCE_BONUS = 1e4
LOG2_E = 1.4426950408889634

LANES = 128
SUBLANES = 8
VMEM_LIMIT_BYTES = 56 * 1024 * 1024

KV_TILE = 512
N_KINDS = 6

_NT = (((1,), (1,)), ((), ()))


def _params(*sem):
    return pltpu.CompilerParams(dimension_semantics=sem, vmem_limit_bytes=VMEM_LIMIT_BYTES)


def _rms(x, g):
    ms = jnp.mean(x * x, axis=-1, keepdims=True)
    return x * lax.rsqrt(ms + EPS) * g


def _rope(x, cos, sin):
    half = ROT_DIM // 2
    lane = lax.broadcasted_iota(jnp.int32, x.shape, 1)
    partner = jnp.where(lane < half, pltpu.roll(x, LANES - half, 1), pltpu.roll(x, half, 1))
    return x * cos + partner * sin


def _rope_tables(pos):
    inv = ROPE_THETA ** (-jnp.arange(0, ROT_DIM, 2, dtype=F32) / ROT_DIM)
    ang = pos.astype(F32)[:, None] * inv[None, :]
    c, s = jnp.cos(ang), jnp.sin(ang)
    n = pos.shape[0]
    pad = LANES - ROT_DIM
    cos = jnp.concatenate([c, c, jnp.ones((n, pad), F32)], axis=1)
    sin = jnp.concatenate([-s, s, jnp.zeros((n, pad), F32)], axis=1)
    return cos, sin


def _gmlp_in_kernel(x_ref, g_ref, w_ref, z_ref, xn_ref):
    @pl.when(pl.program_id(1) == 0)
    def _():
        xn_ref[...] = _rms(x_ref[...], g_ref[...]).astype(BF16)

    z_ref[...] = jax.nn.gelu(jnp.dot(xn_ref[...], w_ref[...], preferred_element_type=F32)).astype(z_ref.dtype)


def _gmlp_in(x, g, w, tm, tn):
    m, d = x.shape
    n = w.shape[1]
    return pl.pallas_call(
        _gmlp_in_kernel,
        grid=(m // tm, n // tn),
        in_specs=[
            pl.BlockSpec((tm, d), lambda i, j: (i, 0)),
            pl.BlockSpec((1, d), lambda i, j: (0, 0)),
            pl.BlockSpec((d, tn), lambda i, j: (0, j)),
        ],
        out_specs=pl.BlockSpec((tm, tn), lambda i, j: (i, j)),
        out_shape=jax.ShapeDtypeStruct((m, n), BF16),
        scratch_shapes=[pltpu.VMEM((tm, d), BF16)],
        compiler_params=_params("parallel", "arbitrary"),
        name="gmlp_in",
    )(x, g, w)


def _gmlp_out_kernel(x_ref, u_ref, v_ref, vg_ref, ws_ref, bt_ref, wo_ref, o_ref, vn_ref, y_ref):
    tm = x_ref.shape[0]
    vn_ref[...] = _rms(v_ref[...].astype(F32), vg_ref[...]).astype(BF16)
    row = lax.broadcasted_iota(jnp.int32, (CHUNK, CHUNK), 0)
    col = lax.broadcasted_iota(jnp.int32, (CHUNK, CHUNK), 1)
    causal = row >= col
    for g in range(A_GROUPS):
        wsg = jnp.where(causal, ws_ref[g], 0.0).astype(BF16)
        bias = bt_ref[:, g:g + 1]
        cs = slice(g * LANES, (g + 1) * LANES)
        for c in range(tm // CHUNK):
            rs = slice(c * CHUNK, (c + 1) * CHUNK)
            mixed = jnp.dot(wsg, vn_ref[rs, cs], preferred_element_type=F32) + bias
            y_ref[rs, cs] = (u_ref[rs, cs].astype(F32) * mixed).astype(BF16)
    o_ref[...] = x_ref[...] + jnp.dot(y_ref[...], wo_ref[...], preferred_element_type=F32)


def _gmlp_out(x, z, vg, ws, bt, wo, tm):
    m, d = x.shape
    return pl.pallas_call(
        _gmlp_out_kernel,
        grid=(m // tm,),
        in_specs=[
            pl.BlockSpec((tm, d), lambda i: (i, 0)),
            pl.BlockSpec((tm, d), lambda i: (i, 0)),
            pl.BlockSpec((tm, d), lambda i: (i, 1)),
            pl.BlockSpec((1, d), lambda i: (0, 0)),
            pl.BlockSpec((A_GROUPS, CHUNK, CHUNK), lambda i: (0, 0, 0)),
            pl.BlockSpec((CHUNK, A_GROUPS), lambda i: (0, 0)),
            pl.BlockSpec((d, d), lambda i: (0, 0)),
        ],
        out_specs=pl.BlockSpec((tm, d), lambda i: (i, 0)),
        out_shape=jax.ShapeDtypeStruct((m, d), F32),
        scratch_shapes=[pltpu.VMEM((tm, d), BF16), pltpu.VMEM((tm, d), BF16)],
        compiler_params=_params("parallel"),
        name="gmlp_out",
    )(x, z, z, vg, ws, bt, wo)


def _gmlp_out_single_kernel(x_ref, u_ref, v_ref, vg_ref, wd_ref, bb_ref, wo_ref, o_ref, vn_ref):
    vn = _rms(v_ref[...].astype(F32), vg_ref[...])
    vn_ref[...] = vn
    mixed = vn * wd_ref[...] + bb_ref[...]
    y = (u_ref[...].astype(F32) * mixed).astype(BF16)
    o_ref[...] = x_ref[...] + jnp.dot(y, wo_ref[...], preferred_element_type=F32)


def _gmlp_out_single(x, z, vg, wd, bb, wo):
    m, d = x.shape
    full = lambda i: (0, 0)
    return pl.pallas_call(
        _gmlp_out_single_kernel,
        grid=(1,),
        in_specs=[
            pl.BlockSpec((m, d), full),
            pl.BlockSpec((m, d), lambda i: (0, 0)),
            pl.BlockSpec((m, d), lambda i: (0, 1)),
            pl.BlockSpec((1, d), full),
            pl.BlockSpec((1, d), full),
            pl.BlockSpec((1, d), full),
            pl.BlockSpec((d, d), full),
        ],
        out_specs=[pl.BlockSpec((m, d), full), pl.BlockSpec((m, d), full)],
        out_shape=[jax.ShapeDtypeStruct((m, d), F32), jax.ShapeDtypeStruct((m, d), F32)],
        compiler_params=_params("arbitrary"),
        name="gmlp_out_single",
    )(x, z, z, vg, wd, bb, wo)


def _mlp_kernel(x_ref, g_ref, wu_ref, wd_ref, o_ref, xn_ref):
    @pl.when(pl.program_id(1) == 0)
    def _():
        x = x_ref[...]
        xn_ref[...] = _rms(x, g_ref[...]).astype(BF16)
        o_ref[...] = x

    h = jnp.dot(xn_ref[...], wu_ref[...], preferred_element_type=F32)
    a = jnp.square(jnp.maximum(h, 0.0)).astype(BF16)
    o_ref[...] += jnp.dot(a, wd_ref[...], preferred_element_type=F32)


def _mlp(x, g, wu, wd, tm, tf):
    m, d = x.shape
    f = wu.shape[1]
    return pl.pallas_call(
        _mlp_kernel,
        grid=(m // tm, f // tf),
        in_specs=[
            pl.BlockSpec((tm, d), lambda i, j: (i, 0)),
            pl.BlockSpec((1, d), lambda i, j: (0, 0)),
            pl.BlockSpec((d, tf), lambda i, j: (0, j)),
            pl.BlockSpec((tf, d), lambda i, j: (j, 0)),
        ],
        out_specs=pl.BlockSpec((tm, d), lambda i, j: (i, 0)),
        out_shape=jax.ShapeDtypeStruct((m, d), F32),
        scratch_shapes=[pltpu.VMEM((tm, d), BF16)],
        compiler_params=_params("parallel", "arbitrary"),
        name="mlp",
    )(x, g, wu, wd)


def _mlp_cast_kernel(x_ref, g_ref, wu_ref, wd_ref, o_ref, wub_ref, wdb_ref, xn_ref):
    @pl.when(pl.program_id(0) == 0)
    def _():
        x = x_ref[...]
        xn_ref[...] = _rms(x, g_ref[...]).astype(BF16)
        o_ref[...] = x

    wu = wu_ref[...].astype(BF16)
    wd = wd_ref[...].astype(BF16)
    wub_ref[...] = wu
    wdb_ref[...] = wd
    h = jnp.dot(xn_ref[...], wu, preferred_element_type=F32)
    a = jnp.square(jnp.maximum(h, 0.0)).astype(BF16)
    o_ref[...] += jnp.dot(a, wd, preferred_element_type=F32)


def _mlp_cast(x, g, wu_all, wd_all, layer, tf):
    m, d = x.shape
    f = wu_all.shape[2]
    return pl.pallas_call(
        _mlp_cast_kernel,
        grid=(f // tf,),
        in_specs=[
            pl.BlockSpec((m, d), lambda j: (0, 0)),
            pl.BlockSpec((1, d), lambda j: (0, 0)),
            pl.BlockSpec((None, d, tf), lambda j: (layer, 0, j)),
            pl.BlockSpec((None, tf, d), lambda j: (layer, j, 0)),
        ],
        out_specs=[
            pl.BlockSpec((m, d), lambda j: (0, 0)),
            pl.BlockSpec((d, tf), lambda j: (0, j)),
            pl.BlockSpec((tf, d), lambda j: (j, 0)),
        ],
        out_shape=[
            jax.ShapeDtypeStruct((m, d), F32),
            jax.ShapeDtypeStruct((d, f), BF16),
            jax.ShapeDtypeStruct((f, d), BF16),
        ],
        scratch_shapes=[pltpu.VMEM((m, d), BF16)],
        compiler_params=_params("arbitrary"),
        name="mlp_cast",
    )(x, g, wu_all, wd_all)


def _kv_proj_kernel(x_ref, g_ref, w_ref, kn_ref, cos_ref, sin_ref, paged_ref, win_ref, cmp_ref, kvb_ref):
    tm = x_ref.shape[0]
    n_paged = 4 * N_KV
    n_win = 2 * N_KV
    xn = _rms(x_ref[...], g_ref[...]).astype(BF16)
    kv = jnp.dot(xn, w_ref[...], preferred_element_type=F32)
    cos, sin = cos_ref[...], sin_ref[...]
    for c in range(N_KINDS * N_KV):
        kind = c // N_KV
        h = kv[:, c * LANES:(c + 1) * LANES]
        if kind == 2:
            h = _rope(_rms(h, kn_ref[1:2, :]), cos, sin)
        elif kind == 4:
            h = _rope(_rms(h, kn_ref[2:3, :]), cos, sin)
        if kind < 4:
            paged_ref[pl.ds(c, tm, stride=n_paged), :] = h
        else:
            win_ref[pl.ds(c - n_paged, tm, stride=n_win), :] = h
        if kind < 2:
            cmp_ref[c] = h
        else:
            kvb_ref[c - 2 * N_KV] = h.astype(BF16)


def _kv_proj(x, g, w, kn, cos, sin, tm):
    m, d = x.shape
    n = w.shape[1]
    n_paged = 4 * N_KV
    n_win = 2 * N_KV
    return pl.pallas_call(
        _kv_proj_kernel,
        grid=(m // tm,),
        in_specs=[
            pl.BlockSpec((tm, d), lambda i: (i, 0)),
            pl.BlockSpec((1, d), lambda i: (0, 0)),
            pl.BlockSpec((d, n), lambda i: (0, 0)),
            pl.BlockSpec((3, HEAD_DIM), lambda i: (0, 0)),
            pl.BlockSpec((tm, LANES), lambda i: (i, 0)),
            pl.BlockSpec((tm, LANES), lambda i: (i, 0)),
        ],
        out_specs=[
            pl.BlockSpec((tm * n_paged, HEAD_DIM), lambda i: (i, 0)),
            pl.BlockSpec((tm * n_win, HEAD_DIM), lambda i: (i, 0)),
            pl.BlockSpec((2 * N_KV, tm, HEAD_DIM), lambda i: (0, i, 0)),
            pl.BlockSpec((4 * N_KV, tm, HEAD_DIM), lambda i: (0, i, 0)),
        ],
        out_shape=[
            jax.ShapeDtypeStruct((m * n_paged, HEAD_DIM), F32),
            jax.ShapeDtypeStruct((m * n_win, HEAD_DIM), F32),
            jax.ShapeDtypeStruct((2 * N_KV, m, HEAD_DIM), F32),
            jax.ShapeDtypeStruct((4 * N_KV, m, HEAD_DIM), BF16),
        ],
        compiler_params=_params("parallel"),
        name="kv_proj",
    )(x, g, w, kn, cos, sin)


def _q_proj_kernel(x_ref, g_ref, wq_ref, wg_ref, qn_ref, cos_ref, sin_ref, q_ref, gate_ref):
    xn = _rms(x_ref[...], g_ref[...]).astype(BF16)
    z = jnp.dot(xn, wq_ref[...], preferred_element_type=F32)
    cos, sin = cos_ref[...], sin_ref[...]
    qn = qn_ref[...]
    scale = HEAD_DIM ** -0.5 * LOG2_E
    for h in range(z.shape[1] // HEAD_DIM):
        cs = slice(h * HEAD_DIM, (h + 1) * HEAD_DIM)
        qh = _rope(_rms(z[:, cs], qn), cos, sin) * scale
        q_ref[:, cs] = qh.astype(BF16)
    gate_ref[...] = jax.nn.sigmoid(jnp.dot(xn, wg_ref[...], preferred_element_type=F32))


def _q_proj(x, g, wq, wg, qn, cos, sin, tm, n):
    m, d = x.shape
    return pl.pallas_call(
        _q_proj_kernel,
        grid=(m // tm,),
        in_specs=[
            pl.BlockSpec((tm, d), lambda i: (i, 0)),
            pl.BlockSpec((1, d), lambda i: (0, 0)),
            pl.BlockSpec((d, n), lambda i: (0, 0)),
            pl.BlockSpec((d, LANES), lambda i: (0, 0)),
            pl.BlockSpec((1, HEAD_DIM), lambda i: (0, 0)),
            pl.BlockSpec((tm, LANES), lambda i: (i, 0)),
            pl.BlockSpec((tm, LANES), lambda i: (i, 0)),
        ],
        out_specs=[
            pl.BlockSpec((tm, n), lambda i: (i, 0)),
            pl.BlockSpec((tm, LANES), lambda i: (i, 0)),
        ],
        out_shape=[
            jax.ShapeDtypeStruct((m, n), BF16),
            jax.ShapeDtypeStruct((m, LANES), F32),
        ],
        compiler_params=_params("parallel"),
        name="q_proj",
    )(x, g, wq, wg, qn, cos, sin)


def _out_proj_kernel(h_ref, o_ref, w_ref, y_ref):
    y_ref[...] = h_ref[...] + jnp.dot(o_ref[...], w_ref[...], preferred_element_type=F32)


def _out_proj(h, o, w, tm):
    m, d = h.shape
    k = o.shape[1]
    return pl.pallas_call(
        _out_proj_kernel,
        grid=(m // tm,),
        in_specs=[
            pl.BlockSpec((tm, d), lambda i: (i, 0)),
            pl.BlockSpec((tm, k), lambda i: (i, 0)),
            pl.BlockSpec((k, d), lambda i: (0, 0)),
        ],
        out_specs=pl.BlockSpec((tm, d), lambda i: (i, 0)),
        out_shape=jax.ShapeDtypeStruct((m, d), F32),
        compiler_params=_params("parallel"),
        name="out_proj",
    )(h, o, w)


def _cmp_bias(pe_ref, w1f_ref, kind):
    return jnp.dot(pe_ref[kind], w1f_ref[kind], preferred_element_type=F32)[0:1, :]


def _cmp_finish(fs, bias, w2, n_rows):
    first = fs[:, :HEAD_DIM]
    second = pltpu.roll(fs[:, HEAD_DIM:], n_rows - 1, 0)
    h = first + second + bias
    h = h * jax.nn.sigmoid(h)
    return jnp.dot(h.astype(BF16), w2, preferred_element_type=F32)


def _compress_prompt_kernel(krows_ref, vrows_ref, w1_ref, pe_ref, w1f_ref, w2_ref, kn_ref,
                            cos_ref, sin_ref, kc_ref, vc_ref):
    n_sub = krows_ref.shape[0] // CMP_STRIDE
    rowi = lax.broadcasted_iota(jnp.int32, (n_sub, HEAD_DIM), 0)
    for kind, rows_ref, out_ref in ((0, krows_ref, kc_ref), (1, vrows_ref, vc_ref)):
        x = jnp.concatenate(
            [rows_ref[pl.ds(r, n_sub, stride=CMP_STRIDE), :].astype(BF16) for r in range(CMP_STRIDE)],
            axis=1)
        fs = jnp.dot(x, w1_ref[kind], preferred_element_type=F32)
        out = _cmp_finish(fs, _cmp_bias(pe_ref, w1f_ref, kind), w2_ref[kind], n_sub)
        if kind == 0:
            out = _rope(_rms(out, kn_ref[0:1, :]), cos_ref[...], sin_ref[...])
        out_ref[...] = jnp.where(rowi < n_sub - 1, out, 0.0).astype(BF16)


def _compress_prompt(paged, w1cat, pe8, w1f, w2, kn, cos, sin, b, t):
    n_sub = t // CMP_STRIDE
    full = lambda nd: (lambda i, g: (0,) * nd)
    out_spec = pl.BlockSpec((None, None, n_sub, HEAD_DIM), lambda i, g: (i, g, 0, 0))
    out_shape = jax.ShapeDtypeStruct((b, N_KV, n_sub, HEAD_DIM), BF16)
    return pl.pallas_call(
        _compress_prompt_kernel,
        grid=(b, N_KV),
        in_specs=[
            pl.BlockSpec((None, t, HEAD_DIM), lambda i, g: (g, i, 0)),
            pl.BlockSpec((None, t, HEAD_DIM), lambda i, g: (N_KV + g, i, 0)),
            pl.BlockSpec(w1cat.shape, full(3)),
            pl.BlockSpec(pe8.shape, full(3)),
            pl.BlockSpec(w1f.shape, full(3)),
            pl.BlockSpec(w2.shape, full(3)),
            pl.BlockSpec(kn.shape, full(2)),
            pl.BlockSpec(cos.shape, full(2)),
            pl.BlockSpec(sin.shape, full(2)),
        ],
        out_specs=[out_spec, out_spec],
        out_shape=[out_shape, out_shape],
        compiler_params=_params("parallel", "parallel"),
        name="compress_prompt",
    )(paged, paged, w1cat, pe8, w1f, w2, kn, cos, sin)


def _split_bf16(x):
    hi = x.astype(BF16)
    lo = (x - hi.astype(F32)).astype(BF16)
    return hi, lo


def _overlap_matrix(n_cmp, n_slc, rows, cols):
    ci = jnp.arange(rows)[:, None] * CMP_STRIDE
    sj = jnp.arange(cols)[None, :] * SLC_BLOCK
    ov = (ci < sj + SLC_BLOCK) & (ci + CMP_BLOCK > sj)
    ov = ov & (jnp.arange(rows)[:, None] < n_cmp) & (jnp.arange(cols)[None, :] < n_slc)
    return ov.astype(BF16)


def _softmax_rows(s, valid):
    s = jnp.where(valid, s, NEG)
    e = jnp.exp2(s - jnp.max(s, axis=-1, keepdims=True))
    return e / jnp.sum(e, axis=-1, keepdims=True)


def _attn_prompt_kernel(q_ref, gate_ref, kc_ref, vc_ref, ks_ref, vs_ref, kw_ref, vw_ref,
                        ovt_ref, nexp_ref, o_ref, m_ref, acc_ref, s_ref, kext_ref, ow_ref, gate_b_ref):
    qb = pl.program_id(2)
    g = pl.program_id(1)
    nq = Q_BLOCK
    rep = q_ref.shape[1] // HEAD_DIM
    n_cmp_pad = kc_ref.shape[0]
    n_slc = ovt_ref.shape[0]
    t_len = ks_ref.shape[0]
    s0 = qb * nq

    q_all = q_ref[...]
    q2 = jnp.concatenate([q_all[:, r * HEAD_DIM:(r + 1) * HEAD_DIM] for r in range(rep)], axis=0)
    t_col = s0 + lax.broadcasted_iota(jnp.int32, (nq, 1), 0)

    gates = gate_ref[...]
    g_lane = lax.broadcasted_iota(jnp.int32, gates.shape, 1)
    for r in range(rep):
        for br in range(N_BRANCH):
            col = (g * rep + r) * N_BRANCH + br
            picked = jnp.sum(jnp.where(g_lane == col, gates, 0.0), axis=-1, keepdims=True)
            gate_b_ref[r * N_BRANCH + br] = jnp.broadcast_to(picked, (nq, LANES))

    slab = nq + WINDOW
    w0 = pl.multiple_of(jnp.maximum(s0 - WINDOW, 0), nq)
    kw = kw_ref[pl.ds(w0, slab), :]
    vw = jnp.concatenate([vw_ref[pl.ds(w0, slab), :], jnp.ones((slab, HEAD_DIM), BF16)], axis=1)
    dist = t_col - (w0 + lax.broadcasted_iota(jnp.int32, (nq, slab), 1))
    bias_w = jnp.where((dist >= 0) & (dist < WINDOW), 0.0, NEG)
    s_w = lax.dot_general(q2, kw, _NT, preferred_element_type=F32)
    p_parts = []
    for r in range(rep):
        s_r = s_w[r * nq:(r + 1) * nq] + bias_w
        p_parts.append(jnp.exp2(s_r - jnp.max(s_r, axis=-1, keepdims=True)).astype(BF16))
    o_w = jnp.dot(jnp.concatenate(p_parts, axis=0), vw, preferred_element_type=F32)
    ow_ref[...] = o_w[:, :HEAD_DIM] / o_w[:, HEAD_DIM:]

    kc = kc_ref[...]
    vc = vc_ref[...]
    cpos = lax.broadcasted_iota(jnp.int32, (1, n_cmp_pad), 1) * CMP_STRIDE + (CMP_BLOCK - 1)
    m_c = cpos <= t_col
    s_c = lax.dot_general(q2, kc, _NT, preferred_element_type=F32)
    psum = jnp.zeros((nq, n_cmp_pad), F32)
    p_parts = []
    for r in range(rep):
        p = jnp.where(m_c, _softmax_rows(s_c[r * nq:(r + 1) * nq], m_c), 0.0)
        psum = psum + p
        p_parts.append(p.astype(BF16))
    o_c = jnp.dot(jnp.concatenate(p_parts, axis=0), vc, preferred_element_type=F32)

    ovt = ovt_ref[...]
    p_hi, p_lo = _split_bf16(psum)
    imp_t = (lax.dot_general(ovt, p_hi, _NT, preferred_element_type=F32)
             + lax.dot_general(ovt, p_lo, _NT, preferred_element_type=F32))
    t_row = s0 + lax.broadcasted_iota(jnp.int32, (n_slc, nq), 1)
    blk = lax.broadcasted_iota(jnp.int32, (n_slc, nq), 0)
    cur = t_row // SLC_BLOCK
    causal = blk * SLC_BLOCK <= t_row
    forced = (blk == 0) | (blk == cur) | (blk == cur - 1)
    score = jnp.where(causal, imp_t + jnp.where(forced, FORCE_BONUS, 0.0), NEG)
    rank = jnp.zeros((n_slc, nq), jnp.int32)
    for i in range(n_slc):
        si = score[i:i + 1, :]
        later = (blk > i).astype(jnp.int32)
        rank = rank + jnp.where(si > score, 1, 0) + jnp.where(si == score, later, 0)
    sel_t = jnp.where(causal & (rank < N_SELECT), 1.0, 0.0)
    sel_t = jnp.concatenate([sel_t, jnp.zeros((LANES - n_slc, nq), F32)], axis=0)
    unsel = 1.0 - sel_t.T

    @pl.when(qb == 0)
    def _():
        kext_ref[:, :HEAD_DIM] = ks_ref[...]
        kext_ref[:, HEAD_DIM:] = nexp_ref[...]

    blk_lane = lax.broadcasted_iota(jnp.int32, (nq, LANES), 1)
    unsel_main = jnp.where(blk_lane >= s0 // SLC_BLOCK, 1.0, unsel).astype(BF16)
    q_ext = jnp.concatenate([q2, jnp.concatenate([unsel_main] * rep, axis=0)], axis=1)
    n_main = (s0 + KV_TILE - 1) // KV_TILE
    m_ref[...] = jnp.full(m_ref.shape, NEG, F32)

    def sweep1(off, width):
        s = lax.dot_general(q_ext, kext_ref[pl.ds(off, width), :], _NT, preferred_element_type=F32)
        s_ref[:, pl.ds(off, width)] = s
        m_run = m_ref[...]
        for c in range(width // LANES):
            m_run = jnp.maximum(m_run, s[:, c * LANES:(c + 1) * LANES])
        m_ref[...] = m_run

    def run_sweep(sweep):
        def quad(j, carry):
            sweep(pl.multiple_of(j * 4 * KV_TILE, 4 * KV_TILE), 4 * KV_TILE)
            return carry

        lax.fori_loop(0, n_main // 4, quad, 0)

        @pl.when(n_main % 4 >= 2)
        def _():
            sweep(pl.multiple_of((n_main // 4) * 4 * KV_TILE, 2 * KV_TILE), 2 * KV_TILE)

        @pl.when(n_main % 2 == 1)
        def _():
            sweep(pl.multiple_of((n_main - 1) * KV_TILE, KV_TILE), KV_TILE)

    run_sweep(sweep1)

    d0 = pl.multiple_of(s0, nq)
    q_i = lax.broadcasted_iota(jnp.int32, (nq, nq), 0)
    k_i = lax.broadcasted_iota(jnp.int32, (nq, nq), 1)
    bias_d = (lax.dot_general(unsel.astype(BF16), nexp_ref[pl.ds(d0, nq), :], _NT,
                              preferred_element_type=F32)
              + jnp.where(k_i <= q_i, 0.0, NEG))
    s_d = lax.dot_general(q2, ks_ref[pl.ds(d0, nq), :], _NT, preferred_element_type=F32)
    s_d = (s_d.reshape(rep, nq, nq) + bias_d[None]).reshape(rep * nq, nq)
    m_all = jnp.maximum(m_ref[...], s_d)
    m_b = jnp.broadcast_to(jnp.max(m_all, axis=-1, keepdims=True), m_all.shape)
    m_ref[...] = m_b
    v_d = jnp.concatenate([vs_ref[pl.ds(d0, nq), :], jnp.ones((nq, HEAD_DIM), BF16)], axis=1)
    acc_ref[...] = jnp.dot(jnp.exp2(s_d - m_b).astype(BF16), v_d, preferred_element_type=F32)

    def sweep2(off, width):
        v1 = jnp.concatenate([vs_ref[pl.ds(off, width), :], jnp.ones((width, HEAD_DIM), BF16)], axis=1)
        m_rows = m_ref[...]
        p = jnp.concatenate(
            [jnp.exp2(s_ref[:, pl.ds(pl.multiple_of(off + c * LANES, LANES), LANES)] - m_rows).astype(BF16)
             for c in range(width // LANES)], axis=1)
        acc_ref[...] += jnp.dot(p, v1, preferred_element_type=F32)

    run_sweep(sweep2)

    o_s = acc_ref[:, :HEAD_DIM] / acc_ref[:, HEAD_DIM:]

    for r in range(rep):
        rs = slice(r * nq, (r + 1) * nq)
        out = (gate_b_ref[r * N_BRANCH] * o_c[rs] + gate_b_ref[r * N_BRANCH + 1] * o_s[rs]
               + gate_b_ref[r * N_BRANCH + 2] * ow_ref[rs, :])
        o_ref[:, r * HEAD_DIM:(r + 1) * HEAD_DIM] = out.astype(BF16)


def _attn_prompt(q, gates, kc, vc, kvb, ovt, expand, b, t):
    nqb = t // Q_BLOCK
    rep = q.shape[1] // HEAD_DIM // N_KV
    n_cmp_pad = kc.shape[2]
    rows = lambda kind: pl.BlockSpec((None, t, HEAD_DIM), lambda i, g, j: ((kind - 2) * N_KV + g, i, 0))
    cmp_spec = pl.BlockSpec((None, None, n_cmp_pad, HEAD_DIM), lambda i, g, j: (i, g, 0, 0))
    return pl.pallas_call(
        _attn_prompt_kernel,
        grid=(b, N_KV, nqb),
        in_specs=[
            pl.BlockSpec((Q_BLOCK, rep * HEAD_DIM), lambda i, g, j: (i * nqb + j, g)),
            pl.BlockSpec((Q_BLOCK, LANES), lambda i, g, j: (i * nqb + j, 0)),
            cmp_spec, cmp_spec,
            rows(2), rows(3), rows(4), rows(5),
            pl.BlockSpec(ovt.shape, lambda i, g, j: (0, 0)),
            pl.BlockSpec(expand.shape, lambda i, g, j: (0, 0)),
        ],
        out_specs=pl.BlockSpec((Q_BLOCK, rep * HEAD_DIM), lambda i, g, j: (i * nqb + j, g)),
        out_shape=jax.ShapeDtypeStruct(q.shape, BF16),
        scratch_shapes=[
            pltpu.VMEM((rep * Q_BLOCK, LANES), F32),
            pltpu.VMEM((rep * Q_BLOCK, 2 * HEAD_DIM), F32),
            pltpu.VMEM((rep * Q_BLOCK, t), F32),
            pltpu.VMEM((t, 2 * HEAD_DIM), BF16),
            pltpu.VMEM((rep * Q_BLOCK, HEAD_DIM), F32),
            pltpu.VMEM((rep * N_BRANCH, Q_BLOCK, LANES), F32),
        ],
        compiler_params=_params("parallel", "parallel", "arbitrary"),
        name="attn_prompt",
    )(q, gates, kc, vc, kvb, kvb, kvb, kvb, ovt, expand)


def _sample_cmp_kernel(pt_ref, *refs, pages_per_step, n_steps, t_pos, n_slc):
    del pt_ref
    page_refs = refs[:pages_per_step]
    (new_ref, w1_ref, pe_ref, w1f_ref, w2_ref, kn_ref, cos_ref, sin_ref, q_ref, ov_ref,
     oc_ref, idx_ref, fs_ref, stage_ref) = refs[pages_per_step:]
    step = pl.program_id(1)
    n_heads = N_KV * 2
    heads_per_row = 4 * N_KV
    page_rows = page_refs[0].shape[0] // heads_per_row
    sub_per_page = page_rows // CMP_STRIDE
    rows_per_step = pages_per_step * sub_per_page
    n_pad = fs_ref.shape[1]
    n_past = n_steps * rows_per_step
    row0 = pl.multiple_of(step * rows_per_step, rows_per_step)

    assert heads_per_row == SUBLANES
    slab = stage_ref.shape[0] // heads_per_row
    for k, p in enumerate(page_refs):
        for n in range(sub_per_page):
            row = k * sub_per_page + n
            for r in range(CMP_STRIDE):
                tok = p[pl.ds((n * CMP_STRIDE + r) * heads_per_row, heads_per_row), :]
                tile = (row // SUBLANES) * CMP_STRIDE + r
                stage_ref[pl.ds(tile * SUBLANES + row % SUBLANES, heads_per_row, stride=slab), :] = tok
    for c in range(n_heads):
        x = jnp.concatenate(
            [jnp.concatenate(
                [stage_ref[pl.ds(c * slab + (i * CMP_STRIDE + r) * SUBLANES, SUBLANES), :]
                 for r in range(CMP_STRIDE)], axis=1)
             for i in range(rows_per_step // SUBLANES)], axis=0)
        fs_ref[c, pl.ds(row0, rows_per_step), :] = jnp.dot(
            x.astype(BF16), w1_ref[c // N_KV], preferred_element_type=F32)

    @pl.when(step == n_steps - 1)
    def _():
        tail = n_pad - n_past
        row_t = lax.broadcasted_iota(jnp.int32, (tail, HEAD_DIM), 0)
        rowi = lax.broadcasted_iota(jnp.int32, (n_pad, HEAD_DIM), 0)
        n_sub = (t_pos + 1 + KV_ALIGN - 1) // KV_ALIGN * KV_ALIGN // CMP_STRIDE
        outs = []
        for c in range(n_heads):
            kind = c // N_KV
            new_row = new_ref[c:c + 1, :]
            x_tail = jnp.where(row_t == 0, new_row, 0.0).astype(BF16)
            fs_ref[c, n_past:n_pad, :] = jnp.dot(x_tail, w1_ref[kind, 0:HEAD_DIM, :],
                                                 preferred_element_type=F32)
            out = _cmp_finish(fs_ref[c], _cmp_bias(pe_ref, w1f_ref, kind), w2_ref[kind], n_pad)
            if kind == 0:
                out = _rope(_rms(out, kn_ref[0:1, :]), cos_ref[...], sin_ref[...])
            outs.append(jnp.where(rowi < n_sub - 1, out, 0.0).astype(BF16))

        q = q_ref[...]
        n_q = q.shape[0]
        rep = n_q // N_KV
        head = lax.broadcasted_iota(jnp.int32, (n_q, 1), 0)
        cpos = lax.broadcasted_iota(jnp.int32, (1, n_pad), 1) * CMP_STRIDE + (CMP_BLOCK - 1)
        m_c = cpos <= t_pos
        lanes = ov_ref.shape[1]
        blk = lax.broadcasted_iota(jnp.int32, (1, lanes), 1)
        cur = t_pos // SLC_BLOCK
        causal = (blk * SLC_BLOCK <= t_pos) & (blk < n_slc)
        forced = (blk == 0) | (blk == cur) | (blk == cur - 1)
        eye_i = lax.broadcasted_iota(jnp.int32, (lanes, lanes), 0)
        eye_j = lax.broadcasted_iota(jnp.int32, (lanes, lanes), 1)
        o_c = jnp.zeros((n_q, HEAD_DIM), F32)
        for g in range(N_KV):
            kc, vc = outs[g], outs[N_KV + g]
            in_group = (head // rep) == g
            s = lax.dot_general(q, kc, _NT, preferred_element_type=F32)
            p = jnp.where(m_c, _softmax_rows(s, m_c), 0.0)
            o_g = jnp.dot(p.astype(BF16), vc, preferred_element_type=F32)
            o_c = jnp.where(in_group, o_g, o_c)
            psum = jnp.sum(jnp.where(in_group, p, 0.0), axis=0, keepdims=True)
            psum8 = jnp.broadcast_to(psum, (SUBLANES, n_pad))
            p_hi, p_lo = _split_bf16(psum8)
            imp = (jnp.dot(p_hi, ov_ref[...], preferred_element_type=F32)
                   + jnp.dot(p_lo, ov_ref[...], preferred_element_type=F32))[0:1, :]
            score = jnp.where(causal, imp + jnp.where(forced, FORCE_BONUS, 0.0), NEG)
            score_b = jnp.broadcast_to(score, (lanes, lanes))
            score_col = jnp.sum(jnp.where(eye_i == eye_j, score_b, 0.0), axis=1, keepdims=True)
            beats = (score_col > score_b) | ((score_col == score_b) & (eye_i < eye_j))
            rank = jnp.sum(jnp.where(beats, 1.0, 0.0), axis=0, keepdims=True)
            sel = jnp.where(causal & (rank < N_SELECT), 1.0, 0.0)
            sel_b = jnp.broadcast_to(sel, (lanes, lanes))
            sel_col = jnp.sum(jnp.where(eye_i == eye_j, sel_b, 0.0), axis=1, keepdims=True)
            slot = jnp.sum(jnp.where(eye_i < eye_j, sel_col, 0.0), axis=0, keepdims=True)
            slot_b = jnp.broadcast_to(slot, (N_SELECT, lanes))
            k_i = lax.broadcasted_iota(jnp.int32, (N_SELECT, lanes), 0).astype(F32)
            j_i = lax.broadcasted_iota(jnp.int32, (N_SELECT, lanes), 1).astype(F32)
            hit = (slot_b == k_i) & (jnp.broadcast_to(sel, (N_SELECT, lanes)) > 0.5)
            idx = jnp.sum(jnp.where(hit, j_i, 0.0), axis=1, keepdims=True)
            idx_ref[g] = jnp.broadcast_to(idx, (N_SELECT, LANES)).astype(jnp.int32)
        oc_ref[...] = o_c


def _sample_cmp(page_table, cache_pages, new_rows, w1cat, pe8, w1f, w2, kn, cos, sin, q3, ov,
                pages_per_step, t_pos, n_slc):
    bd, n_pages = page_table.shape
    n_steps = n_pages // pages_per_step
    flat_rows = cache_pages.shape[1]
    page_rows = flat_rows // (4 * N_KV)
    n_pad = cos.shape[0]
    n_q = q3.shape[1]
    full = lambda nd: (lambda i, s, pt: (0,) * nd)

    n_pool = cache_pages.shape[0]

    def page_spec(k):
        return pl.BlockSpec(
            (None, flat_rows, HEAD_DIM),
            lambda i, s, pt: (jnp.clip(pt[i, s * pages_per_step + k], 0, n_pool - 1), 0, 0))

    grid_spec = pltpu.PrefetchScalarGridSpec(
        num_scalar_prefetch=1,
        grid=(bd, n_steps),
        in_specs=[page_spec(k) for k in range(pages_per_step)] + [
            pl.BlockSpec((None,) + new_rows.shape[1:], lambda i, s, pt: (i, 0, 0)),
            pl.BlockSpec(w1cat.shape, full(3)),
            pl.BlockSpec(pe8.shape, full(3)),
            pl.BlockSpec(w1f.shape, full(3)),
            pl.BlockSpec(w2.shape, full(3)),
            pl.BlockSpec(kn.shape, full(2)),
            pl.BlockSpec(cos.shape, full(2)),
            pl.BlockSpec(sin.shape, full(2)),
            pl.BlockSpec((None, n_q, HEAD_DIM), lambda i, s, pt: (i, 0, 0)),
            pl.BlockSpec(ov.shape, full(2)),
        ],
        out_specs=[
            pl.BlockSpec((None, n_q, HEAD_DIM), lambda i, s, pt: (i, 0, 0)),
            pl.BlockSpec((None, N_KV, N_SELECT, LANES), lambda i, s, pt: (i, 0, 0, 0)),
        ],
        scratch_shapes=[pltpu.VMEM((N_KV * 2, n_pad, 2 * HEAD_DIM), F32),
                        pltpu.VMEM((4 * N_KV * (pages_per_step * page_rows + 4), HEAD_DIM), F32)],
    )
    kern = functools.partial(_sample_cmp_kernel, pages_per_step=pages_per_step, n_steps=n_steps,
                             t_pos=t_pos, n_slc=n_slc)
    return pl.pallas_call(
        kern,
        grid_spec=grid_spec,
        out_shape=[
            jax.ShapeDtypeStruct((bd, n_q, HEAD_DIM), F32),
            jax.ShapeDtypeStruct((bd, N_KV, N_SELECT, LANES), jnp.int32),
        ],
        compiler_params=_params("parallel", "arbitrary"),
        name="sample_cmp",
    )(page_table, *([cache_pages] * pages_per_step), new_rows, w1cat, pe8, w1f, w2, kn, cos, sin, q3, ov)


def _sample_attn_kernel(pt_ref, sel_ref, *refs, t_pos, n_past_blocks):
    del pt_ref
    n_blk = N_KV * N_SELECT
    blk_refs = refs[:n_blk]
    new_kv_ref, new_win_ref, state_ref, q_ref, gate_ref, oc_ref, o_ref = refs[n_blk:]
    n_paged = 4 * N_KV
    n_win = 2 * N_KV
    b = pl.program_id(0)
    q = q_ref[...]
    qf = q.astype(F32)
    n_q = q.shape[0]
    rep = n_q // N_KV
    head = lax.broadcasted_iota(jnp.int32, (n_q, 1), 0)
    n_keys = N_SELECT * SLC_BLOCK
    lane = lax.broadcasted_iota(jnp.int32, (1, n_keys), 1)
    row_b = lax.broadcasted_iota(jnp.int32, (SLC_BLOCK, HEAD_DIM), 0)
    w_keep = state_ref.shape[0] // n_win
    o_s = jnp.zeros((n_q, HEAD_DIM), F32)
    o_w = jnp.zeros((n_q, HEAD_DIM), F32)
    for g in range(N_KV):
        in_group = (head // rep) == g
        k_head = 2 * N_KV + g
        v_head = 3 * N_KV + g
        tail_k = jnp.where(row_b == 0, new_kv_ref[k_head:k_head + 1, :], 0.0)
        tail_v = jnp.where(row_b == 0, new_kv_ref[v_head:v_head + 1, :], 0.0)
        k_parts, v_parts = [], []
        base = jnp.zeros((1, n_keys), jnp.int32)
        for k in range(N_SELECT):
            blk = sel_ref[b, g * N_SELECT + k]
            is_tail = blk >= n_past_blocks
            blk_ref = blk_refs[g * N_SELECT + k]
            k_rows = blk_ref[pl.ds(k_head, SLC_BLOCK, stride=n_paged), :]
            v_rows = blk_ref[pl.ds(v_head, SLC_BLOCK, stride=n_paged), :]
            k_parts.append(jnp.where(is_tail, tail_k, k_rows).astype(BF16))
            v_parts.append(jnp.where(is_tail, tail_v, v_rows).astype(BF16))
            base = jnp.where(lane // SLC_BLOCK == k, blk * SLC_BLOCK, base)
        keys = jnp.concatenate(k_parts, axis=0)
        vals = jnp.concatenate(v_parts, axis=0)
        tok = base + lane % SLC_BLOCK
        s = lax.dot_general(q, keys, _NT, preferred_element_type=F32)
        p = _softmax_rows(s, tok <= t_pos)
        o_s = jnp.where(in_group, jnp.dot(p.astype(BF16), vals, preferred_element_type=F32), o_s)

        kw = state_ref[pl.ds(g, w_keep, stride=n_win), :].astype(BF16)
        vw = state_ref[pl.ds(N_KV + g, w_keep, stride=n_win), :].astype(BF16)
        kw_new = new_win_ref[g:g + 1, :]
        vw_new = new_win_ref[N_KV + g:N_KV + g + 1, :]
        dist = w_keep - lax.broadcasted_iota(jnp.int32, (1, w_keep), 1)
        m_w = (dist >= 0) & (dist < WINDOW) & (t_pos - dist >= 0)
        s_w = jnp.where(m_w, lax.dot_general(q, kw, _NT, preferred_element_type=F32), NEG)
        s_new = jnp.sum(qf * kw_new, axis=-1, keepdims=True)
        m = jnp.maximum(jnp.max(s_w, axis=-1, keepdims=True), s_new)
        e_w = jnp.exp2(s_w - m)
        e_new = jnp.exp2(s_new - m)
        denom = jnp.sum(e_w, axis=-1, keepdims=True) + e_new
        num = jnp.dot(e_w.astype(BF16), vw, preferred_element_type=F32) + e_new * vw_new
        o_w = jnp.where(in_group, num / denom, o_w)

    gates = jnp.broadcast_to(gate_ref[...], (n_q, LANES))
    glane = lax.broadcasted_iota(jnp.int32, (n_q, LANES), 1)
    gsel = [jnp.sum(jnp.where(glane == head * N_BRANCH + br, gates, 0.0), axis=-1, keepdims=True)
            for br in range(N_BRANCH)]
    o_ref[...] = (gsel[0] * oc_ref[...] + gsel[1] * o_s + gsel[2] * o_w).astype(BF16)


def _sample_attn(page_table, sel_idx, cache_blocks, new_kv, new_win, state3, q3, gates3, o_c,
                 t_pos, n_past_blocks, blocks_per_page):
    bd = page_table.shape[0]
    n_q = q3.shape[1]

    n_pool = cache_blocks.shape[0] // blocks_per_page

    def blk_spec(g, k):
        def index(i, pt, sel):
            blk = jnp.clip(sel[i, g * N_SELECT + k], 0, n_past_blocks - 1)
            page = jnp.clip(pt[i, blk // blocks_per_page], 0, n_pool - 1)
            return (page * blocks_per_page + blk % blocks_per_page, 0, 0)
        return pl.BlockSpec((None,) + cache_blocks.shape[1:], index)

    blk_specs = [blk_spec(g, k) for g in range(N_KV) for k in range(N_SELECT)]
    per_b = lambda shape: pl.BlockSpec((None,) + shape, lambda i, pt, sel: (i, 0, 0))
    grid_spec = pltpu.PrefetchScalarGridSpec(
        num_scalar_prefetch=2,
        grid=(bd,),
        in_specs=blk_specs + [
            per_b(new_kv.shape[1:]),
            per_b(new_win.shape[1:]),
            per_b(state3.shape[1:]),
            per_b((n_q, HEAD_DIM)),
            per_b((1, LANES)),
            per_b((n_q, HEAD_DIM)),
        ],
        out_specs=per_b((n_q, HEAD_DIM)),
    )
    kern = functools.partial(_sample_attn_kernel, t_pos=t_pos, n_past_blocks=n_past_blocks)
    n_blk = N_KV * N_SELECT
    return pl.pallas_call(
        kern,
        grid_spec=grid_spec,
        out_shape=jax.ShapeDtypeStruct((bd, n_q, HEAD_DIM), BF16),
        compiler_params=_params("parallel"),
        name="sample_attn",
    )(page_table, sel_idx, *([cache_blocks] * n_blk), new_kv, new_win, state3, q3, gates3, o_c)


def kernel(x_prompt, x_sample, cache_kv, state_kv_win, page_table, a_norm, a_w_in, a_v_norm, a_w_s,
           a_b_s, a_w_out, mlp_norm, mlp_w_up, mlp_w_down, kv_norm, w_kv, cmp_pe, cmp_w1, cmp_w2,
           k_norm, b_norm, b_w_in, b_q_norm, b_w_out):
    bp, t, d = x_prompt.shape
    bd, td, _ = x_sample.shape
    n_pool, page_size = cache_kv.shape[:2]
    n_pages = page_table.shape[1]
    past_len = n_pages * page_size
    w_keep = state_kv_win.shape[1]
    depth = mlp_norm.shape[0]
    n_a = a_norm.shape[0]
    n_b = b_norm.shape[0]
    d_q = b_w_out.shape[1]
    assert td == 1 and n_b == 1 and depth == n_a + n_b
    assert d // A_GROUPS == LANES and t % KV_TILE == 0 and t >= Q_BLOCK + WINDOW
    assert w_keep == WINDOW and page_size % SLC_BLOCK == 0 and past_len % KV_ALIGN == 0

    row = lambda v: v.reshape(1, -1)
    hp = x_prompt.reshape(bp * t, d)
    hs = x_sample.reshape(bd, d)
    tm = 512

    v_rows = []
    for l in range(n_a):
        w_in = a_w_in[l].astype(BF16)
        w_out = a_w_out[l].astype(BF16)
        zs = _gmlp_in(hs, row(a_norm[l]), w_in, bd, 512)
        wd = jnp.repeat(a_w_s[l][:, 0, 0], d // A_GROUPS).reshape(1, d)
        bb = jnp.repeat(a_b_s[l][:, 0], d // A_GROUPS).reshape(1, d)
        hs, v_s = _gmlp_out_single(hs, zs, row(a_v_norm[l]), wd, bb, w_out)
        v_rows.append(v_s.reshape(bd, td, d))
        hs, w_up, w_down = _mlp_cast(hs, row(mlp_norm[l]), mlp_w_up, mlp_w_down, l, 512)
        zp = _gmlp_in(hp, row(a_norm[l]), w_in, 2 * tm, 512)
        hp = _gmlp_out(hp, zp, row(a_v_norm[l]), a_w_s[l], a_b_s[l].T, w_out, 256)
        hp = _mlp(hp, row(mlp_norm[l]), w_up, w_down, 2 * tm, 512)

    w_kv_b = w_kv.astype(BF16)
    cos_p, sin_p = _rope_tables(jnp.arange(t))
    cos_pp, sin_pp = jnp.tile(cos_p, (bp, 1)), jnp.tile(sin_p, (bp, 1))
    cos_s, sin_s = _rope_tables(jnp.full((bd,), past_len))
    paged_p, win_p, cmp_rows_p, kvb_p = _kv_proj(hp, row(kv_norm), w_kv_b, k_norm, cos_pp, sin_pp, tm)
    paged_s, win_s, _, _ = _kv_proj(hs, row(kv_norm), w_kv_b, k_norm, cos_s, sin_s, bd)

    j = 0
    wq = b_w_in[j].astype(BF16)
    wg = jnp.pad(b_w_in[j][:, d_q:], ((0, 0), (0, LANES - (b_w_in.shape[2] - d_q)))).astype(BF16)
    q_p, gates_p = _q_proj(hp, row(b_norm[j]), wq, wg, row(b_q_norm[j]), cos_pp, sin_pp, tm, d_q)
    q_s, gates_s = _q_proj(hs, row(b_norm[j]), wq, wg, row(b_q_norm[j]), cos_s, sin_s, bd, d_q)

    w1cat = jnp.concatenate([cmp_w1[:, :CMP_STRIDE], cmp_w1[:, CMP_STRIDE:]], axis=-1).astype(BF16)
    w1cat = w1cat.reshape(2, CMP_STRIDE * HEAD_DIM, 2 * HEAD_DIM)
    w1f = cmp_w1.reshape(2, CMP_BLOCK * HEAD_DIM, HEAD_DIM).astype(BF16)
    pe8 = jnp.broadcast_to(cmp_pe.reshape(2, 1, CMP_BLOCK * HEAD_DIM),
                           (2, SUBLANES, CMP_BLOCK * HEAD_DIM)).astype(BF16)
    w2 = cmp_w2.astype(BF16)

    n_sub_p = t // CMP_STRIDE
    n_slc_p = t // SLC_BLOCK
    cos_c, sin_c = _rope_tables(jnp.arange(n_sub_p) * CMP_STRIDE + CMP_BLOCK - 1)
    kc_p, vc_p = _compress_prompt(cmp_rows_p, w1cat, pe8, w1f, w2, k_norm, cos_c, sin_c, bp, t)
    ovt = _overlap_matrix(n_sub_p - 1, n_slc_p, n_sub_p, n_slc_p).T
    expand = jnp.where(jnp.arange(t)[:, None] // SLC_BLOCK == jnp.arange(LANES)[None, :], NEG, 0.0).astype(BF16)
    o_p = _attn_prompt(q_p, gates_p, kc_p, vc_p, kvb_p, ovt, expand, bp, t)
    w_o = b_w_out[j].astype(BF16)
    hp = _out_proj(hp, o_p, w_o, tm)

    tp_s = -(-(past_len + td) // KV_ALIGN) * KV_ALIGN
    n_sub_s = tp_s // CMP_STRIDE
    n_slc_s = tp_s // SLC_BLOCK
    n_pad_s = -(-n_sub_s // SUBLANES) * SUBLANES
    sel_lanes = -(-n_slc_s // LANES) * LANES
    assert past_len // SLC_BLOCK + 1 >= N_SELECT
    cos_cs, sin_cs = _rope_tables(jnp.arange(n_pad_s) * CMP_STRIDE + CMP_BLOCK - 1)
    ov_s = _overlap_matrix(n_sub_s - 1, n_slc_s, n_pad_s, sel_lanes)
    n_paged = 4 * N_KV
    n_win = 2 * N_KV
    cache_pages = cache_kv.reshape(n_pool, page_size * n_paged, HEAD_DIM)
    q3 = q_s.reshape(bd, d_q // HEAD_DIM, HEAD_DIM)
    new_kv = paged_s.reshape(bd, n_paged, HEAD_DIM)
    new_win = jnp.pad(win_s.reshape(bd, n_win, HEAD_DIM), ((0, 0), (0, SUBLANES - n_win), (0, 0)))
    o_c, sel_idx = _sample_cmp(page_table, cache_pages, new_kv, w1cat, pe8, w1f, w2, k_norm, cos_cs,
                               sin_cs, q3, ov_s, min(16, n_pages), past_len, n_slc_s)
    blocks_per_page = page_size // SLC_BLOCK
    cache_blocks = cache_kv.reshape(n_pool * blocks_per_page, SLC_BLOCK * n_paged, HEAD_DIM)
    state3 = state_kv_win.reshape(bd, w_keep * n_win, HEAD_DIM)
    o_s = _sample_attn(page_table, sel_idx[..., 0].reshape(bd, N_KV * N_SELECT), cache_blocks, new_kv,
                       new_win, state3, q3, gates_s.reshape(bd, 1, LANES), o_c, past_len,
                       past_len // SLC_BLOCK, blocks_per_page)
    hs = _out_proj(hs, o_s.reshape(bd, d_q), w_o, bd)

    l = n_a
    hs, w_up, w_down = _mlp_cast(hs, row(mlp_norm[l]), mlp_w_up, mlp_w_down, l, 512)
    hp = _mlp(hp, row(mlp_norm[l]), w_up, w_down, 2 * tm, 512)

    y_p = hp.reshape(bp, t, d)
    y_s = hs.reshape(bd, td, d)
    kv_p = paged_p.reshape(bp, t, 4, N_KV, HEAD_DIM)
    win_all_p = win_p.reshape(bp, t, 2, N_KV, HEAD_DIM)
    win_new_p = win_all_p[:, t - min(WINDOW, t):]
    kv_s = paged_s.reshape(bd, td, 4, N_KV, HEAD_DIM)
    win_new_s = jnp.concatenate(
        [state_kv_win[:, td:], win_s.reshape(bd, td, 2, N_KV, HEAD_DIM)], axis=1)
    v_a_s = jnp.stack(v_rows, axis=0)
    return (y_p, y_s, kv_p, win_new_p, kv_s, win_new_s, v_a_s)
```

```python
import functools

import jax
import jax.numpy as jnp
from jax import lax
from jax.experimental import pallas as pl
from jax.experimental.pallas import tpu as pltpu

F32 = jnp.float32
BF16 = jnp.bfloat16

CHUNK = 128
A_GROUPS = 16
HEAD_DIM = 128
N_KV = 2
CMP_STRIDE = 16
CMP_BLOCK = 2 * CMP_STRIDE
SLC_BLOCK = 64
N_SELECT = 16
WINDOW = 512
N_BRANCH = 3
KV_ALIGN = 64
ROT_DIM = HEAD_DIM // 4
ROPE_THETA = 500000.0
Q_BLOCK = 128
EPS = 1e-6
NEG = -1e30
FORCE_BONUS = 1e4
LOG2_E = 1.4426950408889634

LANES = 128
SUBLANES = 8
VMEM_LIMIT_BYTES = 56 * 1024 * 1024

KV_TILE = 512
N_KINDS = 6

_NT = (((1,), (1,)), ((), ()))


def _params(*sem):
    return pltpu.CompilerParams(dimension_semantics=sem, vmem_limit_bytes=VMEM_LIMIT_BYTES)


def _rms(x, g):
    ms = jnp.mean(x * x, axis=-1, keepdims=True)
    return x * lax.rsqrt(ms + EPS) * g


def _rope(x, cos, sin):
    half = ROT_DIM // 2
    lane = lax.broadcasted_iota(jnp.int32, x.shape, 1)
    partner = jnp.where(lane < half, pltpu.roll(x, LANES - half, 1), pltpu.roll(x, half, 1))
    return x * cos + partner * sin


def _rope_tables(pos):
    inv = ROPE_THETA ** (-jnp.arange(0, ROT_DIM, 2, dtype=F32) / ROT_DIM)
    ang = pos.astype(F32)[:, None] * inv[None, :]
    c, s = jnp.cos(ang), jnp.sin(ang)
    n = pos.shape[0]
    pad = LANES - ROT_DIM
    cos = jnp.concatenate([c, c, jnp.ones((n, pad), F32)], axis=1)
    sin = jnp.concatenate([-s, s, jnp.zeros((n, pad), F32)], axis=1)
    return cos, sin


def _gmlp_in_kernel(x_ref, g_ref, w_ref, z_ref, xn_ref):
    @pl.when(pl.program_id(1) == 0)
    def _():
        xn_ref[...] = _rms(x_ref[...], g_ref[...]).astype(BF16)

    z_ref[...] = jax.nn.gelu(jnp.dot(xn_ref[...], w_ref[...], preferred_element_type=F32)).astype(z_ref.dtype)


def _gmlp_in(x, g, w, tm, tn):
    m, d = x.shape
    n = w.shape[1]
    return pl.pallas_call(
        _gmlp_in_kernel,
        grid=(m // tm, n // tn),
        in_specs=[
            pl.BlockSpec((tm, d), lambda i, j: (i, 0)),
            pl.BlockSpec((1, d), lambda i, j: (0, 0)),
            pl.BlockSpec((d, tn), lambda i, j: (0, j)),
        ],
        out_specs=pl.BlockSpec((tm, tn), lambda i, j: (i, j)),
        out_shape=jax.ShapeDtypeStruct((m, n), BF16),
        scratch_shapes=[pltpu.VMEM((tm, d), BF16)],
        compiler_params=_params("parallel", "arbitrary"),
        name="gmlp_in",
    )(x, g, w)


def _gmlp_out_kernel(x_ref, u_ref, v_ref, vg_ref, ws_ref, bt_ref, wo_ref, o_ref, vn_ref, y_ref):
    tm = x_ref.shape[0]
    vn_ref[...] = _rms(v_ref[...].astype(F32), vg_ref[...]).astype(BF16)
    row = lax.broadcasted_iota(jnp.int32, (CHUNK, CHUNK), 0)
    col = lax.broadcasted_iota(jnp.int32, (CHUNK, CHUNK), 1)
    causal = row >= col
    for g in range(A_GROUPS):
        wsg = jnp.where(causal, ws_ref[g], 0.0).astype(BF16)
        bias = bt_ref[:, g:g + 1]
        cs = slice(g * LANES, (g + 1) * LANES)
        for c in range(tm // CHUNK):
            rs = slice(c * CHUNK, (c + 1) * CHUNK)
            mixed = jnp.dot(wsg, vn_ref[rs, cs], preferred_element_type=F32) + bias
            y_ref[rs, cs] = (u_ref[rs, cs].astype(F32) * mixed).astype(BF16)
    o_ref[...] = x_ref[...] + jnp.dot(y_ref[...], wo_ref[...], preferred_element_type=F32)


def _gmlp_out(x, z, vg, ws, bt, wo, tm):
    m, d = x.shape
    return pl.pallas_call(
        _gmlp_out_kernel,
        grid=(m // tm,),
        in_specs=[
            pl.BlockSpec((tm, d), lambda i: (i, 0)),
            pl.BlockSpec((tm, d), lambda i: (i, 0)),
            pl.BlockSpec((tm, d), lambda i: (i, 1)),
            pl.BlockSpec((1, d), lambda i: (0, 0)),
            pl.BlockSpec((A_GROUPS, CHUNK, CHUNK), lambda i: (0, 0, 0)),
            pl.BlockSpec((CHUNK, A_GROUPS), lambda i: (0, 0)),
            pl.BlockSpec((d, d), lambda i: (0, 0)),
        ],
        out_specs=pl.BlockSpec((tm, d), lambda i: (i, 0)),
        out_shape=jax.ShapeDtypeStruct((m, d), F32),
        scratch_shapes=[pltpu.VMEM((tm, d), BF16), pltpu.VMEM((tm, d), BF16)],
        compiler_params=_params("parallel"),
        name="gmlp_out",
    )(x, z, z, vg, ws, bt, wo)


def _gmlp_out_single_kernel(x_ref, u_ref, v_ref, vg_ref, wd_ref, bb_ref, wo_ref, o_ref, vn_ref):
    vn = _rms(v_ref[...].astype(F32), vg_ref[...])
    vn_ref[...] = vn
    mixed = vn * wd_ref[...] + bb_ref[...]
    y = (u_ref[...].astype(F32) * mixed).astype(BF16)
    o_ref[...] = x_ref[...] + jnp.dot(y, wo_ref[...], preferred_element_type=F32)


def _gmlp_out_single(x, z, vg, wd, bb, wo):
    m, d = x.shape
    full = lambda i: (0, 0)
    return pl.pallas_call(
        _gmlp_out_single_kernel,
        grid=(1,),
        in_specs=[
            pl.BlockSpec((m, d), full),
            pl.BlockSpec((m, d), lambda i: (0, 0)),
            pl.BlockSpec((m, d), lambda i: (0, 1)),
            pl.BlockSpec((1, d), full),
            pl.BlockSpec((1, d), full),
            pl.BlockSpec((1, d), full),
            pl.BlockSpec((d, d), full),
        ],
        out_specs=[pl.BlockSpec((m, d), full), pl.BlockSpec((m, d), full)],
        out_shape=[jax.ShapeDtypeStruct((m, d), F32), jax.ShapeDtypeStruct((m, d), F32)],
        compiler_params=_params("arbitrary"),
        name="gmlp_out_single",
    )(x, z, z, vg, wd, bb, wo)


def _mlp_kernel(x_ref, g_ref, wu_ref, wd_ref, o_ref, xn_ref):
    @pl.when(pl.program_id(1) == 0)
    def _():
        x = x_ref[...]
        xn_ref[...] = _rms(x, g_ref[...]).astype(BF16)
        o_ref[...] = x

    h = jnp.dot(xn_ref[...], wu_ref[...], preferred_element_type=F32)
    a = jnp.square(jnp.maximum(h, 0.0)).astype(BF16)
    o_ref[...] += jnp.dot(a, wd_ref[...], preferred_element_type=F32)


def _mlp(x, g, wu, wd, tm, tf):
    m, d = x.shape
    f = wu.shape[1]
    return pl.pallas_call(
        _mlp_kernel,
        grid=(m // tm, f // tf),
        in_specs=[
            pl.BlockSpec((tm, d), lambda i, j: (i, 0)),
            pl.BlockSpec((1, d), lambda i, j: (0, 0)),
            pl.BlockSpec((d, tf), lambda i, j: (0, j)),
            pl.BlockSpec((tf, d), lambda i, j: (j, 0)),
        ],
        out_specs=pl.BlockSpec((tm, d), lambda i, j: (i, 0)),
        out_shape=jax.ShapeDtypeStruct((m, d), F32),
        scratch_shapes=[pltpu.VMEM((tm, d), BF16)],
        compiler_params=_params("parallel", "arbitrary"),
        name="mlp",
    )(x, g, wu, wd)


def _mlp_cast_kernel(x_ref, g_ref, wu_ref, wd_ref, o_ref, wub_ref, wdb_ref, xn_ref):
    @pl.when(pl.program_id(0) == 0)
    def _():
        x = x_ref[...]
        xn_ref[...] = _rms(x, g_ref[...]).astype(BF16)
        o_ref[...] = x

    wu = wu_ref[...].astype(BF16)
    wd = wd_ref[...].astype(BF16)
    wub_ref[...] = wu
    wdb_ref[...] = wd
    h = jnp.dot(xn_ref[...], wu, preferred_element_type=F32)
    a = jnp.square(jnp.maximum(h, 0.0)).astype(BF16)
    o_ref[...] += jnp.dot(a, wd, preferred_element_type=F32)


def _mlp_cast(x, g, wu_all, wd_all, layer, tf):
    m, d = x.shape
    f = wu_all.shape[2]
    return pl.pallas_call(
        _mlp_cast_kernel,
        grid=(f // tf,),
        in_specs=[
            pl.BlockSpec((m, d), lambda j: (0, 0)),
            pl.BlockSpec((1, d), lambda j: (0, 0)),
            pl.BlockSpec((None, d, tf), lambda j: (layer, 0, j)),
            pl.BlockSpec((None, tf, d), lambda j: (layer, j, 0)),
        ],
        out_specs=[
            pl.BlockSpec((m, d), lambda j: (0, 0)),
            pl.BlockSpec((d, tf), lambda j: (0, j)),
            pl.BlockSpec((tf, d), lambda j: (j, 0)),
        ],
        out_shape=[
            jax.ShapeDtypeStruct((m, d), F32),
            jax.ShapeDtypeStruct((d, f), BF16),
            jax.ShapeDtypeStruct((f, d), BF16),
        ],
        scratch_shapes=[pltpu.VMEM((m, d), BF16)],
        compiler_params=_params("arbitrary"),
        name="mlp_cast",
    )(x, g, wu_all, wd_all)


def _kv_proj_kernel(x_ref, g_ref, w_ref, kn_ref, cos_ref, sin_ref, paged_ref, win_ref, cmp_ref, kvb_ref):
    tm = x_ref.shape[0]
    n_paged = 4 * N_KV
    n_win = 2 * N_KV
    xn = _rms(x_ref[...], g_ref[...]).astype(BF16)
    kv = jnp.dot(xn, w_ref[...], preferred_element_type=F32)
    cos, sin = cos_ref[...], sin_ref[...]
    for c in range(N_KINDS * N_KV):
        kind = c // N_KV
        h = kv[:, c * LANES:(c + 1) * LANES]
        if kind == 2:
            h = _rope(_rms(h, kn_ref[1:2, :]), cos, sin)
        elif kind == 4:
            h = _rope(_rms(h, kn_ref[2:3, :]), cos, sin)
        if kind < 4:
            paged_ref[pl.ds(c, tm, stride=n_paged), :] = h
        else:
            win_ref[pl.ds(c - n_paged, tm, stride=n_win), :] = h
        if kind < 2:
            cmp_ref[c] = h
        else:
            kvb_ref[c - 2 * N_KV] = h.astype(BF16)


def _kv_proj(x, g, w, kn, cos, sin, tm):
    m, d = x.shape
    n = w.shape[1]
    n_paged = 4 * N_KV
    n_win = 2 * N_KV
    return pl.pallas_call(
        _kv_proj_kernel,
        grid=(m // tm,),
        in_specs=[
            pl.BlockSpec((tm, d), lambda i: (i, 0)),
            pl.BlockSpec((1, d), lambda i: (0, 0)),
            pl.BlockSpec((d, n), lambda i: (0, 0)),
            pl.BlockSpec((3, HEAD_DIM), lambda i: (0, 0)),
            pl.BlockSpec((tm, LANES), lambda i: (i, 0)),
            pl.BlockSpec((tm, LANES), lambda i: (i, 0)),
        ],
        out_specs=[
            pl.BlockSpec((tm * n_paged, HEAD_DIM), lambda i: (i, 0)),
            pl.BlockSpec((tm * n_win, HEAD_DIM), lambda i: (i, 0)),
            pl.BlockSpec((2 * N_KV, tm, HEAD_DIM), lambda i: (0, i, 0)),
            pl.BlockSpec((4 * N_KV, tm, HEAD_DIM), lambda i: (0, i, 0)),
        ],
        out_shape=[
            jax.ShapeDtypeStruct((m * n_paged, HEAD_DIM), F32),
            jax.ShapeDtypeStruct((m * n_win, HEAD_DIM), F32),
            jax.ShapeDtypeStruct((2 * N_KV, m, HEAD_DIM), F32),
            jax.ShapeDtypeStruct((4 * N_KV, m, HEAD_DIM), BF16),
        ],
        compiler_params=_params("parallel"),
        name="kv_proj",
    )(x, g, w, kn, cos, sin)


def _q_proj_kernel(x_ref, g_ref, wq_ref, wg_ref, qn_ref, cos_ref, sin_ref, q_ref, gate_ref):
    xn = _rms(x_ref[...], g_ref[...]).astype(BF16)
    z = jnp.dot(xn, wq_ref[...], preferred_element_type=F32)
    cos, sin = cos_ref[...], sin_ref[...]
    qn = qn_ref[...]
    scale = HEAD_DIM ** -0.5 * LOG2_E
    for h in range(z.shape[1] // HEAD_DIM):
        cs = slice(h * HEAD_DIM, (h + 1) * HEAD_DIM)
        qh = _rope(_rms(z[:, cs], qn), cos, sin) * scale
        q_ref[:, cs] = qh.astype(BF16)
    gate_ref[...] = jax.nn.sigmoid(jnp.dot(xn, wg_ref[...], preferred_element_type=F32))


def _q_proj(x, g, wq, wg, qn, cos, sin, tm, n):
    m, d = x.shape
    return pl.pallas_call(
        _q_proj_kernel,
        grid=(m // tm,),
        in_specs=[
            pl.BlockSpec((tm, d), lambda i: (i, 0)),
            pl.BlockSpec((1, d), lambda i: (0, 0)),
            pl.BlockSpec((d, n), lambda i: (0, 0)),
            pl.BlockSpec((d, LANES), lambda i: (0, 0)),
            pl.BlockSpec((1, HEAD_DIM), lambda i: (0, 0)),
            pl.BlockSpec((tm, LANES), lambda i: (i, 0)),
            pl.BlockSpec((tm, LANES), lambda i: (i, 0)),
        ],
        out_specs=[
            pl.BlockSpec((tm, n), lambda i: (i, 0)),
            pl.BlockSpec((tm, LANES), lambda i: (i, 0)),
        ],
        out_shape=[
            jax.ShapeDtypeStruct((m, n), BF16),
            jax.ShapeDtypeStruct((m, LANES), F32),
        ],
        compiler_params=_params("parallel"),
        name="q_proj",
    )(x, g, wq, wg, qn, cos, sin)


def _out_proj_kernel(h_ref, o_ref, w_ref, y_ref):
    y_ref[...] = h_ref[...] + jnp.dot(o_ref[...], w_ref[...], preferred_element_type=F32)


def _out_proj(h, o, w, tm):
    m, d = h.shape
    k = o.shape[1]
    return pl.pallas_call(
        _out_proj_kernel,
        grid=(m // tm,),
        in_specs=[
            pl.BlockSpec((tm, d), lambda i: (i, 0)),
            pl.BlockSpec((tm, k), lambda i: (i, 0)),
            pl.BlockSpec((k, d), lambda i: (0, 0)),
        ],
        out_specs=pl.BlockSpec((tm, d), lambda i: (i, 0)),
        out_shape=jax.ShapeDtypeStruct((m, d), F32),
        compiler_params=_params("parallel"),
        name="out_proj",
    )(h, o, w)


def _cmp_bias(pe_ref, w1f_ref, kind):
    return jnp.dot(pe_ref[kind], w1f_ref[kind], preferred_element_type=F32)[0:1, :]


def _cmp_finish(fs, bias, w2, n_rows):
    first = fs[:, :HEAD_DIM]
    second = pltpu.roll(fs[:, HEAD_DIM:], n_rows - 1, 0)
    h = first + second + bias
    h = h * jax.nn.sigmoid(h)
    return jnp.dot(h.astype(BF16), w2, preferred_element_type=F32)


def _compress_prompt_kernel(krows_ref, vrows_ref, w1_ref, pe_ref, w1f_ref, w2_ref, kn_ref,
                            cos_ref, sin_ref, kc_ref, vc_ref):
    n_sub = krows_ref.shape[0] // CMP_STRIDE
    rowi = lax.broadcasted_iota(jnp.int32, (n_sub, HEAD_DIM), 0)
    for kind, rows_ref, out_ref in ((0, krows_ref, kc_ref), (1, vrows_ref, vc_ref)):
        x = jnp.concatenate(
            [rows_ref[pl.ds(r, n_sub, stride=CMP_STRIDE), :].astype(BF16) for r in range(CMP_STRIDE)],
            axis=1)
        fs = jnp.dot(x, w1_ref[kind], preferred_element_type=F32)
        out = _cmp_finish(fs, _cmp_bias(pe_ref, w1f_ref, kind), w2_ref[kind], n_sub)
        if kind == 0:
            out = _rope(_rms(out, kn_ref[0:1, :]), cos_ref[...], sin_ref[...])
        out_ref[...] = jnp.where(rowi < n_sub - 1, out, 0.0).astype(BF16)


def _compress_prompt(paged, w1cat, pe8, w1f, w2, kn, cos, sin, b, t):
    n_sub = t // CMP_STRIDE
    full = lambda nd: (lambda i, g: (0,) * nd)
    out_spec = pl.BlockSpec((None, None, n_sub, HEAD_DIM), lambda i, g: (i, g, 0, 0))
    out_shape = jax.ShapeDtypeStruct((b, N_KV, n_sub, HEAD_DIM), BF16)
    return pl.pallas_call(
        _compress_prompt_kernel,
        grid=(b, N_KV),
        in_specs=[
            pl.BlockSpec((None, t, HEAD_DIM), lambda i, g: (g, i, 0)),
            pl.BlockSpec((None, t, HEAD_DIM), lambda i, g: (N_KV + g, i, 0)),
            pl.BlockSpec(w1cat.shape, full(3)),
            pl.BlockSpec(pe8.shape, full(3)),
            pl.BlockSpec(w1f.shape, full(3)),
            pl.BlockSpec(w2.shape, full(3)),
            pl.BlockSpec(kn.shape, full(2)),
            pl.BlockSpec(cos.shape, full(2)),
            pl.BlockSpec(sin.shape, full(2)),
        ],
        out_specs=[out_spec, out_spec],
        out_shape=[out_shape, out_shape],
        compiler_params=_params("parallel", "parallel"),
        name="compress_prompt",
    )(paged, paged, w1cat, pe8, w1f, w2, kn, cos, sin)


def _split_bf16(x):
    hi = x.astype(BF16)
    lo = (x - hi.astype(F32)).astype(BF16)
    return hi, lo


def _overlap_matrix(n_cmp, n_slc, rows, cols):
    ci = jnp.arange(rows)[:, None] * CMP_STRIDE
    sj = jnp.arange(cols)[None, :] * SLC_BLOCK
    ov = (ci < sj + SLC_BLOCK) & (ci + CMP_BLOCK > sj)
    ov = ov & (jnp.arange(rows)[:, None] < n_cmp) & (jnp.arange(cols)[None, :] < n_slc)
    return ov.astype(BF16)


def _softmax_rows(s, valid):
    s = jnp.where(valid, s, NEG)
    e = jnp.exp2(s - jnp.max(s, axis=-1, keepdims=True))
    return e / jnp.sum(e, axis=-1, keepdims=True)


def _attn_prompt_kernel(q_ref, gate_ref, kc_ref, vc_ref, ks_ref, vs_ref, kw_ref, vw_ref,
                        ovt_ref, nexp_ref, o_ref, m_ref, acc_ref, s_ref, kext_ref, ow_ref, gate_b_ref):
    qb = pl.program_id(2)
    g = pl.program_id(1)
    nq = Q_BLOCK
    rep = q_ref.shape[1] // HEAD_DIM
    n_cmp_pad = kc_ref.shape[0]
    n_slc = ovt_ref.shape[0]
    t_len = ks_ref.shape[0]
    s0 = qb * nq

    q_all = q_ref[...]
    q2 = jnp.concatenate([q_all[:, r * HEAD_DIM:(r + 1) * HEAD_DIM] for r in range(rep)], axis=0)
    t_col = s0 + lax.broadcasted_iota(jnp.int32, (nq, 1), 0)

    gates = gate_ref[...]
    g_lane = lax.broadcasted_iota(jnp.int32, gates.shape, 1)
    for r in range(rep):
        for br in range(N_BRANCH):
            col = (g * rep + r) * N_BRANCH + br
            picked = jnp.sum(jnp.where(g_lane == col, gates, 0.0), axis=-1, keepdims=True)
            gate_b_ref[r * N_BRANCH + br] = jnp.broadcast_to(picked, (nq, LANES))

    slab = nq + WINDOW
    w0 = pl.multiple_of(jnp.maximum(s0 - WINDOW, 0), nq)
    kw = kw_ref[pl.ds(w0, slab), :]
    vw = jnp.concatenate([vw_ref[pl.ds(w0, slab), :], jnp.ones((slab, HEAD_DIM), BF16)], axis=1)
    dist = t_col - (w0 + lax.broadcasted_iota(jnp.int32, (nq, slab), 1))
    bias_w = jnp.where((dist >= 0) & (dist < WINDOW), 0.0, NEG)
    s_w = lax.dot_general(q2, kw, _NT, preferred_element_type=F32)
    p_parts = []
    for r in range(rep):
        s_r = s_w[r * nq:(r + 1) * nq] + bias_w
        p_parts.append(jnp.exp2(s_r - jnp.max(s_r, axis=-1, keepdims=True)).astype(BF16))
    o_w = jnp.dot(jnp.concatenate(p_parts, axis=0), vw, preferred_element_type=F32)
    ow_ref[...] = o_w[:, :HEAD_DIM] / o_w[:, HEAD_DIM:]

    kc = kc_ref[...]
    vc = vc_ref[...]
    cpos = lax.broadcasted_iota(jnp.int32, (1, n_cmp_pad), 1) * CMP_STRIDE + (CMP_BLOCK - 1)
    m_c = cpos <= t_col
    s_c = lax.dot_general(q2, kc, _NT, preferred_element_type=F32)
    psum = jnp.zeros((nq, n_cmp_pad), F32)
    p_parts = []
    for r in range(rep):
        p = jnp.where(m_c, _softmax_rows(s_c[r * nq:(r + 1) * nq], m_c), 0.0)
        psum = psum + p
        p_parts.append(p.astype(BF16))
    o_c = jnp.dot(jnp.concatenate(p_parts, axis=0), vc, preferred_element_type=F32)

    ovt = ovt_ref[...]
    p_hi, p_lo = _split_bf16(psum)
    imp_t = (lax.dot_general(ovt, p_hi, _NT, preferred_element_type=F32)
             + lax.dot_general(ovt, p_lo, _NT, preferred_element_type=F32))
    t_row = s0 + lax.broadcasted_iota(jnp.int32, (n_slc, nq), 1)
    blk = lax.broadcasted_iota(jnp.int32, (n_slc, nq), 0)
    cur = t_row // SLC_BLOCK
    causal = blk * SLC_BLOCK <= t_row
    forced = (blk == 0) | (blk == cur) | (blk == cur - 1)
    score = jnp.where(causal, imp_t + jnp.where(forced, FORCE_BONUS, 0.0), NEG)
    rank = jnp.zeros((n_slc, nq), jnp.int32)
    for i in range(n_slc):
        si = score[i:i + 1, :]
        later = (blk > i).astype(jnp.int32)
        rank = rank + jnp.where(si > score, 1, 0) + jnp.where(si == score, later, 0)
    sel_t = jnp.where(causal & (rank < N_SELECT), 1.0, 0.0)
    sel_t = jnp.concatenate([sel_t, jnp.zeros((LANES - n_slc, nq), F32)], axis=0)
    unsel = 1.0 - sel_t.T

    @pl.when(qb == 0)
    def _():
        kext_ref[:, :HEAD_DIM] = ks_ref[...]
        kext_ref[:, HEAD_DIM:] = nexp_ref[...]

    blk_lane = lax.broadcasted_iota(jnp.int32, (nq, LANES), 1)
    unsel_main = jnp.where(blk_lane >= s0 // SLC_BLOCK, 1.0, unsel).astype(BF16)
    q_ext = jnp.concatenate([q2, jnp.concatenate([unsel_main] * rep, axis=0)], axis=1)
    n_main = (s0 + KV_TILE - 1) // KV_TILE
    m_ref[...] = jnp.full(m_ref.shape, NEG, F32)

    def sweep1(off, width):
        s = lax.dot_general(q_ext, kext_ref[pl.ds(off, width), :], _NT, preferred_element_type=F32)
        s_ref[:, pl.ds(off, width)] = s
        m_run = m_ref[...]
        for c in range(width // LANES):
            m_run = jnp.maximum(m_run, s[:, c * LANES:(c + 1) * LANES])
        m_ref[...] = m_run

    def run_sweep(sweep):
        def quad(j, carry):
            sweep(pl.multiple_of(j * 4 * KV_TILE, 4 * KV_TILE), 4 * KV_TILE)
            return carry

        lax.fori_loop(0, n_main // 4, quad, 0)

        @pl.when(n_main % 4 >= 2)
        def _():
            sweep(pl.multiple_of((n_main // 4) * 4 * KV_TILE, 2 * KV_TILE), 2 * KV_TILE)

        @pl.when(n_main % 2 == 1)
        def _():
            sweep(pl.multiple_of((n_main - 1) * KV_TILE, KV_TILE), KV_TILE)

    run_sweep(sweep1)

    d0 = pl.multiple_of(s0, nq)
    q_i = lax.broadcasted_iota(jnp.int32, (nq, nq), 0)
    k_i = lax.broadcasted_iota(jnp.int32, (nq, nq), 1)
    bias_d = (lax.dot_general(unsel.astype(BF16), nexp_ref[pl.ds(d0, nq), :], _NT,
                              preferred_element_type=F32)
              + jnp.where(k_i <= q_i, 0.0, NEG))
    s_d = lax.dot_general(q2, ks_ref[pl.ds(d0, nq), :], _NT, preferred_element_type=F32)
    s_d = (s_d.reshape(rep, nq, nq) + bias_d[None]).reshape(rep * nq, nq)
    m_all = jnp.maximum(m_ref[...], s_d)
    m_b = jnp.broadcast_to(jnp.max(m_all, axis=-1, keepdims=True), m_all.shape)
    m_ref[...] = m_b
    v_d = jnp.concatenate([vs_ref[pl.ds(d0, nq), :], jnp.ones((nq, HEAD_DIM), BF16)], axis=1)
    acc_ref[...] = jnp.dot(jnp.exp2(s_d - m_b).astype(BF16), v_d, preferred_element_type=F32)

    def sweep2(off, width):
        v1 = jnp.concatenate([vs_ref[pl.ds(off, width), :], jnp.ones((width, HEAD_DIM), BF16)], axis=1)
        m_rows = m_ref[...]
        p = jnp.concatenate(
            [jnp.exp2(s_ref[:, pl.ds(pl.multiple_of(off + c * LANES, LANES), LANES)] - m_rows).astype(BF16)
             for c in range(width // LANES)], axis=1)
        acc_ref[...] += jnp.dot(p, v1, preferred_element_type=F32)

    run_sweep(sweep2)

    o_s = acc_ref[:, :HEAD_DIM] / acc_ref[:, HEAD_DIM:]

    for r in range(rep):
        rs = slice(r * nq, (r + 1) * nq)
        out = (gate_b_ref[r * N_BRANCH] * o_c[rs] + gate_b_ref[r * N_BRANCH + 1] * o_s[rs]
               + gate_b_ref[r * N_BRANCH + 2] * ow_ref[rs, :])
        o_ref[:, r * HEAD_DIM:(r + 1) * HEAD_DIM] = out.astype(BF16)


def _attn_prompt(q, gates, kc, vc, kvb, ovt, expand, b, t):
    nqb = t // Q_BLOCK
    rep = q.shape[1] // HEAD_DIM // N_KV
    n_cmp_pad = kc.shape[2]
    rows = lambda kind: pl.BlockSpec((None, t, HEAD_DIM), lambda i, g, j: ((kind - 2) * N_KV + g, i, 0))
    cmp_spec = pl.BlockSpec((None, None, n_cmp_pad, HEAD_DIM), lambda i, g, j: (i, g, 0, 0))
    return pl.pallas_call(
        _attn_prompt_kernel,
        grid=(b, N_KV, nqb),
        in_specs=[
            pl.BlockSpec((Q_BLOCK, rep * HEAD_DIM), lambda i, g, j: (i * nqb + j, g)),
            pl.BlockSpec((Q_BLOCK, LANES), lambda i, g, j: (i * nqb + j, 0)),
            cmp_spec, cmp_spec,
            rows(2), rows(3), rows(4), rows(5),
            pl.BlockSpec(ovt.shape, lambda i, g, j: (0, 0)),
            pl.BlockSpec(expand.shape, lambda i, g, j: (0, 0)),
        ],
        out_specs=pl.BlockSpec((Q_BLOCK, rep * HEAD_DIM), lambda i, g, j: (i * nqb + j, g)),
        out_shape=jax.ShapeDtypeStruct(q.shape, BF16),
        scratch_shapes=[
            pltpu.VMEM((rep * Q_BLOCK, LANES), F32),
            pltpu.VMEM((rep * Q_BLOCK, 2 * HEAD_DIM), F32),
            pltpu.VMEM((rep * Q_BLOCK, t), F32),
            pltpu.VMEM((t, 2 * HEAD_DIM), BF16),
            pltpu.VMEM((rep * Q_BLOCK, HEAD_DIM), F32),
            pltpu.VMEM((rep * N_BRANCH, Q_BLOCK, LANES), F32),
        ],
        compiler_params=_params("parallel", "parallel", "arbitrary"),
        name="attn_prompt",
    )(q, gates, kc, vc, kvb, kvb, kvb, kvb, ovt, expand)


def _sample_cmp_kernel(pt_ref, *refs, pages_per_step, n_steps, t_pos, n_slc):
    del pt_ref
    page_refs = refs[:pages_per_step]
    (new_ref, w1_ref, pe_ref, w1f_ref, w2_ref, kn_ref, cos_ref, sin_ref, q_ref, ov_ref,
     oc_ref, idx_ref, fs_ref, stage_ref) = refs[pages_per_step:]
    step = pl.program_id(1)
    n_heads = N_KV * 2
    heads_per_row = 4 * N_KV
    page_rows = page_refs[0].shape[0] // heads_per_row
    sub_per_page = page_rows // CMP_STRIDE
    rows_per_step = pages_per_step * sub_per_page
    n_pad = fs_ref.shape[1]
    n_past = n_steps * rows_per_step
    row0 = pl.multiple_of(step * rows_per_step, rows_per_step)

    assert heads_per_row == SUBLANES
    slab = stage_ref.shape[0] // heads_per_row
    for k, p in enumerate(page_refs):
        for n in range(sub_per_page):
            row = k * sub_per_page + n
            for r in range(CMP_STRIDE):
                tok = p[pl.ds((n * CMP_STRIDE + r) * heads_per_row, heads_per_row), :]
                tile = (row // SUBLANES) * CMP_STRIDE + r
                stage_ref[pl.ds(tile * SUBLANES + row % SUBLANES, heads_per_row, stride=slab), :] = tok
    for c in range(n_heads):
        x = jnp.concatenate(
            [jnp.concatenate(
                [stage_ref[pl.ds(c * slab + (i * CMP_STRIDE + r) * SUBLANES, SUBLANES), :]
                 for r in range(CMP_STRIDE)], axis=1)
             for i in range(rows_per_step // SUBLANES)], axis=0)
        fs_ref[c, pl.ds(row0, rows_per_step), :] = jnp.dot(
            x.astype(BF16), w1_ref[c // N_KV], preferred_element_type=F32)

    @pl.when(step == n_steps - 1)
    def _():
        tail = n_pad - n_past
        row_t = lax.broadcasted_iota(jnp.int32, (tail, HEAD_DIM), 0)
        rowi = lax.broadcasted_iota(jnp.int32, (n_pad, HEAD_DIM), 0)
        n_sub = (t_pos + 1 + KV_ALIGN - 1) // KV_ALIGN * KV_ALIGN // CMP_STRIDE
        outs = []
        for c in range(n_heads):
            kind = c // N_KV
            new_row = new_ref[c:c + 1, :]
            x_tail = jnp.where(row_t == 0, new_row, 0.0).astype(BF16)
            fs_ref[c, n_past:n_pad, :] = jnp.dot(x_tail, w1_ref[kind, 0:HEAD_DIM, :],
                                                 preferred_element_type=F32)
            out = _cmp_finish(fs_ref[c], _cmp_bias(pe_ref, w1f_ref, kind), w2_ref[kind], n_pad)
            if kind == 0:
                out = _rope(_rms(out, kn_ref[0:1, :]), cos_ref[...], sin_ref[...])
            outs.append(jnp.where(rowi < n_sub - 1, out, 0.0).astype(BF16))

        q = q_ref[...]
        n_q = q.shape[0]
        rep = n_q // N_KV
        head = lax.broadcasted_iota(jnp.int32, (n_q, 1), 0)
        cpos = lax.broadcasted_iota(jnp.int32, (1, n_pad), 1) * CMP_STRIDE + (CMP_BLOCK - 1)
        m_c = cpos <= t_pos
        lanes = ov_ref.shape[1]
        blk = lax.broadcasted_iota(jnp.int32, (1, lanes), 1)
        cur = t_pos // SLC_BLOCK
        causal = (blk * SLC_BLOCK <= t_pos) & (blk < n_slc)
        forced = (blk == 0) | (blk == cur) | (blk == cur - 1)
        eye_i = lax.broadcasted_iota(jnp.int32, (lanes, lanes), 0)
        eye_j = lax.broadcasted_iota(jnp.int32, (lanes, lanes), 1)
        o_c = jnp.zeros((n_q, HEAD_DIM), F32)
        for g in range(N_KV):
            kc, vc = outs[g], outs[N_KV + g]
            in_group = (head // rep) == g
            s = lax.dot_general(q, kc, _NT, preferred_element_type=F32)
            p = jnp.where(m_c, _softmax_rows(s, m_c), 0.0)
            o_g = jnp.dot(p.astype(BF16), vc, preferred_element_type=F32)
            o_c = jnp.where(in_group, o_g, o_c)
            psum = jnp.sum(jnp.where(in_group, p, 0.0), axis=0, keepdims=True)
            psum8 = jnp.broadcast_to(psum, (SUBLANES, n_pad))
            p_hi, p_lo = _split_bf16(psum8)
            imp = (jnp.dot(p_hi, ov_ref[...], preferred_element_type=F32)
                   + jnp.dot(p_lo, ov_ref[...], preferred_element_type=F32))[0:1, :]
            score = jnp.where(causal, imp + jnp.where(forced, FORCE_BONUS, 0.0), NEG)
            score_b = jnp.broadcast_to(score, (lanes, lanes))
            score_col = jnp.sum(jnp.where(eye_i == eye_j, score_b, 0.0), axis=1, keepdims=True)
            beats = (score_col > score_b) | ((score_col == score_b) & (eye_i < eye_j))
            rank = jnp.sum(jnp.where(beats, 1.0, 0.0), axis=0, keepdims=True)
            sel = jnp.where(causal & (rank < N_SELECT), 1.0, 0.0)
            sel_b = jnp.broadcast_to(sel, (lanes, lanes))
            sel_col = jnp.sum(jnp.where(eye_i == eye_j, sel_b, 0.0), axis=1, keepdims=True)
            slot = jnp.sum(jnp.where(eye_i < eye_j, sel_col, 0.0), axis=0, keepdims=True)
            slot_b = jnp.broadcast_to(slot, (N_SELECT, lanes))
            k_i = lax.broadcasted_iota(jnp.int32, (N_SELECT, lanes), 0).astype(F32)
            j_i = lax.broadcasted_iota(jnp.int32, (N_SELECT, lanes), 1).astype(F32)
            hit = (slot_b == k_i) & (jnp.broadcast_to(sel, (N_SELECT, lanes)) > 0.5)
            idx = jnp.sum(jnp.where(hit, j_i, 0.0), axis=1, keepdims=True)
            idx_ref[g] = jnp.broadcast_to(idx, (N_SELECT, LANES)).astype(jnp.int32)
        oc_ref[...] = o_c


def _sample_cmp(page_table, cache_pages, new_rows, w1cat, pe8, w1f, w2, kn, cos, sin, q3, ov,
                pages_per_step, t_pos, n_slc):
    bd, n_pages = page_table.shape
    n_steps = n_pages // pages_per_step
    flat_rows = cache_pages.shape[1]
    page_rows = flat_rows // (4 * N_KV)
    n_pad = cos.shape[0]
    n_q = q3.shape[1]
    full = lambda nd: (lambda i, s, pt: (0,) * nd)

    n_pool = cache_pages.shape[0]

    def page_spec(k):
        return pl.BlockSpec(
            (None, flat_rows, HEAD_DIM),
            lambda i, s, pt: (jnp.clip(pt[i, s * pages_per_step + k], 0, n_pool - 1), 0, 0))

    grid_spec = pltpu.PrefetchScalarGridSpec(
        num_scalar_prefetch=1,
        grid=(bd, n_steps),
        in_specs=[page_spec(k) for k in range(pages_per_step)] + [
            pl.BlockSpec((None,) + new_rows.shape[1:], lambda i, s, pt: (i, 0, 0)),
            pl.BlockSpec(w1cat.shape, full(3)),
            pl.BlockSpec(pe8.shape, full(3)),
            pl.BlockSpec(w1f.shape, full(3)),
            pl.BlockSpec(w2.shape, full(3)),
            pl.BlockSpec(kn.shape, full(2)),
            pl.BlockSpec(cos.shape, full(2)),
            pl.BlockSpec(sin.shape, full(2)),
            pl.BlockSpec((None, n_q, HEAD_DIM), lambda i, s, pt: (i, 0, 0)),
            pl.BlockSpec(ov.shape, full(2)),
        ],
        out_specs=[
            pl.BlockSpec((None, n_q, HEAD_DIM), lambda i, s, pt: (i, 0, 0)),
            pl.BlockSpec((None, N_KV, N_SELECT, LANES), lambda i, s, pt: (i, 0, 0, 0)),
        ],
        scratch_shapes=[pltpu.VMEM((N_KV * 2, n_pad, 2 * HEAD_DIM), F32),
                        pltpu.VMEM((4 * N_KV * (pages_per_step * page_rows + 4), HEAD_DIM), F32)],
    )
    kern = functools.partial(_sample_cmp_kernel, pages_per_step=pages_per_step, n_steps=n_steps,
                             t_pos=t_pos, n_slc=n_slc)
    return pl.pallas_call(
        kern,
        grid_spec=grid_spec,
        out_shape=[
            jax.ShapeDtypeStruct((bd, n_q, HEAD_DIM), F32),
            jax.ShapeDtypeStruct((bd, N_KV, N_SELECT, LANES), jnp.int32),
        ],
        compiler_params=_params("parallel", "arbitrary"),
        name="sample_cmp",
    )(page_table, *([cache_pages] * pages_per_step), new_rows, w1cat, pe8, w1f, w2, kn, cos, sin, q3, ov)


def _sample_attn_kernel(pt_ref, sel_ref, *refs, t_pos, n_past_blocks):
    del pt_ref
    n_blk = N_KV * N_SELECT
    blk_refs = refs[:n_blk]
    new_kv_ref, new_win_ref, state_ref, q_ref, gate_ref, oc_ref, o_ref = refs[n_blk:]
    n_paged = 4 * N_KV
    n_win = 2 * N_KV
    b = pl.program_id(0)
    q = q_ref[...]
    qf = q.astype(F32)
    n_q = q.shape[0]
    rep = n_q // N_KV
    head = lax.broadcasted_iota(jnp.int32, (n_q, 1), 0)
    n_keys = N_SELECT * SLC_BLOCK
    lane = lax.broadcasted_iota(jnp.int32, (1, n_keys), 1)
    row_b = lax.broadcasted_iota(jnp.int32, (SLC_BLOCK, HEAD_DIM), 0)
    w_keep = state_ref.shape[0] // n_win
    o_s = jnp.zeros((n_q, HEAD_DIM), F32)
    o_w = jnp.zeros((n_q, HEAD_DIM), F32)
    for g in range(N_KV):
        in_group = (head // rep) == g
        k_head = 2 * N_KV + g
        v_head = 3 * N_KV + g
        tail_k = jnp.where(row_b == 0, new_kv_ref[k_head:k_head + 1, :], 0.0)
        tail_v = jnp.where(row_b == 0, new_kv_ref[v_head:v_head + 1, :], 0.0)
        k_parts, v_parts = [], []
        base = jnp.zeros((1, n_keys), jnp.int32)
        for k in range(N_SELECT):
            blk = sel_ref[b, g * N_SELECT + k]
            is_tail = blk >= n_past_blocks
            blk_ref = blk_refs[g * N_SELECT + k]
            k_rows = blk_ref[pl.ds(k_head, SLC_BLOCK, stride=n_paged), :]
            v_rows = blk_ref[pl.ds(v_head, SLC_BLOCK, stride=n_paged), :]
            k_parts.append(jnp.where(is_tail, tail_k, k_rows).astype(BF16))
            v_parts.append(jnp.where(is_tail, tail_v, v_rows).astype(BF16))
            base = jnp.where(lane // SLC_BLOCK == k, blk * SLC_BLOCK, base)
        keys = jnp.concatenate(k_parts, axis=0)
        vals = jnp.concatenate(v_parts, axis=0)
        tok = base + lane % SLC_BLOCK
        s = lax.dot_general(q, keys, _NT, preferred_element_type=F32)
        p = _softmax_rows(s, tok <= t_pos)
        o_s = jnp.where(in_group, jnp.dot(p.astype(BF16), vals, preferred_element_type=F32), o_s)

        kw = state_ref[pl.ds(g, w_keep, stride=n_win), :].astype(BF16)
        vw = state_ref[pl.ds(N_KV + g, w_keep, stride=n_win), :].astype(BF16)
        kw_new = new_win_ref[g:g + 1, :]
        vw_new = new_win_ref[N_KV + g:N_KV + g + 1, :]
        dist = w_keep - lax.broadcasted_iota(jnp.int32, (1, w_keep), 1)
        m_w = (dist >= 0) & (dist < WINDOW) & (t_pos - dist >= 0)
        s_w = jnp.where(m_w, lax.dot_general(q, kw, _NT, preferred_element_type=F32), NEG)
        s_new = jnp.sum(qf * kw_new, axis=-1, keepdims=True)
        m = jnp.maximum(jnp.max(s_w, axis=-1, keepdims=True), s_new)
        e_w = jnp.exp2(s_w - m)
        e_new = jnp.exp2(s_new - m)
        denom = jnp.sum(e_w, axis=-1, keepdims=True) + e_new
        num = jnp.dot(e_w.astype(BF16), vw, preferred_element_type=F32) + e_new * vw_new
        o_w = jnp.where(in_group, num / denom, o_w)

    gates = jnp.broadcast_to(gate_ref[...], (n_q, LANES))
    glane = lax.broadcasted_iota(jnp.int32, (n_q, LANES), 1)
    gsel = [jnp.sum(jnp.where(glane == head * N_BRANCH + br, gates, 0.0), axis=-1, keepdims=True)
            for br in range(N_BRANCH)]
    o_ref[...] = (gsel[0] * oc_ref[...] + gsel[1] * o_s + gsel[2] * o_w).astype(BF16)


def _sample_attn(page_table, sel_idx, cache_blocks, new_kv, new_win, state3, q3, gates3, o_c,
                 t_pos, n_past_blocks, blocks_per_page):
    bd = page_table.shape[0]
    n_q = q3.shape[1]

    n_pool = cache_blocks.shape[0] // blocks_per_page

    def blk_spec(g, k):
        def index(i, pt, sel):
            blk = jnp.clip(sel[i, g * N_SELECT + k], 0, n_past_blocks - 1)
            page = jnp.clip(pt[i, blk // blocks_per_page], 0, n_pool - 1)
            return (page * blocks_per_page + blk % blocks_per_page, 0, 0)
        return pl.BlockSpec((None,) + cache_blocks.shape[1:], index)

    blk_specs = [blk_spec(g, k) for g in range(N_KV) for k in range(N_SELECT)]
    per_b = lambda shape: pl.BlockSpec((None,) + shape, lambda i, pt, sel: (i, 0, 0))
    grid_spec = pltpu.PrefetchScalarGridSpec(
        num_scalar_prefetch=2,
        grid=(bd,),
        in_specs=blk_specs + [
            per_b(new_kv.shape[1:]),
            per_b(new_win.shape[1:]),
            per_b(state3.shape[1:]),
            per_b((n_q, HEAD_DIM)),
            per_b((1, LANES)),
            per_b((n_q, HEAD_DIM)),
        ],
        out_specs=per_b((n_q, HEAD_DIM)),
    )
    kern = functools.partial(_sample_attn_kernel, t_pos=t_pos, n_past_blocks=n_past_blocks)
    n_blk = N_KV * N_SELECT
    return pl.pallas_call(
        kern,
        grid_spec=grid_spec,
        out_shape=jax.ShapeDtypeStruct((bd, n_q, HEAD_DIM), BF16),
        compiler_params=_params("parallel"),
        name="sample_attn",
    )(page_table, sel_idx, *([cache_blocks] * n_blk), new_kv, new_win, state3, q3, gates3, o_c)


def kernel(x_prompt, x_sample, cache_kv, state_kv_win, page_table, a_norm, a_w_in, a_v_norm, a_w_s,
           a_b_s, a_w_out, mlp_norm, mlp_w_up, mlp_w_down, kv_norm, w_kv, cmp_pe, cmp_w1, cmp_w2,
           k_norm, b_norm, b_w_in, b_q_norm, b_w_out):
    bp, t, d = x_prompt.shape
    bd, td, _ = x_sample.shape
    n_pool, page_size = cache_kv.shape[:2]
    n_pages = page_table.shape[1]
    past_len = n_pages * page_size
    w_keep = state_kv_win.shape[1]
    depth = mlp_norm.shape[0]
    n_a = a_norm.shape[0]
    n_b = b_norm.shape[0]
    d_q = b_w_out.shape[1]
    assert td == 1 and n_b == 1 and depth == n_a + n_b
    assert d // A_GROUPS == LANES and t % KV_TILE == 0 and t >= Q_BLOCK + WINDOW
    assert w_keep == WINDOW and page_size % SLC_BLOCK == 0 and past_len % KV_ALIGN == 0

    row = lambda v: v.reshape(1, -1)
    hp = x_prompt.reshape(bp * t, d)
    hs = x_sample.reshape(bd, d)
    tm = 512

    v_rows = []
    for l in range(n_a):
        w_in = a_w_in[l].astype(BF16)
        w_out = a_w_out[l].astype(BF16)
        zs = _gmlp_in(hs, row(a_norm[l]), w_in, bd, 512)
        wd = jnp.repeat(a_w_s[l][:, 0, 0], d // A_GROUPS).reshape(1, d)
        bb = jnp.repeat(a_b_s[l][:, 0], d // A_GROUPS).reshape(1, d)
        hs, v_s = _gmlp_out_single(hs, zs, row(a_v_norm[l]), wd, bb, w_out)
        v_rows.append(v_s.reshape(bd, td, d))
        hs, w_up, w_down = _mlp_cast(hs, row(mlp_norm[l]), mlp_w_up, mlp_w_down, l, 512)
        zp = _gmlp_in(hp, row(a_norm[l]), w_in, 2 * tm, 1024)
        hp = _gmlp_out(hp, zp, row(a_v_norm[l]), a_w_s[l], a_b_s[l].T, w_out, 256)
        hp = _mlp(hp, row(mlp_norm[l]), w_up, w_down, 2 * tm, 512)

    w_kv_b = w_kv.astype(BF16)
    cos_p, sin_p = _rope_tables(jnp.arange(t))
    cos_pp, sin_pp = jnp.tile(cos_p, (bp, 1)), jnp.tile(sin_p, (bp, 1))
    cos_s, sin_s = _rope_tables(jnp.full((bd,), past_len))
    paged_p, win_p, cmp_rows_p, kvb_p = _kv_proj(hp, row(kv_norm), w_kv_b, k_norm, cos_pp, sin_pp, tm)
    paged_s, win_s, _, _ = _kv_proj(hs, row(kv_norm), w_kv_b, k_norm, cos_s, sin_s, bd)

    j = 0
    wq = b_w_in[j].astype(BF16)
    wg = jnp.pad(b_w_in[j][:, d_q:], ((0, 0), (0, LANES - (b_w_in.shape[2] - d_q)))).astype(BF16)
    q_p, gates_p = _q_proj(hp, row(b_norm[j]), wq, wg, row(b_q_norm[j]), cos_pp, sin_pp, tm, d_q)
    q_s, gates_s = _q_proj(hs, row(b_norm[j]), wq, wg, row(b_q_norm[j]), cos_s, sin_s, bd, d_q)

    w1cat = jnp.concatenate([cmp_w1[:, :CMP_STRIDE], cmp_w1[:, CMP_STRIDE:]], axis=-1).astype(BF16)
    w1cat = w1cat.reshape(2, CMP_STRIDE * HEAD_DIM, 2 * HEAD_DIM)
    w1f = cmp_w1.reshape(2, CMP_BLOCK * HEAD_DIM, HEAD_DIM).astype(BF16)
    pe8 = jnp.broadcast_to(cmp_pe.reshape(2, 1, CMP_BLOCK * HEAD_DIM),
                           (2, SUBLANES, CMP_BLOCK * HEAD_DIM)).astype(BF16)
    w2 = cmp_w2.astype(BF16)

    n_sub_p = t // CMP_STRIDE
    n_slc_p = t // SLC_BLOCK
    cos_c, sin_c = _rope_tables(jnp.arange(n_sub_p) * CMP_STRIDE + CMP_BLOCK - 1)
    kc_p, vc_p = _compress_prompt(cmp_rows_p, w1cat, pe8, w1f, w2, k_norm, cos_c, sin_c, bp, t)
    ovt = _overlap_matrix(n_sub_p - 1, n_slc_p, n_sub_p, n_slc_p).T
    expand = jnp.where(jnp.arange(t)[:, None] // SLC_BLOCK == jnp.arange(LANES)[None, :], NEG, 0.0).astype(BF16)
    o_p = _attn_prompt(q_p, gates_p, kc_p, vc_p, kvb_p, ovt, expand, bp, t)
    w_o = b_w_out[j].astype(BF16)
    hp = _out_proj(hp, o_p, w_o, tm)

    tp_s = -(-(past_len + td) // KV_ALIGN) * KV_ALIGN
    n_sub_s = tp_s // CMP_STRIDE
    n_slc_s = tp_s // SLC_BLOCK
    n_pad_s = -(-n_sub_s // SUBLANES) * SUBLANES
    sel_lanes = -(-n_slc_s // LANES) * LANES
    assert past_len // SLC_BLOCK + 1 >= N_SELECT
    cos_cs, sin_cs = _rope_tables(jnp.arange(n_pad_s) * CMP_STRIDE + CMP_BLOCK - 1)
    ov_s = _overlap_matrix(n_sub_s - 1, n_slc_s, n_pad_s, sel_lanes)
    n_paged = 4 * N_KV
    n_win = 2 * N_KV
    cache_pages = cache_kv.reshape(n_pool, page_size * n_paged, HEAD_DIM)
    q3 = q_s.reshape(bd, d_q // HEAD_DIM, HEAD_DIM)
    new_kv = paged_s.reshape(bd, n_paged, HEAD_DIM)
    new_win = jnp.pad(win_s.reshape(bd, n_win, HEAD_DIM), ((0, 0), (0, SUBLANES - n_win), (0, 0)))
    o_c, sel_idx = _sample_cmp(page_table, cache_pages, new_kv, w1cat, pe8, w1f, w2, k_norm, cos_cs,
                               sin_cs, q3, ov_s, min(16, n_pages), past_len, n_slc_s)
    blocks_per_page = page_size // SLC_BLOCK
    cache_blocks = cache_kv.reshape(n_pool * blocks_per_page, SLC_BLOCK * n_paged, HEAD_DIM)
    state3 = state_kv_win.reshape(bd, w_keep * n_win, HEAD_DIM)
    o_s = _sample_attn(page_table, sel_idx[..., 0].reshape(bd, N_KV * N_SELECT), cache_blocks, new_kv,
                       new_win, state3, q3, gates_s.reshape(bd, 1, LANES), o_c, past_len,
                       past_len // SLC_BLOCK, blocks_per_page)
    hs = _out_proj(hs, o_s.reshape(bd, d_q), w_o, bd)

    l = n_a
    hs, w_up, w_down = _mlp_cast(hs, row(mlp_norm[l]), mlp_w_up, mlp_w_down, l, 512)
    hp = _mlp(hp, row(mlp_norm[l]), w_up, w_down, 2 * tm, 512)

    y_p = hp.reshape(bp, t, d)
    y_s = hs.reshape(bd, td, d)
    kv_p = paged_p.reshape(bp, t, 4, N_KV, HEAD_DIM)
    win_all_p = win_p.reshape(bp, t, 2, N_KV, HEAD_DIM)
    win_new_p = win_all_p[:, t - min(WINDOW, t):]
    kv_s = paged_s.reshape(bd, td, 4, N_KV, HEAD_DIM)
    win_new_s = jnp.concatenate(
        [state_kv_win[:, td:], win_s.reshape(bd, td, 2, N_KV, HEAD_DIM)], axis=1)
    v_a_s = jnp.stack(v_rows, axis=0)
    return (y_p, y_s, kv_p, win_new_p, kv_s, win_new_s, v_a_s)
```
